```python
import math
import jax
import jax.numpy as jnp
from jax import lax
import numpy as np

D_MODEL = 1024
BATCH = 4
SEQ = 4096
DEPTH = 4
DEC_BATCH = 16
DEC_SEQ = 32
PAST_LEN = 4096

CHUNK = 64
Q_BLOCK = 128
HEAD_DIM = 64
FOX_HEADS = 8
DIFF_HEADS = 4
RWKV_HEADS = 8
FOX_W = FOX_HEADS * HEAD_DIM
DIFF_W = DIFF_HEADS * 2 * HEAD_DIM
RWKV_W = RWKV_HEADS * HEAD_DIM
BRANCH_W = 512
N_BRANCH = 3
DECAY_LORA = 64
AAA_LORA = 64
GATE_LORA = 128
RWKV_IN = 3 * RWKV_W + DECAY_LORA + AAA_LORA + GATE_LORA
ROT_DIM = HEAD_DIM // 4
ROPE_THETA = 500000.0
N_MEM = 256
XATTN_HEADS = 4
XATTN_HEAD_DIM = 128
XATTN_W = XATTN_HEADS * XATTN_HEAD_DIM
FFN_HIDDEN = -(-8 * D_MODEL // (3 * 256)) * 256
IN_SPLITS = (FOX_W, FOX_W, FOX_W, FOX_HEADS, DIFF_W, DIFF_W, DIFF_W, RWKV_IN, N_BRANCH * D_MODEL)
IN_WIDTH = sum(IN_SPLITS)
RWKV_SPLITS = (RWKV_W, RWKV_W, RWKV_W, DECAY_LORA, AAA_LORA, GATE_LORA)
FORGET_BIAS_MEAN = 2.0
RMS_EPS = 1e-6
RWKV_GN_EPS = 64e-5

kernel_name = 'hybrid_fox_diff_rwkv7_stream_step'


def _rmsnorm(x, g):
    x32 = x.astype(jnp.float32)
    y = x32 * lax.rsqrt(jnp.mean(x32 * x32, axis=-1, keepdims=True) + RMS_EPS)
    return (y * g.astype(jnp.float32)).astype(x.dtype)


def _split_cols(a, sizes):
    return jnp.split(a, np.cumsum(sizes)[:-1].tolist(), axis=-1)


def _rotary(x, pos):
    half = ROT_DIM // 2
    inv = ROPE_THETA ** (-jnp.arange(half, dtype=jnp.float32) / half)
    ang = pos.astype(jnp.float32)[:, None] * inv[None, :]
    shape = (1, pos.shape[0]) + (1,) * (x.ndim - 3) + (half,)
    cos, sin = jnp.cos(ang).reshape(shape), jnp.sin(ang).reshape(shape)
    x1 = x[..., :half].astype(jnp.float32)
    x2 = x[..., half:ROT_DIM].astype(jnp.float32)
    rot = jnp.concatenate([x1 * cos - x2 * sin, x2 * cos + x1 * sin], axis=-1).astype(x.dtype)
    return jnp.concatenate([rot, x[..., ROT_DIM:]], axis=-1)


def _over_query_blocks(fn, *qs):
    b, t = qs[0].shape[:2]
    if t <= Q_BLOCK:
        return fn(*qs)
    nb = t // Q_BLOCK
    blocks = tuple(jnp.moveaxis(a.reshape((b, nb, Q_BLOCK) + a.shape[2:]), 1, 0) for a in qs)
    out = lax.map(lambda args: fn(*args), blocks)
    return jnp.moveaxis(out, 0, 1).reshape((b, t) + out.shape[3:])


def _fox_attention(q, k_all, v_all, d_q, d_all, qpos, kpos):
    scale = HEAD_DIM ** -0.5
    d_k = jnp.swapaxes(d_all, 1, 2)

    def block(qb, dqb, pb):
        s = jnp.einsum('bqhd,bkhd->bhqk', qb, k_all).astype(jnp.float32) * scale
        s = s + jnp.swapaxes(dqb, 1, 2)[:, :, :, None] - d_k[:, :, None, :]
        mask = kpos[None, None, None, :] <= pb[:, None, :, None]
        p = jax.nn.softmax(jnp.where(mask, s, -jnp.inf), axis=-1)
        return jnp.einsum('bhqk,bkhd->bqhd', p.astype(v_all.dtype), v_all)

    return _over_query_blocks(block, q, d_q, qpos)


def _diff_attention(q, k_all, v_all, qpos, kpos, lam):
    scale = HEAD_DIM ** -0.5
    kchunk = kpos // CHUNK

    def block(qb, pb):
        s = jnp.einsum('bqhcd,bkhcd->bhcqk', qb, k_all).astype(jnp.float32) * scale
        mask = kchunk[None, None, None, None, :] <= (pb // CHUNK)[:, None, None, :, None]
        p = jax.nn.softmax(jnp.where(mask, s, -jnp.inf), axis=-1)
        pd = p[:, :, 0] - lam * p[:, :, 1]
        return jnp.einsum('bhqk,bkhv->bqhv', pd.astype(v_all.dtype), v_all)

    return _over_query_blocks(block, q, qpos)


def _rwkv7_scan(r, w, k, v, kk, a, s0):
    def step(S, xs):
        rt, wt, kt, vt, kkt, at = xs
        sa = jnp.einsum('bhvk,bhk->bhv', S, -kkt)
        S = S * wt[:, :, None, :] + sa[..., None] * (kkt * at)[:, :, None, :] + vt[..., None] * kt[:, :, None, :]
        return S, jnp.einsum('bhvk,bhk->bhv', S, rt)

    xs = tuple(jnp.swapaxes(z.astype(jnp.float32), 0, 1) for z in (r, w, k, v, kk, a))
    s_fin, y = lax.scan(step, s0.astype(jnp.float32), xs)
    return jnp.swapaxes(y, 0, 1), s_fin


def _mixer_sublayer(h, past, l, mw):
    (w_in, fox_forget_bias, diff_lambda, diff_subln, rwkv_mu, rwkv_w0, rwkv_w2, rwkv_a0, rwkv_a2,
     rwkv_g2, rwkv_kk_scale, rwkv_ka, rwkv_rk, rwkv_ln_gain, rwkv_ln_bias, w_branch, w_out) = mw
    fox_k0, fox_v0, fox_lf0, diff_k0, diff_v0, rwkv_s0, rwkv_prev0 = past
    f32 = jnp.float32
    b, t, _ = h.shape
    p_len = fox_k0.shape[1]
    qpos1 = p_len + jnp.arange(t, dtype=jnp.int32)
    kpos = jnp.arange(p_len + t, dtype=jnp.int32)
    qpos = jnp.broadcast_to(qpos1, (b, t))
    a_q, a_k, a_v, a_f, b_q, b_k, b_v, c_in, gate_in = _split_cols(h @ w_in, IN_SPLITS)

    fq = a_q.reshape(b, t, FOX_HEADS, HEAD_DIM)
    fk = a_k.reshape(b, t, FOX_HEADS, HEAD_DIM)
    fv = a_v.reshape(b, t, FOX_HEADS, HEAD_DIM)
    log_f = jax.nn.log_sigmoid((a_f + fox_forget_bias).astype(f32))
    d_all = jnp.cumsum(jnp.concatenate([fox_lf0.astype(f32), log_f], axis=1), axis=1)
    out_a = _fox_attention(fq, jnp.concatenate([fox_k0, fk], axis=1), jnp.concatenate([fox_v0, fv], axis=1),
                           d_all[:, p_len:], d_all, qpos, kpos).reshape(b, t, FOX_W)

    dq = _rotary(b_q.reshape(b, t, DIFF_HEADS, 2, HEAD_DIM), qpos1)
    dk = _rotary(b_k.reshape(b, t, DIFF_HEADS, 2, HEAD_DIM), qpos1)
    dv = b_v.reshape(b, t, DIFF_HEADS, 2 * HEAD_DIM)
    lam_init = 0.8 - 0.6 * math.exp(-0.3 * l)
    lq1, lk1, lq2, lk2 = diff_lambda.astype(f32)
    lam = jnp.exp(jnp.sum(lq1 * lk1)) - jnp.exp(jnp.sum(lq2 * lk2)) + lam_init
    o_b = _diff_attention(dq, jnp.concatenate([diff_k0, dk], axis=1), jnp.concatenate([diff_v0, dv], axis=1),
                          qpos, kpos, lam)
    out_b = (_rmsnorm(o_b, diff_subln) * (1.0 - lam_init)).reshape(b, t, DIFF_W)

    prev = jnp.concatenate([rwkv_prev0[:, None].astype(c_in.dtype), c_in[:, :-1]], axis=1)
    xs = c_in + rwkv_mu * (prev - c_in)
    cr, ck, cv, xw, xa, xg = _split_cols(xs, RWKV_SPLITS)
    heads = lambda z: z.reshape(b, t, RWKV_HEADS, HEAD_DIM)
    w_raw = -jax.nn.softplus(-(rwkv_w0 + jnp.tanh(xw) @ rwkv_w2).astype(f32)) - 0.5
    decay = jnp.exp(-jnp.exp(w_raw))
    a = jax.nn.sigmoid((rwkv_a0 + xa @ rwkv_a2).astype(f32))
    g = jax.nn.sigmoid(xg) @ rwkv_g2
    kk = heads(ck * rwkv_kk_scale).astype(f32)
    kk = kk / jnp.maximum(jnp.sqrt(jnp.sum(kk * kk, axis=-1, keepdims=True)), 1e-12)
    kmod = ck.astype(f32) * (1.0 + (a - 1.0) * rwkv_ka.astype(f32))
    y_c, s_new = _rwkv7_scan(heads(cr), heads(decay), heads(kmod), heads(cv), kk, heads(a), rwkv_s0)
    mu = jnp.mean(y_c, axis=-1, keepdims=True)
    var = jnp.mean(jnp.square(y_c - mu), axis=-1, keepdims=True)
    y_n = ((y_c - mu) * lax.rsqrt(var + RWKV_GN_EPS)).reshape(b, t, RWKV_W)
    y_n = y_n * rwkv_ln_gain.astype(f32) + rwkv_ln_bias.astype(f32)
    bonus = jnp.sum(heads(cr).astype(f32) * heads(kmod) * rwkv_rk.astype(f32), axis=-1, keepdims=True) * heads(cv).astype(f32)
    out_c = ((y_n + bonus.reshape(b, t, RWKV_W)) * g.astype(f32)).astype(h.dtype)

    branches = jnp.stack([out_a, out_b, out_c], axis=2)
    proj = jnp.einsum('btiw,iwd->btid', branches, w_branch)
    gates = jax.nn.sigmoid(gate_in.reshape(b, t, N_BRANCH, D_MODEL))
    y = jnp.sum(gates * proj, axis=2) @ w_out
    new_state = (fk, fv, log_f, dk, dv, s_new, c_in[:, -1])
    return y, new_state


def _memory_kv(mem, g, w_xkv):
    m = _rmsnorm(mem, g) @ w_xkv
    mk, mv = jnp.split(m, 2, axis=-1)
    b, n, _ = mem.shape
    return mk.reshape(b, n, XATTN_HEADS, XATTN_HEAD_DIM), mv.reshape(b, n, XATTN_HEADS, XATTN_HEAD_DIM)


def _cross_attention(h, mem_k, mem_v, w_xq, w_xo):
    b, t, _ = h.shape
    q = (h @ w_xq).reshape(b, t, XATTN_HEADS, XATTN_HEAD_DIM)
    s = jnp.einsum('bqhd,bkhd->bhqk', q, mem_k).astype(jnp.float32) * XATTN_HEAD_DIM ** -0.5
    p = jax.nn.softmax(s, axis=-1)
    o = jnp.einsum('bhqk,bkhd->bqhd', p.astype(mem_v.dtype), mem_v).reshape(b, t, XATTN_W)
    return o @ w_xo


def _swiglu(h, w_ffn_in, w_ffn_out):
    u, v = jnp.split(h @ w_ffn_in, 2, axis=-1)
    return (jax.nn.silu(u) * v) @ w_ffn_out


def _layer(x, past, mem_k, mem_v, l, mw, ow):
    (norm_mix_pre, norm_mix_post, norm_x_pre, norm_x_post, w_xq, w_xo,
     norm_ffn_pre, norm_ffn_post, w_ffn_in, w_ffn_out) = ow
    y, new_state = _mixer_sublayer(_rmsnorm(x, norm_mix_pre), past, l, mw)
    x = x + _rmsnorm(y, norm_mix_post)
    x = x + _rmsnorm(_cross_attention(_rmsnorm(x, norm_x_pre), mem_k, mem_v, w_xq, w_xo), norm_x_post)
    x = x + _rmsnorm(_swiglu(_rmsnorm(x, norm_ffn_pre), w_ffn_in, w_ffn_out), norm_ffn_post)
    return x, new_state


def setup_inputs(seed: int = 0) -> dict:
    key = jax.random.key(seed)
    ks = iter(jax.random.split(key, 64))
    f32 = jnp.float32
    L = DEPTH

    def nrm(shape, scale=1.0):
        return jax.random.normal(next(ks), shape, f32) * scale

    def gain(width):
        return 1.0 + nrm((L, width), 0.05)

    return {
        'x_prompt': nrm((BATCH, SEQ, D_MODEL)),
        'x_sample': nrm((DEC_BATCH, DEC_SEQ, D_MODEL)),
        'mem_prompt': nrm((BATCH, N_MEM, D_MODEL)),
        'cache_fox_k': nrm((L, DEC_BATCH, PAST_LEN, FOX_HEADS, HEAD_DIM)),
        'cache_fox_v': nrm((L, DEC_BATCH, PAST_LEN, FOX_HEADS, HEAD_DIM)),
        'cache_fox_logf': jax.nn.log_sigmoid(FORGET_BIAS_MEAN + nrm((L, DEC_BATCH, PAST_LEN, FOX_HEADS))),
        'cache_diff_k': nrm((L, DEC_BATCH, PAST_LEN, DIFF_HEADS, 2, HEAD_DIM)),
        'cache_diff_v': nrm((L, DEC_BATCH, PAST_LEN, DIFF_HEADS, 2 * HEAD_DIM)),
        'state_rwkv': nrm((L, DEC_BATCH, RWKV_HEADS, HEAD_DIM, HEAD_DIM)),
        'state_rwkv_shift': nrm((L, DEC_BATCH, RWKV_IN)),
        'cache_mem_k': nrm((L, DEC_BATCH, N_MEM, XATTN_HEADS, XATTN_HEAD_DIM)),
        'cache_mem_v': nrm((L, DEC_BATCH, N_MEM, XATTN_HEADS, XATTN_HEAD_DIM)),
        'norm_mix_pre': gain(D_MODEL),
        'norm_mix_post': gain(D_MODEL),
        'w_in': nrm((L, D_MODEL, IN_WIDTH), D_MODEL ** -0.5),
        'fox_forget_bias': FORGET_BIAS_MEAN + nrm((L, FOX_HEADS), 0.3),
        'diff_lambda': nrm((L, 4, HEAD_DIM), 0.1),
        'diff_subln': gain(2 * HEAD_DIM),
        'rwkv_mu': jax.random.uniform(next(ks), (L, RWKV_IN), f32),
        'rwkv_w0': nrm((L, RWKV_W), 0.5),
        'rwkv_w2': nrm((L, DECAY_LORA, RWKV_W), DECAY_LORA ** -0.5),
        'rwkv_a0': nrm((L, RWKV_W), 0.5),
        'rwkv_a2': nrm((L, AAA_LORA, RWKV_W), AAA_LORA ** -0.5),
        'rwkv_g2': nrm((L, GATE_LORA, RWKV_W), GATE_LORA ** -0.5),
        'rwkv_kk_scale': 0.85 + nrm((L, RWKV_W), 0.05),
        'rwkv_ka': 1.0 + nrm((L, RWKV_W), 0.05),
        'rwkv_rk': nrm((L, RWKV_HEADS, HEAD_DIM), 0.1),
        'rwkv_ln_gain': gain(RWKV_W),
        'rwkv_ln_bias': nrm((L, RWKV_W), 0.02),
        'w_branch': nrm((L, N_BRANCH, BRANCH_W, D_MODEL), BRANCH_W ** -0.5),
        'w_out': nrm((L, D_MODEL, D_MODEL), D_MODEL ** -0.5),
        'norm_x_pre': gain(D_MODEL),
        'norm_x_post': gain(D_MODEL),
        'norm_mem': gain(D_MODEL),
        'w_xq': nrm((L, D_MODEL, XATTN_W), D_MODEL ** -0.5),
        'w_xkv': nrm((L, D_MODEL, 2 * XATTN_W), D_MODEL ** -0.5),
        'w_xo': nrm((L, XATTN_W, D_MODEL), XATTN_W ** -0.5),
        'norm_ffn_pre': gain(D_MODEL),
        'norm_ffn_post': gain(D_MODEL),
        'w_ffn_in': nrm((L, D_MODEL, 2 * FFN_HIDDEN), D_MODEL ** -0.5),
        'w_ffn_out': nrm((L, FFN_HIDDEN, D_MODEL), FFN_HIDDEN ** -0.5),
    }


def reference(x_prompt, x_sample, mem_prompt, cache_fox_k, cache_fox_v, cache_fox_logf, cache_diff_k,
              cache_diff_v, state_rwkv, state_rwkv_shift, cache_mem_k, cache_mem_v, norm_mix_pre, norm_mix_post,
              w_in, fox_forget_bias, diff_lambda, diff_subln, rwkv_mu, rwkv_w0, rwkv_w2, rwkv_a0, rwkv_a2,
              rwkv_g2, rwkv_kk_scale, rwkv_ka, rwkv_rk, rwkv_ln_gain, rwkv_ln_bias, w_branch, w_out,
              norm_x_pre, norm_x_post, norm_mem, w_xq, w_xkv, w_xo, norm_ffn_pre, norm_ffn_post,
              w_ffn_in, w_ffn_out):
    mixer_w = (w_in, fox_forget_bias, diff_lambda, diff_subln, rwkv_mu, rwkv_w0, rwkv_w2, rwkv_a0, rwkv_a2,
               rwkv_g2, rwkv_kk_scale, rwkv_ka, rwkv_rk, rwkv_ln_gain, rwkv_ln_bias, w_branch, w_out)
    other_w = (norm_mix_pre, norm_mix_post, norm_x_pre, norm_x_post, w_xq, w_xo,
               norm_ffn_pre, norm_ffn_post, w_ffn_in, w_ffn_out)
    b, dt = x_prompt.shape[0], x_prompt.dtype
    past_prompt = (jnp.zeros((b, 0, FOX_HEADS, HEAD_DIM), dt), jnp.zeros((b, 0, FOX_HEADS, HEAD_DIM), dt),
                   jnp.zeros((b, 0, FOX_HEADS), jnp.float32), jnp.zeros((b, 0, DIFF_HEADS, 2, HEAD_DIM), dt),
                   jnp.zeros((b, 0, DIFF_HEADS, 2 * HEAD_DIM), dt),
                   jnp.zeros((b, RWKV_HEADS, HEAD_DIM, HEAD_DIM), jnp.float32), jnp.zeros((b, RWKV_IN), dt))
    xp, xs = x_prompt, x_sample
    p_new, s_new, p_mk, p_mv = [], [], [], []
    for l in range(DEPTH):
        mw = tuple(w[l] for w in mixer_w)
        ow = tuple(w[l] for w in other_w)
        mk, mv = _memory_kv(mem_prompt, norm_mem[l], w_xkv[l])
        xp, st_p = _layer(xp, past_prompt, mk, mv, l, mw, ow)
        past_s = (cache_fox_k[l], cache_fox_v[l], cache_fox_logf[l], cache_diff_k[l], cache_diff_v[l],
                  state_rwkv[l], state_rwkv_shift[l])
        xs, st_s = _layer(xs, past_s, cache_mem_k[l], cache_mem_v[l], l, mw, ow)
        p_new.append(st_p)
        s_new.append(st_s)
        p_mk.append(mk)
        p_mv.append(mv)
    p_fox_k, p_fox_v, p_fox_logf, p_diff_k, p_diff_v, p_rwkv_state, p_rwkv_shift = (jnp.stack(e) for e in zip(*p_new))
    s_fox_k, s_fox_v, s_fox_logf, s_diff_k, s_diff_v, s_rwkv_state, s_rwkv_shift = (jnp.stack(e) for e in zip(*s_new))
    p_mem_k = jnp.stack(p_mk)
    p_mem_v = jnp.stack(p_mv)
    return (xp, xs, p_fox_k, p_fox_v, p_fox_logf, p_diff_k, p_diff_v, p_rwkv_state, p_rwkv_shift, p_mem_k, p_mem_v,
            s_fox_k, s_fox_v, s_fox_logf, s_diff_k, s_diff_v, s_rwkv_state, s_rwkv_shift)
```

```python
import functools
import math

import numpy as np
import jax
import jax.numpy as jnp
from jax import lax
from jax.experimental import pallas as pl
from jax.experimental.pallas import tpu as pltpu

F32 = jnp.float32
BF = jnp.bfloat16

CHUNK = 64
HEAD_DIM = 64
FOX_HEADS = 8
DIFF_HEADS = 4
RWKV_HEADS = 8
GROUPS = 8
BRANCH_W = 512
DECAY_LORA = 64
AAA_LORA = 64
GATE_LORA = 128
RWKV_IN = 3 * BRANCH_W + DECAY_LORA + AAA_LORA + GATE_LORA
ROT_DIM = HEAD_DIM // 4
ROPE_THETA = 500000.0
XATTN_HEADS = 4
XATTN_HEAD_DIM = 128
RMS_EPS = 1e-6
RWKV_GN_EPS = 64e-5
ATTN_SCALE = HEAD_DIM ** -0.5
NEG_BIG = -1e30

LANES = 128
SUBLANES = 8
VMEM_LIMIT_BYTES = 56 * 1024 * 1024


def _cparams(*sem):
    return pltpu.CompilerParams(dimension_semantics=sem, vmem_limit_bytes=VMEM_LIMIT_BYTES)


def _const_spec(shape):
    nd = len(shape)
    return pl.BlockSpec(shape, lambda *_: (0,) * nd)


def _rms(x, g):
    return x * lax.rsqrt(jnp.mean(x * x, axis=-1, keepdims=True) + RMS_EPS) * g


_NN = ((1,), (0,))
_NT = ((1,), (1,))
_TN = ((0,), (0,))


def _dg(a, b, dims):
    return lax.dot_general(a, b, (dims, ((), ())), preferred_element_type=F32)


def _dot(a, b):
    return _dg(a.astype(BF), b.astype(BF), _NN)


def _dot_nt(a, b):
    return _dg(a.astype(BF), b.astype(BF), _NT)


def _split2(x):
    hi = x.astype(BF)
    lo = (x - hi.astype(F32)).astype(BF)
    return hi, lo


def _dot3(a, b, dims):
    ah, al = _split2(a)
    bh, bl = _split2(b)
    return _dg(ah, bh, dims) + (_dg(ah, bl, dims) + _dg(al, bh, dims))


def _split3(x):
    hi = x.astype(BF)
    r1 = x - hi.astype(F32)
    mid = r1.astype(BF)
    lo = (r1 - mid.astype(F32)).astype(BF)
    return hi, mid, lo


def _dot_sel(sel, x):
    hi, mid, lo = _split3(x)
    return _dg(sel, hi, _NN) + (_dg(sel, mid, _NN) + _dg(sel, lo, _NN))


def _dot_sel_right(x, sel):
    hi, mid, lo = _split3(x)
    return _dg(hi, sel, _NN) + (_dg(mid, sel, _NN) + _dg(lo, sel, _NN))


def _softplus(x):
    return jnp.maximum(x, 0.0) + jnp.log(1.0 + jnp.exp(-jnp.abs(x)))


def _fox_inproj_kernel(x_ref, g_ref, wq_ref, wkv_t_ref, wf_t_ref, q_ref, kt_ref, vt_ref, f_ref):
    xn = _rms(x_ref[...], g_ref[...]).astype(BF)
    w = BRANCH_W
    q_ref[...] = (_dg(xn, wq_ref[...], _NN) * ATTN_SCALE).astype(BF)
    kt_ref[0] = _dg(wkv_t_ref[0:w, :], xn, _NT)
    vt_ref[0] = _dg(wkv_t_ref[w:2 * w, :], xn, _NT)
    f_ref[0] = _dg(wf_t_ref[...], xn, _NT)


def _row_specs(b, t, tm):
    nt = t // tm
    row = lambda width: pl.BlockSpec((tm, width), lambda bi, ti: (bi * nt + ti, 0))
    col = lambda height: pl.BlockSpec((1, height, tm), lambda bi, ti: (bi, 0, ti))
    return nt, row, col


def _fox_inproj(x, g, wq, wkv_t, wf_t, b, tm):
    n, d = x.shape
    t = n // b
    w = BRANCH_W
    nt, row, col = _row_specs(b, t, tm)
    return pl.pallas_call(
        _fox_inproj_kernel,
        grid=(b, nt),
        in_specs=[row(d), _const_spec((1, d)), _const_spec((d, w)), _const_spec((2 * w, d)),
                  _const_spec((FOX_HEADS, d))],
        out_specs=[row(w), col(w), col(w), col(FOX_HEADS)],
        out_shape=[jax.ShapeDtypeStruct((n, w), BF), jax.ShapeDtypeStruct((b, w, t), F32),
                   jax.ShapeDtypeStruct((b, w, t), F32), jax.ShapeDtypeStruct((b, FOX_HEADS, t), F32)],
        compiler_params=_cparams("parallel", "parallel"),
    )(x, g, wq, wkv_t, wf_t)


def _diff_inproj_kernel(x_ref, g_ref, wq_ref, wk_t_ref, wv_ref, cos_ref, sa_ref, sb_ref, cos_t_ref, sin_t_ref,
                        q_ref, kt_ref, v_ref):
    xn = _rms(x_ref[...], g_ref[...]).astype(BF)
    w = BRANCH_W
    half = ROT_DIM // 2
    cos, sa, sb = cos_ref[...], sa_ref[...], sb_ref[...]
    yq = _dg(xn, wq_ref[...], _NN)
    for j in range(w // LANES):
        yj = yq[:, j * LANES:(j + 1) * LANES]
        rj = yj * cos + pltpu.roll(yj, LANES - half, 1) * sa + pltpu.roll(yj, half, 1) * sb
        q_ref[:, j * LANES:(j + 1) * LANES] = (rj * ATTN_SCALE).astype(BF)
    yk = _dg(wk_t_ref[...], xn, _NT)
    cos_t, sin_t = cos_t_ref[...], sin_t_ref[...]
    for gi in range(GROUPS):
        r0 = gi * HEAD_DIM
        y1, y2 = yk[r0:r0 + half, :], yk[r0 + half:r0 + ROT_DIM, :]
        kt_ref[0, r0:r0 + half, :] = y1 * cos_t - y2 * sin_t
        kt_ref[0, r0 + half:r0 + ROT_DIM, :] = y2 * cos_t + y1 * sin_t
        kt_ref[0, r0 + ROT_DIM:r0 + HEAD_DIM, :] = yk[r0 + ROT_DIM:r0 + HEAD_DIM, :]
    v_ref[...] = _dg(xn, wv_ref[...], _NN)


def _diff_inproj(x, g, wq, wk_t, wv, tabs, b, tm):
    n, d = x.shape
    t = n // b
    w = BRANCH_W
    nt, row, col = _row_specs(b, t, tm)
    row_tab = pl.BlockSpec((tm, LANES), lambda bi, ti: (ti, 0))
    col_tab = pl.BlockSpec((ROT_DIM // 2, tm), lambda bi, ti: (0, ti))
    return pl.pallas_call(
        _diff_inproj_kernel,
        grid=(b, nt),
        in_specs=[row(d), _const_spec((1, d)), _const_spec((d, w)), _const_spec((w, d)), _const_spec((d, w)),
                  row_tab, row_tab, row_tab, col_tab, col_tab],
        out_specs=[row(w), col(w), row(w)],
        out_shape=[jax.ShapeDtypeStruct((n, w), BF), jax.ShapeDtypeStruct((b, w, t), F32),
                   jax.ShapeDtypeStruct((n, w), F32)],
        compiler_params=_cparams("parallel", "parallel"),
    )(x, g, wq, wk_t, wv, *tabs)


def _rotary_tables(pos):
    half = ROT_DIM // 2
    inv = ROPE_THETA ** (-np.arange(half, dtype=np.float32) / half)
    ang = pos.astype(F32)[:, None] * jnp.asarray(inv, F32)[None, :]
    cos, sin = jnp.cos(ang), jnp.sin(ang)
    t = pos.shape[0]
    one = jnp.ones((t, HEAD_DIM - ROT_DIM), F32)
    zero = jnp.zeros((t, HEAD_DIM - ROT_DIM), F32)
    zh = jnp.zeros((t, half), F32)
    cos_h = jnp.concatenate([cos, cos, one], axis=1)
    sa_h = jnp.concatenate([-sin, zh, zero], axis=1)
    sb_h = jnp.concatenate([zh, sin, zero], axis=1)
    dup = lambda a: jnp.concatenate([a, a], axis=1)
    return dup(cos_h), dup(sa_h), dup(sb_h), cos.T, sin.T


def _rwkv_prep_kernel(x_ref, g_ref, wc_ref, shift0_ref, mu_ref, w0_ref, w2_ref, a0_ref, a2_ref, g2_ref,
                      kks_ref, ka_ref, rk_ref, seg_ref,
                      r_ref, lw_ref, km_ref, v_ref, kk_ref, kka_ref, gg_ref, bonus_ref, shift_ref,
                      carry_ref):
    t = pl.program_id(1)
    tm = x_ref.shape[0]
    w = BRANCH_W
    xn = _rms(x_ref[...], g_ref[...]).astype(BF)
    c = _dg(xn, wc_ref[...], _NN)

    @pl.when(t == 0)
    def _():
        carry_ref[...] = shift0_ref[0]

    row = lax.broadcasted_iota(jnp.int32, (tm, 1), 0)
    prev = jnp.where(row == 0, carry_ref[...], pltpu.roll(c, 1, 0))
    last = c[tm - 1:tm, :]
    carry_ref[...] = last
    shift_ref[0] = last

    xs = c + mu_ref[...] * (prev - c)
    cr, ck, cv = xs[:, 0:w], xs[:, w:2 * w], xs[:, 2 * w:3 * w]
    o = 3 * w
    xw = xs[:, o:o + DECAY_LORA]
    xa = xs[:, o + DECAY_LORA:o + DECAY_LORA + AAA_LORA]
    xg = xs[:, o + DECAY_LORA + AAA_LORA:]
    z = w0_ref[...] + _dot(jnp.tanh(xw), w2_ref[...])
    w_raw = -_softplus(-z) - 0.5
    a = jax.nn.sigmoid(a0_ref[...] + _dot(xa, a2_ref[...]))
    seg = seg_ref[...]
    kk = ck * kks_ref[...]
    kk = kk / jnp.maximum(jnp.sqrt(_dot_sel_right(kk * kk, seg)), 1e-12)
    kmod = ck * (1.0 + (a - 1.0) * ka_ref[...])
    r_ref[...] = cr
    lw_ref[...] = -jnp.exp(w_raw)
    km_ref[...] = kmod
    v_ref[...] = cv
    kk_ref[...] = kk
    kka_ref[...] = kk * a
    gg_ref[...] = _dot(jax.nn.sigmoid(xg), g2_ref[...])
    bonus_ref[...] = _dot_sel_right(cr * kmod * rk_ref[...], seg) * cv


def _rwkv_prep(x, g, wc, shift0, vecs, mats, seg, b, tm):
    n, d = x.shape
    t = n // b
    w = BRANCH_W
    mu, w0, a0, kks, ka, rk = vecs
    w2, a2, g2 = mats
    nt, row, _ = _row_specs(b, t, tm)
    per_b = pl.BlockSpec((1, 1, RWKV_IN), lambda bi, ti: (bi, 0, 0))
    outs = [jax.ShapeDtypeStruct((n, w), F32)] * 8 + [jax.ShapeDtypeStruct((b, 1, RWKV_IN), F32)]
    return pl.pallas_call(
        _rwkv_prep_kernel,
        grid=(b, nt),
        in_specs=[row(d), _const_spec((1, d)), _const_spec((d, RWKV_IN)), per_b,
                  _const_spec((1, RWKV_IN)), _const_spec((1, w)), _const_spec((DECAY_LORA, w)),
                  _const_spec((1, w)), _const_spec((AAA_LORA, w)), _const_spec((GATE_LORA, w)),
                  _const_spec((1, w)), _const_spec((1, w)), _const_spec((1, w)), _const_spec((w, w))],
        out_specs=[row(w)] * 8 + [per_b],
        out_shape=outs,
        scratch_shapes=[pltpu.VMEM((1, RWKV_IN), F32)],
        compiler_params=_cparams("parallel", "arbitrary"),
    )(x, g, wc, shift0, mu, w0, w2, a0, a2, g2, kks, ka, rk, seg)


def _logf_cumsum_kernel(z_ref, bias_ref, lf_ref, d_ref, *, past, new):
    z = z_ref[0]
    width = z.shape[1]
    col = lax.broadcasted_iota(jnp.int32, z.shape, 1)
    zz = z + bias_ref[...]
    log_sig = jnp.minimum(zz, 0.0) - jnp.log(1.0 + jnp.exp(-jnp.abs(zz)))
    lf = jnp.where(col < past, z, jnp.where(col < past + new, log_sig, 0.0))
    lf_ref[0] = lf
    x = lf
    s = 1
    while s < width:
        x = x + jnp.where(col >= s, pltpu.roll(x, s, 1), 0.0)
        s *= 2
    d_ref[0] = x


def _logf_cumsum(z, bias, past, new):
    b, h, width = z.shape
    blk = pl.BlockSpec((1, h, width), lambda i: (i, 0, 0))
    return pl.pallas_call(
        functools.partial(_logf_cumsum_kernel, past=past, new=new),
        grid=(b,),
        in_specs=[blk, _const_spec((h, 1))],
        out_specs=[blk, blk],
        out_shape=[jax.ShapeDtypeStruct(z.shape, F32)] * 2,
        compiler_params=_cparams("parallel"),
    )(z, bias)


def _softmax_update(s, m_ref, l_ref, idx):
    m_prev = m_ref[idx]
    m_new = jnp.maximum(m_prev, jnp.max(s, axis=-1, keepdims=True))
    alpha = jnp.exp(m_prev - m_new)
    p = jnp.exp(s - m_new)
    l_ref[idx] = alpha * l_ref[idx] + jnp.sum(p, axis=-1, keepdims=True)
    m_ref[idx] = m_new
    return p.astype(BF), alpha


def _init_softmax_state(m_ref, l_ref, acc_ref):
    m_ref[...] = jnp.full(m_ref.shape, NEG_BIG, F32)
    l_ref[...] = jnp.zeros(l_ref.shape, F32)
    acc_ref[...] = jnp.zeros(acc_ref.shape, F32)


def _fox_prompt_kernel(q_ref, kt_ref, vt_ref, dq_ref, dk_ref, o_ref, m_ref, l_ref, acc_ref):
    i, j = pl.program_id(1), pl.program_id(2)
    tq, tk = q_ref.shape[0], kt_ref.shape[2]

    @pl.when(j == 0)
    def _():
        _init_softmax_state(m_ref, l_ref, acc_ref)

    @pl.when(j <= i)
    def _():
        rows = i * tq + lax.broadcasted_iota(jnp.int32, (tq, tk), 0)
        cols = j * tk + lax.broadcasted_iota(jnp.int32, (tq, tk), 1)
        visible = cols <= rows
        dq = dq_ref[...]
        dk = dk_ref[0]
        for h in range(FOX_HEADS):
            hs = slice(h * HEAD_DIM, (h + 1) * HEAD_DIM)
            s = _dg(q_ref[:, hs], kt_ref[0, hs, :].astype(BF), _NN)
            s = s + dq[:, h:h + 1] - dk[h:h + 1, :]
            s = jnp.where(visible, s, NEG_BIG)
            p, alpha = _softmax_update(s, m_ref, l_ref, h)
            acc_ref[h] = alpha * acc_ref[h] + _dg(p, vt_ref[0, hs, :].astype(BF), _NT)

    @pl.when(j == pl.num_programs(2) - 1)
    def _():
        for h in range(FOX_HEADS):
            o_ref[:, h * HEAD_DIM:(h + 1) * HEAD_DIM] = acc_ref[h] / l_ref[h]


def _fox_prompt(q, kt, vt, d_col, d_row, b, tq):
    n, w = q.shape
    t = n // b
    nq = t // tq
    qspec = pl.BlockSpec((tq, w), lambda bi, i, j: (bi * nq + i, 0))
    kspec = pl.BlockSpec((1, w, tq), lambda bi, i, j: (bi, 0, jnp.minimum(j, i)))
    return pl.pallas_call(
        _fox_prompt_kernel,
        grid=(b, nq, nq),
        in_specs=[qspec, kspec, kspec,
                  pl.BlockSpec((tq, FOX_HEADS), lambda bi, i, j: (bi * nq + i, 0)),
                  pl.BlockSpec((1, FOX_HEADS, tq), lambda bi, i, j: (bi, 0, jnp.minimum(j, i)))],
        out_specs=qspec,
        out_shape=jax.ShapeDtypeStruct((n, w), F32),
        scratch_shapes=[pltpu.VMEM((FOX_HEADS, tq, 1), F32), pltpu.VMEM((FOX_HEADS, tq, 1), F32),
                        pltpu.VMEM((FOX_HEADS, tq, HEAD_DIM), F32)],
        compiler_params=_cparams("parallel", "parallel", "arbitrary"),
    )(q, kt, vt, d_col, d_row)


def _diff_lambda(lam_ref, lam_init):
    p = lam_ref[...]
    s1 = jnp.sum(p[0:1, :] * p[1:2, :], axis=-1, keepdims=True)
    s2 = jnp.sum(p[2:3, :] * p[3:4, :], axis=-1, keepdims=True)
    return jnp.exp(s1) - jnp.exp(s2) + lam_init


def _diff_combine(acc0, l0, acc1, l1, lam, subln, lam_init):
    o = acc0 / l0 - lam * (acc1 / l1)
    return _rms(o, subln) * (1.0 - lam_init)


def _diff_prompt_kernel(q_ref, kt_ref, v_ref, lam_ref, subln_ref, o_ref, m_ref, l_ref, acc_ref, *, lam_init):
    i, j = pl.program_id(1), pl.program_id(2)
    tq, tk = q_ref.shape[0], kt_ref.shape[2]
    dv = 2 * HEAD_DIM

    @pl.when(j == 0)
    def _():
        _init_softmax_state(m_ref, l_ref, acc_ref)

    @pl.when(j <= i)
    def _():
        rows = i * tq + lax.broadcasted_iota(jnp.int32, (tq, tk), 0)
        cols = j * tk + lax.broadcasted_iota(jnp.int32, (tq, tk), 1)
        visible = (cols // CHUNK) <= (rows // CHUNK)
        for h in range(DIFF_HEADS):
            vh = v_ref[:, h * dv:(h + 1) * dv].astype(BF)
            for c in range(2):
                gi = 2 * h + c
                gs = slice(gi * HEAD_DIM, (gi + 1) * HEAD_DIM)
                s = jnp.where(visible, _dg(q_ref[:, gs], kt_ref[0, gs, :].astype(BF), _NN), NEG_BIG)
                p, alpha = _softmax_update(s, m_ref, l_ref, gi)
                acc_ref[gi] = alpha * acc_ref[gi] + _dg(p, vh, _NN)

    @pl.when(j == pl.num_programs(2) - 1)
    def _():
        lam = _diff_lambda(lam_ref, lam_init)
        for h in range(DIFF_HEADS):
            o_ref[:, h * dv:(h + 1) * dv] = _diff_combine(
                acc_ref[2 * h], l_ref[2 * h], acc_ref[2 * h + 1], l_ref[2 * h + 1], lam, subln_ref[...], lam_init)


def _diff_prompt(q, kt, v, lam_p, subln, lam_init, b, tq):
    n, w = q.shape
    t = n // b
    nq = t // tq
    qspec = pl.BlockSpec((tq, w), lambda bi, i, j: (bi * nq + i, 0))
    ktspec = pl.BlockSpec((1, w, tq), lambda bi, i, j: (bi, 0, jnp.minimum(j, i)))
    vspec = pl.BlockSpec((tq, w), lambda bi, i, j: (bi * nq + jnp.minimum(j, i), 0))
    return pl.pallas_call(
        functools.partial(_diff_prompt_kernel, lam_init=lam_init),
        grid=(b, nq, nq),
        in_specs=[qspec, ktspec, vspec, _const_spec((4, HEAD_DIM)), _const_spec((1, 2 * HEAD_DIM))],
        out_specs=qspec,
        out_shape=jax.ShapeDtypeStruct((n, w), F32),
        scratch_shapes=[pltpu.VMEM((GROUPS, tq, 1), F32), pltpu.VMEM((GROUPS, tq, 1), F32),
                        pltpu.VMEM((GROUPS, tq, 2 * HEAD_DIM), F32)],
        compiler_params=_cparams("parallel", "parallel", "arbitrary"),
    )(q, kt, v, lam_p, subln)


def _sample_attn_kernel(*refs, fox, t_new, lam_init):
    if fox:
        q_ref, kc_ref, vc_ref, kn_ref, vn_ref, dq_ref, dkc_ref, dkn_ref, o_ref, m_ref, l_ref, acc_ref = refs
    else:
        q_ref, kc_ref, vc_ref, kn_ref, vn_ref, lam_ref, subln_ref, o_ref, m_ref, l_ref, acc_ref = refs
    j = pl.program_id(1)
    dv = 2 * HEAD_DIM
    pair = 2 * t_new

    @pl.when(j == 0)
    def _():
        _init_softmax_state(m_ref, l_ref, acc_ref)

    def expand(dk):
        return jnp.concatenate([jnp.broadcast_to(dk[gi:gi + 1, :], (t_new, dk.shape[1])) for gi in range(GROUPS)],
                               axis=0)

    def scores(kt, dk, causal):
        s = _dg(q_ref[0], kt.astype(BF), _NN)
        if fox:
            s = s + dq_ref[0] - expand(dk)
        if causal:
            qi = lax.broadcasted_iota(jnp.int32, s.shape, 0) % t_new
            kj = lax.broadcasted_iota(jnp.int32, s.shape, 1)
            s = jnp.where(kj <= qi, s, NEG_BIG)
        return _softmax_update(s, m_ref, l_ref, 0)

    def accumulate_diff(p, alpha, value_of_head):
        for h in range(DIFF_HEADS):
            rs = slice(h * pair, (h + 1) * pair)
            acc_ref[0, rs, :] = alpha[rs] * acc_ref[0, rs, :] + _dg(p[rs], value_of_head(h).astype(BF), _NN)

    tk = kc_ref.shape[3]
    if fox:
        p, alpha = scores(kc_ref[0, 0], dkc_ref[0], False)
        acc_ref[0] = alpha * acc_ref[0] + _dg(p, vc_ref[0, 0].astype(BF), _NT)
    else:
        p, alpha = scores(kc_ref[0, 0], None, False)
        accumulate_diff(p, alpha, lambda h: vc_ref[0, 0, pl.ds(h, tk, stride=DIFF_HEADS), :])

    @pl.when(j == pl.num_programs(1) - 1)
    def _():
        if fox:
            p, alpha = scores(kn_ref[0], dkn_ref[0][:, 0:t_new], True)
            acc = alpha * acc_ref[0] + _dg(p, vn_ref[0].astype(BF), _NT)
            l = l_ref[0]
            for gi in range(GROUPS):
                rs = slice(gi * t_new, (gi + 1) * t_new)
                cs = slice(gi * HEAD_DIM, (gi + 1) * HEAD_DIM)
                o_ref[:, cs] = acc[rs, cs] / l[rs]
        else:
            p, alpha = scores(kn_ref[0], None, False)
            accumulate_diff(p, alpha, lambda h: vn_ref[:, h * dv:(h + 1) * dv])
            acc = acc_ref[0]
            l = l_ref[0]
            lam = _diff_lambda(lam_ref, lam_init)
            for h in range(DIFF_HEADS):
                r0 = slice(h * pair, h * pair + t_new)
                r1 = slice(h * pair + t_new, (h + 1) * pair)
                o_ref[:, h * dv:(h + 1) * dv] = _diff_combine(acc[r0], l[r0], acc[r1], l[r1], lam, subln_ref[...],
                                                              lam_init)


def _sample_attn(qbd, kc, vc, kn, vn, extra, *, layer, fox, b, t_new, past, tk, lam_init=0.0):
    w = BRANCH_W
    rows = GROUPS * t_new
    nk = past // tk
    qspec = pl.BlockSpec((1, rows, w), lambda bi, j: (bi, 0, 0))
    kcspec = pl.BlockSpec((1, 1, w, tk), lambda bi, j: (layer, bi, 0, j))
    knspec = pl.BlockSpec((1, w, t_new), lambda bi, j: (bi, 0, 0))
    ospec = pl.BlockSpec((t_new, w), lambda bi, j: (bi, 0))
    if fox:
        dq, d_row = extra
        vcspec, vnspec, acc_w = kcspec, knspec, w
        especs = [pl.BlockSpec((1, rows, 1), lambda bi, j: (bi, 0, 0)),
                  pl.BlockSpec((1, GROUPS, tk), lambda bi, j: (bi, 0, j)),
                  pl.BlockSpec((1, GROUPS, LANES), lambda bi, j: (bi, 0, past // LANES))]
        eargs = [dq, d_row, d_row]
    else:
        vcspec = pl.BlockSpec((1, 1, DIFF_HEADS * tk, 2 * HEAD_DIM), lambda bi, j: (layer, bi, j, 0))
        vnspec, acc_w = ospec, 2 * HEAD_DIM
        especs = [_const_spec((4, HEAD_DIM)), _const_spec((1, 2 * HEAD_DIM))]
        eargs = list(extra)
    return pl.pallas_call(
        functools.partial(_sample_attn_kernel, fox=fox, t_new=t_new, lam_init=lam_init),
        grid=(b, nk),
        in_specs=[qspec, kcspec, vcspec, knspec, vnspec] + especs,
        out_specs=ospec,
        out_shape=jax.ShapeDtypeStruct((b * t_new, w), F32),
        scratch_shapes=[pltpu.VMEM((1, rows, 1), F32), pltpu.VMEM((1, rows, 1), F32),
                        pltpu.VMEM((1, rows, acc_w), F32)],
        compiler_params=_cparams("parallel", "arbitrary"),
    )(qbd, kc, vc, kn, vn, *eargs)


def _block_diag_queries(q, b, t_new):
    q4 = q.reshape(b, t_new, GROUPS, HEAD_DIM)
    eye = jnp.eye(GROUPS, dtype=q.dtype)
    qbd = q4.transpose(0, 2, 1, 3)[:, :, :, None, :] * eye[None, :, None, :, None]
    return qbd.reshape(b, GROUPS * t_new, GROUPS * HEAD_DIM)


def _rwkv_scan_kernel(r_ref, lw_ref, k_ref, v_ref, kk_ref, kka_ref, gg_ref, bonus_ref, gain_ref, bias_ref,
                      s0_ref, o_ref, sfin_ref, s_ref):
    c = pl.program_id(1)
    ch = r_ref.shape[0]
    n = HEAD_DIM

    @pl.when(c == 0)
    def _():
        s_ref[...] = s0_ref[0]

    ri = lax.broadcasted_iota(jnp.int32, (ch, ch), 0)
    ci = lax.broadcasted_iota(jnp.int32, (ch, ch), 1)
    tri = (ci <= ri).astype(BF)
    ri2 = lax.broadcasted_iota(jnp.int32, (ch, 2 * ch), 0)
    ci2 = lax.broadcasted_iota(jnp.int32, (ch, 2 * ch), 1)
    ci2m = jnp.where(ci2 >= ch, ci2 - ch, ci2)
    strict_right = (ci2 >= ch) & (ci2m < ri2)
    strict_left = ci < ri
    incl_both = ci2m <= ri2
    steps = max(1, int(math.ceil(math.log2(ch))))

    for h in range(RWKV_HEADS):
        hs = slice(h * n, (h + 1) * n)
        lw = lw_ref[:, hs]
        cum = _dot_sel(tri, lw)
        cum_prev = cum - lw
        cum_end = cum[ch - 1:ch, :]
        kk, kka, kmod, vh = kk_ref[:, hs], kka_ref[:, hs], k_ref[:, hs], v_ref[:, hs]
        e_neg = jnp.exp(-cum)
        e_tail = jnp.exp(cum_end - cum)
        ar = jnp.concatenate([-kk * jnp.exp(cum_prev), r_ref[:, hs] * jnp.exp(cum)], axis=0)
        bk = jnp.concatenate([kka * e_neg, kmod * e_neg], axis=0)
        bk_tail = jnp.concatenate([kka * e_tail, kmod * e_tail], axis=0)
        s_old = s_ref[h]
        gmat = _dot3(ar, bk, _NT)
        ars = _dot3(ar, s_old, _NT)
        g_top = gmat[0:ch, :]
        vv = jnp.concatenate([vh, vh], axis=0)
        x = ars[0:ch, :] + _dot3(jnp.where(strict_right, g_top, 0.0), vv, _NN)
        nil = jnp.where(strict_left, g_top[:, 0:ch], 0.0)
        for it in range(steps):
            x = x + _dot3(nil, x, _NN)
            if it + 1 < steps:
                nil = _dot3(nil, nil, _NN)
        xv = jnp.concatenate([x, vh], axis=0)
        y = ars[ch:, :] + _dot3(jnp.where(incl_both, gmat[ch:, :], 0.0), xv, _NN)
        s_ref[h] = s_old * jnp.exp(cum_end) + _dot3(xv, bk_tail, _TN)

        mu = jnp.mean(y, axis=-1, keepdims=True)
        yc = y - mu
        var = jnp.mean(yc * yc, axis=-1, keepdims=True)
        yn = yc * lax.rsqrt(var + RWKV_GN_EPS) * gain_ref[:, hs] + bias_ref[:, hs]
        o_ref[:, hs] = (yn + bonus_ref[:, hs]) * gg_ref[:, hs]

    @pl.when(c == pl.num_programs(1) - 1)
    def _():
        sfin_ref[0] = s_ref[...]


def _rwkv_scan(r, lw, km, v, kk, kka, gg, bonus, gain, bias, s0, b, ch):
    n, w = r.shape
    t = n // b
    nc = t // ch
    row = pl.BlockSpec((ch, w), lambda bi, ci: (bi * nc + ci, 0))
    st = pl.BlockSpec((1, RWKV_HEADS, HEAD_DIM, HEAD_DIM), lambda bi, ci: (bi, 0, 0, 0))
    return pl.pallas_call(
        _rwkv_scan_kernel,
        grid=(b, nc),
        in_specs=[row] * 8 + [_const_spec((1, w)), _const_spec((1, w)), st],
        out_specs=[row, st],
        out_shape=[jax.ShapeDtypeStruct((n, w), F32), jax.ShapeDtypeStruct(s0.shape, F32)],
        scratch_shapes=[pltpu.VMEM((RWKV_HEADS, HEAD_DIM, HEAD_DIM), F32)],
        compiler_params=_cparams("parallel", "arbitrary"),
    )(r, lw, km, v, kk, kka, gg, bonus, gain, bias, s0)


def _merge_kernel(x_ref, oa_ref, ob_ref, oc_ref, gpre_ref, wg_ref, wb_ref, wo_ref, gpost_ref, y_ref):
    x = x_ref[...]
    d = x.shape[1]
    xn = _rms(x, gpre_ref[...]).astype(BF)
    acc = jnp.zeros(x.shape, F32)
    for i, o_ref in enumerate((oa_ref, ob_ref, oc_ref)):
        gate = jax.nn.sigmoid(_dg(xn, wg_ref[:, i * d:(i + 1) * d], _NN))
        acc = acc + gate * _dot(o_ref[...], wb_ref[i])
    y = _dot(acc, wo_ref[...])
    y_ref[...] = x + _rms(y, gpost_ref[...])


def _merge(x, oa, ob, oc, gpre, wg, wb, wo, gpost, tm):
    n, d = x.shape
    w = BRANCH_W
    row = lambda width: pl.BlockSpec((tm, width), lambda i: (i, 0))
    return pl.pallas_call(
        _merge_kernel,
        grid=(n // tm,),
        in_specs=[row(d), row(w), row(w), row(w), _const_spec((1, d)), _const_spec((d, 3 * d)),
                  _const_spec((3, w, d)), _const_spec((d, d)), _const_spec((1, d))],
        out_specs=row(d),
        out_shape=jax.ShapeDtypeStruct((n, d), F32),
        compiler_params=_cparams("parallel"),
    )(x, oa, ob, oc, gpre, wg, wb, wo, gpost)


def _memkv_kernel(m_ref, g_ref, w_ref, k_ref, v_ref):
    xn = _rms(m_ref[...], g_ref[...]).astype(BF)
    w = k_ref.shape[1]
    k_ref[...] = _dg(xn, w_ref[:, 0:w], _NN)
    v_ref[...] = _dg(xn, w_ref[:, w:2 * w], _NN)


def _memkv(mem, g, w_xkv, tm):
    n, d = mem.shape
    w = w_xkv.shape[1] // 2
    row = lambda width: pl.BlockSpec((tm, width), lambda i: (i, 0))
    return pl.pallas_call(
        _memkv_kernel,
        grid=(n // tm,),
        in_specs=[row(d), _const_spec((1, d)), _const_spec((d, 2 * w))],
        out_specs=[row(w), row(w)],
        out_shape=[jax.ShapeDtypeStruct((n, w), F32)] * 2,
        compiler_params=_cparams("parallel"),
    )(mem, g, w_xkv)


def _xattn_kernel(x_ref, gpre_ref, wq_ref, mk_ref, mv_ref, wo_ref, gpost_ref, y_ref):
    x = x_ref[...]
    xn = _rms(x, gpre_ref[...])
    q = _dot(xn, wq_ref[...])
    hd = XATTN_HEAD_DIM
    outs = []
    for h in range(XATTN_HEADS):
        hs = slice(h * hd, (h + 1) * hd)
        s = _dot_nt(q[:, hs], mk_ref[0, 0, :, hs]) * (hd ** -0.5)
        p = jnp.exp(s - jnp.max(s, axis=-1, keepdims=True))
        p = p / jnp.sum(p, axis=-1, keepdims=True)
        outs.append(_dot(p, mv_ref[0, 0, :, hs]))
    o = jnp.concatenate(outs, axis=-1)
    y_ref[...] = x + _rms(_dot(o, wo_ref[...]), gpost_ref[...])


def _xattn(x, gpre, wq, mk, mv, wo, gpost, layer, b, tm):
    n, d = x.shape
    t = n // b
    nt = t // tm
    n_mem, w = mk.shape[2], mk.shape[3]
    row = pl.BlockSpec((tm, d), lambda bi, ti: (bi * nt + ti, 0))
    mem = pl.BlockSpec((1, 1, n_mem, w), lambda bi, ti: (layer, bi, 0, 0))
    return pl.pallas_call(
        _xattn_kernel,
        grid=(b, nt),
        in_specs=[row, _const_spec((1, d)), _const_spec((d, w)), mem, mem, _const_spec((w, d)),
                  _const_spec((1, d))],
        out_specs=row,
        out_shape=jax.ShapeDtypeStruct((n, d), F32),
        compiler_params=_cparams("parallel", "parallel"),
    )(x, gpre, wq, mk, mv, wo, gpost)


def _ffn_kernel(x_ref, gpre_ref, wu_ref, wv_ref, wo_ref, gpost_ref, y_ref, xn_ref, acc_ref):
    j = pl.program_id(1)

    @pl.when(j == 0)
    def _():
        xn_ref[...] = _rms(x_ref[...], gpre_ref[...]).astype(BF)
        acc_ref[...] = jnp.zeros(acc_ref.shape, F32)

    xn = xn_ref[...]
    u = _dg(xn, wu_ref[...], _NN)
    v = _dg(xn, wv_ref[...], _NN)
    acc_ref[...] += _dot(u * jax.nn.sigmoid(u) * v, wo_ref[...])

    @pl.when(j == pl.num_programs(1) - 1)
    def _():
        y_ref[...] = x_ref[...] + _rms(acc_ref[...], gpost_ref[...])


def _ffn(x, gpre, w_in, w_out, gpost, tm, th):
    n, d = x.shape
    hidden = w_out.shape[0]
    nh = hidden // th
    row = pl.BlockSpec((tm, d), lambda i, j: (i, 0))
    return pl.pallas_call(
        _ffn_kernel,
        grid=(n // tm, nh),
        in_specs=[row, _const_spec((1, d)),
                  pl.BlockSpec((d, th), lambda i, j: (0, j)),
                  pl.BlockSpec((d, th), lambda i, j: (0, nh + j)),
                  pl.BlockSpec((th, d), lambda i, j: (j, 0)),
                  _const_spec((1, d))],
        out_specs=row,
        out_shape=jax.ShapeDtypeStruct((n, d), F32),
        scratch_shapes=[pltpu.VMEM((tm, d), BF), pltpu.VMEM((tm, d), F32)],
        compiler_params=_cparams("parallel", "arbitrary"),
    )(x, gpre, w_in, w_in, w_out, gpost)


def _tile(n, pref):
    t = min(n, pref)
    assert n % t == 0, (n, t)
    return t


def _layer_weights(l, wts):
    p = {k: v[l] for k, v in wts.items()}
    w = BRANCH_W
    w_in = p["w_in"].astype(BF)
    w_in_t = p["w_in"].T.astype(BF)
    gate_w = w_in.shape[1] - (6 * w + FOX_HEADS + RWKV_IN)
    o_fox, o_f, o_diff = 0, 3 * w, 3 * w + FOX_HEADS
    o_rwkv = o_diff + 3 * w
    o_gate = o_rwkv + RWKV_IN
    row = lambda a: a.reshape(1, -1)
    seg = np.kron(np.eye(RWKV_HEADS, dtype=np.float32), np.ones((HEAD_DIM, HEAD_DIM), np.float32))
    return dict(
        g_mix_pre=row(p["norm_mix_pre"]), g_mix_post=row(p["norm_mix_post"]),
        w_fox_q=w_in[:, o_fox:o_fox + w], w_fox_kv_t=w_in_t[o_fox + w:o_fox + 3 * w],
        w_fox_f_t=w_in_t[o_f:o_f + FOX_HEADS],
        w_diff_q=w_in[:, o_diff:o_diff + w], w_diff_k_t=w_in_t[o_diff + w:o_diff + 2 * w],
        w_diff_v=w_in[:, o_diff + 2 * w:o_diff + 3 * w],
        w_rwkv=w_in[:, o_rwkv:o_rwkv + RWKV_IN], w_gate=w_in[:, o_gate:o_gate + gate_w],
        fox_bias=p["fox_forget_bias"].reshape(FOX_HEADS, 1),
        diff_lambda=p["diff_lambda"], diff_subln=row(p["diff_subln"]),
        rwkv_vecs=tuple(row(p[k]) for k in ("rwkv_mu", "rwkv_w0", "rwkv_a0", "rwkv_kk_scale", "rwkv_ka",
                                            "rwkv_rk")),
        rwkv_mats=tuple(p[k].astype(BF) for k in ("rwkv_w2", "rwkv_a2", "rwkv_g2")),
        rwkv_seg=jnp.asarray(seg, BF),
        rwkv_gain=row(p["rwkv_ln_gain"]), rwkv_bias=row(p["rwkv_ln_bias"]),
        w_branch=p["w_branch"].astype(BF), w_out=p["w_out"].astype(BF),
        g_x_pre=row(p["norm_x_pre"]), g_x_post=row(p["norm_x_post"]), g_mem=row(p["norm_mem"]),
        w_xq=p["w_xq"].astype(BF), w_xkv=p["w_xkv"].astype(BF), w_xo=p["w_xo"].astype(BF),
        g_ffn_pre=row(p["norm_ffn_pre"]), g_ffn_post=row(p["norm_ffn_post"]),
        w_ffn_in=p["w_ffn_in"].astype(BF), w_ffn_out=p["w_ffn_out"].astype(BF),
    )


def _mixer(x, lw, l, b, t, past, tabs):
    n = x.shape[0]
    lam_init = 0.8 - 0.6 * math.exp(-0.3 * l)
    tm = _tile(t, 512)
    qf, kf_t, vf_t, f_bt = _fox_inproj(x, lw["g_mix_pre"], lw["w_fox_q"], lw["w_fox_kv_t"], lw["w_fox_f_t"], b, tm)
    qd, kd_t, vd = _diff_inproj(x, lw["g_mix_pre"], lw["w_diff_q"], lw["w_diff_k_t"], lw["w_diff_v"], tabs, b, tm)

    if past is None:
        p_len = 0
        z = f_bt
        shift0 = jnp.zeros((b, 1, RWKV_IN), F32)
        s0 = jnp.zeros((b, RWKV_HEADS, HEAD_DIM, HEAD_DIM), F32)
    else:
        fox_kt, fox_vt, fox_lf_t, diff_kt, diff_v, s0, shift0 = past
        p_len = fox_kt.shape[3]
        pad = (-(p_len + t)) % LANES
        z = jnp.concatenate([fox_lf_t[l], f_bt, jnp.zeros((b, FOX_HEADS, pad), F32)], axis=2)
        s0 = s0[l]
        shift0 = shift0[l].reshape(b, 1, RWKV_IN)
    lf, d_row = _logf_cumsum(z, lw["fox_bias"], p_len, t)
    log_f = lf[:, :, p_len:p_len + t].transpose(0, 2, 1)

    if past is None:
        d_col = d_row.transpose(0, 2, 1).reshape(n, FOX_HEADS)
        tq = _tile(t, 256)
        out_a = _fox_prompt(qf, kf_t, vf_t, d_col, d_row, b, tq)
        out_b = _diff_prompt(qd, kd_t, vd, lw["diff_lambda"], lw["diff_subln"], lam_init, b, tq)
    else:
        assert p_len % CHUNK == 0 and t <= CHUNK and p_len % LANES == 0
        tk = _tile(p_len, 1024)
        dq = d_row[:, :, p_len:p_len + t].reshape(b, GROUPS * t, 1)
        out_a = _sample_attn(_block_diag_queries(qf, b, t), fox_kt, fox_vt, kf_t, vf_t, (dq, d_row),
                             layer=l, fox=True, b=b, t_new=t, past=p_len, tk=tk)
        out_b = _sample_attn(_block_diag_queries(qd, b, t), diff_kt, diff_v, kd_t, vd,
                             (lw["diff_lambda"], lw["diff_subln"]),
                             layer=l, fox=False, b=b, t_new=t, past=p_len, tk=tk, lam_init=lam_init)

    r, lgw, km, v, kk, kka, gg, bonus, shift = _rwkv_prep(
        x, lw["g_mix_pre"], lw["w_rwkv"], shift0, lw["rwkv_vecs"], lw["rwkv_mats"], lw["rwkv_seg"], b, tm)
    ch = _tile(t, CHUNK)
    out_c, s_new = _rwkv_scan(r, lgw, km, v, kk, kka, gg, bonus, lw["rwkv_gain"], lw["rwkv_bias"], s0, b, ch)

    y = _merge(x, out_a, out_b, out_c, lw["g_mix_pre"], lw["w_gate"], lw["w_branch"], lw["w_out"],
               lw["g_mix_post"], _tile(n, 256))
    state = (kf_t, vf_t, log_f, kd_t, vd, s_new, shift.reshape(b, RWKV_IN))
    return y, state


def _layer(x, lw, l, b, t, past, mem_k, mem_v, mem_layer, tabs):
    n = x.shape[0]
    x, state = _mixer(x, lw, l, b, t, past, tabs)
    x = _xattn(x, lw["g_x_pre"], lw["w_xq"], mem_k, mem_v, lw["w_xo"], lw["g_x_post"], mem_layer, b,
               _tile(t, 512))
    x = _ffn(x, lw["g_ffn_pre"], lw["w_ffn_in"], lw["w_ffn_out"], lw["g_ffn_post"], _tile(n, 1024), 256)
    return x, state


def _assemble_states(states, b, t):
    kf_t, vf_t, log_f, kd_t, vd, s_new, shift = (jnp.stack(e) for e in zip(*states))
    depth = kf_t.shape[0]
    tok_major = lambda a: a.reshape(depth, b, FOX_HEADS, HEAD_DIM, t).transpose(0, 1, 4, 2, 3)
    fox_k, fox_v = tok_major(kf_t), tok_major(vf_t)
    diff_k = tok_major(kd_t).reshape(depth, b, t, DIFF_HEADS, 2, HEAD_DIM)
    diff_v = vd.reshape(depth, b, t, DIFF_HEADS, 2 * HEAD_DIM)
    return fox_k, fox_v, log_f, diff_k, diff_v, s_new, shift


def kernel(x_prompt, x_sample, mem_prompt, cache_fox_k, cache_fox_v, cache_fox_logf, cache_diff_k, cache_diff_v, state_rwkv, state_rwkv_shift, cache_mem_k, cache_mem_v, norm_mix_pre, norm_mix_post, w_in, fox_forget_bias, diff_lambda, diff_subln, rwkv_mu, rwkv_w0, rwkv_w2, rwkv_a0, rwkv_a2, rwkv_g2, rwkv_kk_scale, rwkv_ka, rwkv_rk, rwkv_ln_gain, rwkv_ln_bias, w_branch, w_out, norm_x_pre, norm_x_post, norm_mem, w_xq, w_xkv, w_xo, norm_ffn_pre, norm_ffn_post, w_ffn_in, w_ffn_out):
    wts = dict(norm_mix_pre=norm_mix_pre, norm_mix_post=norm_mix_post, w_in=w_in, fox_forget_bias=fox_forget_bias,
               diff_lambda=diff_lambda, diff_subln=diff_subln, rwkv_mu=rwkv_mu, rwkv_w0=rwkv_w0, rwkv_w2=rwkv_w2,
               rwkv_a0=rwkv_a0, rwkv_a2=rwkv_a2, rwkv_g2=rwkv_g2, rwkv_kk_scale=rwkv_kk_scale, rwkv_ka=rwkv_ka,
               rwkv_rk=rwkv_rk, rwkv_ln_gain=rwkv_ln_gain, rwkv_ln_bias=rwkv_ln_bias, w_branch=w_branch,
               w_out=w_out, norm_x_pre=norm_x_pre, norm_x_post=norm_x_post, norm_mem=norm_mem, w_xq=w_xq,
               w_xkv=w_xkv, w_xo=w_xo, norm_ffn_pre=norm_ffn_pre, norm_ffn_post=norm_ffn_post,
               w_ffn_in=w_ffn_in, w_ffn_out=w_ffn_out)
    depth = w_in.shape[0]
    bp, tp, d = x_prompt.shape
    bs, ts, _ = x_sample.shape
    p_len = cache_fox_k.shape[2]
    n_mem = mem_prompt.shape[1]
    xw = XATTN_HEADS * XATTN_HEAD_DIM
    w = BRANCH_W

    tabs_p = _rotary_tables(jnp.arange(tp, dtype=jnp.int32))
    tabs_s = _rotary_tables(p_len + jnp.arange(ts, dtype=jnp.int32))

    past_s = (cache_fox_k.transpose(0, 1, 3, 4, 2).reshape(depth, bs, w, p_len),
              cache_fox_v.transpose(0, 1, 3, 4, 2).reshape(depth, bs, w, p_len),
              cache_fox_logf.transpose(0, 1, 3, 2),
              cache_diff_k.transpose(0, 1, 3, 4, 5, 2).reshape(depth, bs, w, p_len),
              cache_diff_v.reshape(depth, bs, p_len * DIFF_HEADS, 2 * HEAD_DIM),
              state_rwkv, state_rwkv_shift)
    mem_k_s = cache_mem_k.reshape(depth, bs, n_mem, xw)
    mem_v_s = cache_mem_v.reshape(depth, bs, n_mem, xw)

    xp = x_prompt.reshape(bp * tp, d)
    xs = x_sample.reshape(bs * ts, d)
    mem = mem_prompt.reshape(bp * n_mem, d)
    p_new, s_new, p_mk, p_mv = [], [], [], []
    for l in range(depth):
        lw = _layer_weights(l, wts)
        mk, mv = _memkv(mem, lw["g_mem"], lw["w_xkv"], _tile(bp * n_mem, 256))
        xp, st_p = _layer(xp, lw, l, bp, tp, None, mk.reshape(1, bp, n_mem, xw), mv.reshape(1, bp, n_mem, xw), 0,
                          tabs_p)
        xs, st_s = _layer(xs, lw, l, bs, ts, past_s, mem_k_s, mem_v_s, l, tabs_s)
        p_new.append(st_p)
        s_new.append(st_s)
        p_mk.append(mk.reshape(bp, n_mem, XATTN_HEADS, XATTN_HEAD_DIM))
        p_mv.append(mv.reshape(bp, n_mem, XATTN_HEADS, XATTN_HEAD_DIM))
    p_out = _assemble_states(p_new, bp, tp)
    s_out = _assemble_states(s_new, bs, ts)
    return (xp.reshape(bp, tp, d), xs.reshape(bs, ts, d)) + p_out + (jnp.stack(p_mk), jnp.stack(p_mv)) + s_out
```

```python
import functools
import math

import numpy as np
import jax
import jax.numpy as jnp
from jax import lax
from jax.experimental import pallas as pl
from jax.experimental.pallas import tpu as pltpu

F32 = jnp.float32
BF = jnp.bfloat16

CHUNK = 64
HEAD_DIM = 64
FOX_HEADS = 8
DIFF_HEADS = 4
RWKV_HEADS = 8
GROUPS = 8
BRANCH_W = 512
DECAY_LORA = 64
AAA_LORA = 64
GATE_LORA = 128
RWKV_IN = 3 * BRANCH_W + DECAY_LORA + AAA_LORA + GATE_LORA
ROT_DIM = HEAD_DIM // 4
ROPE_THETA = 500000.0
XATTN_HEADS = 4
XATTN_HEAD_DIM = 128
RMS_EPS = 1e-6
RWKV_GN_EPS = 64e-5
ATTN_SCALE = HEAD_DIM ** -0.5
NEG_BIG = -1e30

LANES = 128
SUBLANES = 8
VMEM_LIMIT_BYTES = 56 * 1024 * 1024


def _cparams(*sem):
    return pltpu.CompilerParams(dimension_semantics=sem, vmem_limit_bytes=VMEM_LIMIT_BYTES)


def _const_spec(shape):
    nd = len(shape)
    return pl.BlockSpec(shape, lambda *_: (0,) * nd)


def _rms(x, g):
    return x * lax.rsqrt(jnp.mean(x * x, axis=-1, keepdims=True) + RMS_EPS) * g


_NN = ((1,), (0,))
_NT = ((1,), (1,))
_TN = ((0,), (0,))


def _dg(a, b, dims):
    return lax.dot_general(a, b, (dims, ((), ())), preferred_element_type=F32)


def _dot(a, b):
    return _dg(a.astype(BF), b.astype(BF), _NN)


def _dot_nt(a, b):
    return _dg(a.astype(BF), b.astype(BF), _NT)


def _split2(x):
    hi = x.astype(BF)
    lo = (x - hi.astype(F32)).astype(BF)
    return hi, lo


def _split3(x):
    hi = x.astype(BF)
    r1 = x - hi.astype(F32)
    mid = r1.astype(BF)
    lo = (r1 - mid.astype(F32)).astype(BF)
    return hi, mid, lo


def _dot_sel(sel, x):
    hi, mid, lo = _split3(x)
    return _dg(sel, hi, _NN) + (_dg(sel, mid, _NN) + _dg(sel, lo, _NN))


def _dot_sel_right(x, sel):
    hi, mid, lo = _split3(x)
    return _dg(hi, sel, _NN) + (_dg(mid, sel, _NN) + _dg(lo, sel, _NN))


def _softplus(x):
    return jnp.maximum(x, 0.0) + jnp.log(1.0 + jnp.exp(-jnp.abs(x)))


def _fox_inproj_kernel(x_ref, g_ref, wq_ref, wkv_t_ref, wf_t_ref, q_ref, kt_ref, vt_ref, f_ref):
    xn = _rms(x_ref[...], g_ref[...]).astype(BF)
    w = BRANCH_W
    q_ref[...] = (_dg(xn, wq_ref[...], _NN) * ATTN_SCALE).astype(BF)
    kt_ref[0] = _dg(wkv_t_ref[0:w, :], xn, _NT)
    vt_ref[0] = _dg(wkv_t_ref[w:2 * w, :], xn, _NT)
    f_ref[0] = _dg(wf_t_ref[...], xn, _NT)


def _row_specs(b, t, tm):
    nt = t // tm
    row = lambda width: pl.BlockSpec((tm, width), lambda bi, ti: (bi * nt + ti, 0))
    col = lambda height: pl.BlockSpec((1, height, tm), lambda bi, ti: (bi, 0, ti))
    return nt, row, col


def _fox_inproj(x, g, wq, wkv_t, wf_t, b, tm):
    n, d = x.shape
    t = n // b
    w = BRANCH_W
    nt, row, col = _row_specs(b, t, tm)
    return pl.pallas_call(
        _fox_inproj_kernel,
        grid=(b, nt),
        in_specs=[row(d), _const_spec((1, d)), _const_spec((d, w)), _const_spec((2 * w, d)),
                  _const_spec((FOX_HEADS, d))],
        out_specs=[row(w), col(w), col(w), col(FOX_HEADS)],
        out_shape=[jax.ShapeDtypeStruct((n, w), BF), jax.ShapeDtypeStruct((b, w, t), F32),
                   jax.ShapeDtypeStruct((b, w, t), F32), jax.ShapeDtypeStruct((b, FOX_HEADS, t), F32)],
        compiler_params=_cparams("parallel", "parallel"),
    )(x, g, wq, wkv_t, wf_t)


def _diff_inproj_kernel(x_ref, g_ref, wq_ref, wk_t_ref, wv_ref, cos_ref, sa_ref, sb_ref, cos_t_ref, sin_t_ref,
                        q_ref, kt_ref, v_ref):
    xn = _rms(x_ref[...], g_ref[...]).astype(BF)
    w = BRANCH_W
    half = ROT_DIM // 2
    cos, sa, sb = cos_ref[...], sa_ref[...], sb_ref[...]
    yq = _dg(xn, wq_ref[...], _NN)
    for j in range(w // LANES):
        yj = yq[:, j * LANES:(j + 1) * LANES]
        rj = yj * cos + pltpu.roll(yj, LANES - half, 1) * sa + pltpu.roll(yj, half, 1) * sb
        q_ref[:, j * LANES:(j + 1) * LANES] = (rj * ATTN_SCALE).astype(BF)
    yk = _dg(wk_t_ref[...], xn, _NT)
    cos_t, sin_t = cos_t_ref[...], sin_t_ref[...]
    for gi in range(GROUPS):
        r0 = gi * HEAD_DIM
        y1, y2 = yk[r0:r0 + half, :], yk[r0 + half:r0 + ROT_DIM, :]
        kt_ref[0, r0:r0 + half, :] = y1 * cos_t - y2 * sin_t
        kt_ref[0, r0 + half:r0 + ROT_DIM, :] = y2 * cos_t + y1 * sin_t
        kt_ref[0, r0 + ROT_DIM:r0 + HEAD_DIM, :] = yk[r0 + ROT_DIM:r0 + HEAD_DIM, :]
    v_ref[...] = _dg(xn, wv_ref[...], _NN)


def _diff_inproj(x, g, wq, wk_t, wv, tabs, b, tm):
    n, d = x.shape
    t = n // b
    w = BRANCH_W
    nt, row, col = _row_specs(b, t, tm)
    row_tab = pl.BlockSpec((tm, LANES), lambda bi, ti: (ti, 0))
    col_tab = pl.BlockSpec((ROT_DIM // 2, tm), lambda bi, ti: (0, ti))
    return pl.pallas_call(
        _diff_inproj_kernel,
        grid=(b, nt),
        in_specs=[row(d), _const_spec((1, d)), _const_spec((d, w)), _const_spec((w, d)), _const_spec((d, w)),
                  row_tab, row_tab, row_tab, col_tab, col_tab],
        out_specs=[row(w), col(w), row(w)],
        out_shape=[jax.ShapeDtypeStruct((n, w), BF), jax.ShapeDtypeStruct((b, w, t), F32),
                   jax.ShapeDtypeStruct((n, w), F32)],
        compiler_params=_cparams("parallel", "parallel"),
    )(x, g, wq, wk_t, wv, *tabs)


def _rotary_tables(pos):
    half = ROT_DIM // 2
    inv = ROPE_THETA ** (-np.arange(half, dtype=np.float32) / half)
    ang = pos.astype(F32)[:, None] * jnp.asarray(inv, F32)[None, :]
    cos, sin = jnp.cos(ang), jnp.sin(ang)
    t = pos.shape[0]
    one = jnp.ones((t, HEAD_DIM - ROT_DIM), F32)
    zero = jnp.zeros((t, HEAD_DIM - ROT_DIM), F32)
    zh = jnp.zeros((t, half), F32)
    cos_h = jnp.concatenate([cos, cos, one], axis=1)
    sa_h = jnp.concatenate([-sin, zh, zero], axis=1)
    sb_h = jnp.concatenate([zh, sin, zero], axis=1)
    dup = lambda a: jnp.concatenate([a, a], axis=1)
    return dup(cos_h), dup(sa_h), dup(sb_h), cos.T, sin.T


def _rwkv_prep_kernel(x_ref, g_ref, wc_ref, shift0_ref, mu_ref, w0_ref, w2_ref, a0_ref, a2_ref, g2_ref,
                      kks_ref, ka_ref, rk_ref, seg_ref,
                      r_ref, lw_ref, km_ref, v_ref, kk_ref, kka_ref, gg_ref, bonus_ref, shift_ref,
                      carry_ref):
    t = pl.program_id(1)
    tm = x_ref.shape[0]
    w = BRANCH_W
    xn = _rms(x_ref[...], g_ref[...]).astype(BF)
    c = _dg(xn, wc_ref[...], _NN)

    @pl.when(t == 0)
    def _():
        carry_ref[...] = shift0_ref[0]

    row = lax.broadcasted_iota(jnp.int32, (tm, 1), 0)
    prev = jnp.where(row == 0, carry_ref[...], pltpu.roll(c, 1, 0))
    last = c[tm - 1:tm, :]
    carry_ref[...] = last
    shift_ref[0] = last

    xs = c + mu_ref[...] * (prev - c)
    cr, ck, cv = xs[:, 0:w], xs[:, w:2 * w], xs[:, 2 * w:3 * w]
    o = 3 * w
    xw = xs[:, o:o + DECAY_LORA]
    xa = xs[:, o + DECAY_LORA:o + DECAY_LORA + AAA_LORA]
    xg = xs[:, o + DECAY_LORA + AAA_LORA:]
    z = w0_ref[...] + _dot(jnp.tanh(xw), w2_ref[...])
    w_raw = -_softplus(-z) - 0.5
    a = jax.nn.sigmoid(a0_ref[...] + _dot(xa, a2_ref[...]))
    seg = seg_ref[...]
    kk = ck * kks_ref[...]
    kk = kk / jnp.maximum(jnp.sqrt(_dot_sel_right(kk * kk, seg)), 1e-12)
    kmod = ck * (1.0 + (a - 1.0) * ka_ref[...])
    r_ref[...] = cr
    lw_ref[...] = -jnp.exp(w_raw)
    km_ref[...] = kmod
    v_ref[...] = cv
    kk_ref[...] = kk
    kka_ref[...] = kk * a
    gg_ref[...] = _dot(jax.nn.sigmoid(xg), g2_ref[...])
    bonus_ref[...] = _dot_sel_right(cr * kmod * rk_ref[...], seg) * cv


def _rwkv_prep(x, g, wc, shift0, vecs, mats, seg, b, tm):
    n, d = x.shape
    t = n // b
    w = BRANCH_W
    mu, w0, a0, kks, ka, rk = vecs
    w2, a2, g2 = mats
    nt, row, _ = _row_specs(b, t, tm)
    per_b = pl.BlockSpec((1, 1, RWKV_IN), lambda bi, ti: (bi, 0, 0))
    outs = [jax.ShapeDtypeStruct((n, w), F32)] * 8 + [jax.ShapeDtypeStruct((b, 1, RWKV_IN), F32)]
    return pl.pallas_call(
        _rwkv_prep_kernel,
        grid=(b, nt),
        in_specs=[row(d), _const_spec((1, d)), _const_spec((d, RWKV_IN)), per_b,
                  _const_spec((1, RWKV_IN)), _const_spec((1, w)), _const_spec((DECAY_LORA, w)),
                  _const_spec((1, w)), _const_spec((AAA_LORA, w)), _const_spec((GATE_LORA, w)),
                  _const_spec((1, w)), _const_spec((1, w)), _const_spec((1, w)), _const_spec((w, w))],
        out_specs=[row(w)] * 8 + [per_b],
        out_shape=outs,
        scratch_shapes=[pltpu.VMEM((1, RWKV_IN), F32)],
        compiler_params=_cparams("parallel", "arbitrary"),
    )(x, g, wc, shift0, mu, w0, w2, a0, a2, g2, kks, ka, rk, seg)


def _logf_cumsum_kernel(z_ref, bias_ref, lf_ref, d_ref, *, past, new):
    z = z_ref[0]
    width = z.shape[1]
    col = lax.broadcasted_iota(jnp.int32, z.shape, 1)
    zz = z + bias_ref[...]
    log_sig = jnp.minimum(zz, 0.0) - jnp.log(1.0 + jnp.exp(-jnp.abs(zz)))
    lf = jnp.where(col < past, z, jnp.where(col < past + new, log_sig, 0.0))
    lf_ref[0] = lf
    x = lf
    s = 1
    while s < width:
        x = x + jnp.where(col >= s, pltpu.roll(x, s, 1), 0.0)
        s *= 2
    d_ref[0] = x


def _logf_cumsum(z, bias, past, new):
    b, h, width = z.shape
    blk = pl.BlockSpec((1, h, width), lambda i: (i, 0, 0))
    return pl.pallas_call(
        functools.partial(_logf_cumsum_kernel, past=past, new=new),
        grid=(b,),
        in_specs=[blk, _const_spec((h, 1))],
        out_specs=[blk, blk],
        out_shape=[jax.ShapeDtypeStruct(z.shape, F32)] * 2,
        compiler_params=_cparams("parallel"),
    )(z, bias)


def _softmax_update(s, m_ref, l_ref, idx):
    m_prev = m_ref[idx]
    m_new = jnp.maximum(m_prev, jnp.max(s, axis=-1, keepdims=True))
    alpha = jnp.exp(m_prev - m_new)
    p = jnp.exp(s - m_new)
    l_ref[idx] = alpha * l_ref[idx] + jnp.sum(p, axis=-1, keepdims=True)
    m_ref[idx] = m_new
    return p.astype(BF), alpha


def _init_softmax_state(m_ref, l_ref, acc_ref):
    m_ref[...] = jnp.full(m_ref.shape, NEG_BIG, F32)
    l_ref[...] = jnp.zeros(l_ref.shape, F32)
    acc_ref[...] = jnp.zeros(acc_ref.shape, F32)


def _softmax_update_rep(s, m_ref, l_ref, idx):
    reps = s.shape[1] // LANES
    m_prev = m_ref[idx]
    m_new = jnp.maximum(m_prev, jnp.max(s, axis=-1, keepdims=True))
    alpha = jnp.exp(m_prev - m_new)
    p = jnp.exp(s - pltpu.repeat(m_new, reps, axis=1))
    l_ref[idx] = alpha * l_ref[idx] + jnp.sum(p, axis=-1, keepdims=True)
    m_ref[idx] = m_new
    return p.astype(BF), alpha


def _causal_blocks(i, j, block):
    pl.when(j < i)(functools.partial(block, False))
    pl.when(j == i)(functools.partial(block, True))


def _fox_prompt_kernel(q_ref, kt_ref, vt_ref, dk_ref, o_ref, m_ref, l_ref, acc_ref):
    i, j = pl.program_id(1), pl.program_id(2)
    tq, tk = q_ref.shape[0], kt_ref.shape[2]
    heads = [slice(h * HEAD_DIM, (h + 1) * HEAD_DIM) for h in range(FOX_HEADS)]

    @pl.when(j == 0)
    def _():
        _init_softmax_state(m_ref, l_ref, acc_ref)

    def block(masked):
        kb = kt_ref[0].astype(BF)
        vb = vt_ref[0].astype(BF)
        dk = dk_ref[0]
        scores = [_dg(q_ref[:, hs], kb[hs, :], _NN) for hs in heads]
        if masked:
            visible = (lax.broadcasted_iota(jnp.int32, (tq, tk), 1) <= lax.broadcasted_iota(jnp.int32, (tq, tk), 0))
        for h, hs in enumerate(heads):
            s = scores[h] - dk[h:h + 1, :]
            if masked:
                s = jnp.where(visible, s, NEG_BIG)
            p, alpha = _softmax_update_rep(s, m_ref, l_ref, h)
            acc_ref[h] = alpha[:, 0:HEAD_DIM] * acc_ref[h] + _dg(p, vb[hs, :], _NT)

    _causal_blocks(i, j, block)

    @pl.when(j == pl.num_programs(2) - 1)
    def _():
        for h, hs in enumerate(heads):
            o_ref[:, hs] = acc_ref[h] / l_ref[h][:, 0:HEAD_DIM]


def _fox_prompt(q, kt, vt, d_row, b, tq):
    n, w = q.shape
    t = n // b
    nq = t // tq
    qspec = pl.BlockSpec((tq, w), lambda bi, i, j: (bi * nq + i, 0))
    kspec = pl.BlockSpec((1, w, tq), lambda bi, i, j: (bi, 0, jnp.minimum(j, i)))
    return pl.pallas_call(
        _fox_prompt_kernel,
        grid=(b, nq, nq),
        in_specs=[qspec, kspec, kspec,
                  pl.BlockSpec((1, FOX_HEADS, tq), lambda bi, i, j: (bi, 0, jnp.minimum(j, i)))],
        out_specs=qspec,
        out_shape=jax.ShapeDtypeStruct((n, w), F32),
        scratch_shapes=[pltpu.VMEM((FOX_HEADS, tq, LANES), F32), pltpu.VMEM((FOX_HEADS, tq, LANES), F32),
                        pltpu.VMEM((FOX_HEADS, tq, HEAD_DIM), F32)],
        compiler_params=_cparams("parallel", "parallel", "arbitrary"),
    )(q, kt, vt, d_row)


def _diff_lambda(lam_ref, lam_init):
    p = lam_ref[...]
    s1 = jnp.sum(p[0:1, :] * p[1:2, :], axis=-1, keepdims=True)
    s2 = jnp.sum(p[2:3, :] * p[3:4, :], axis=-1, keepdims=True)
    return jnp.exp(s1) - jnp.exp(s2) + lam_init


def _diff_combine(acc0, l0, acc1, l1, lam, subln, lam_init):
    o = acc0 / l0 - lam * (acc1 / l1)
    return _rms(o, subln) * (1.0 - lam_init)


def _diff_prompt_kernel(q_ref, kt_ref, v_ref, lam_ref, subln_ref, o_ref, m_ref, l_ref, acc_ref, *, lam_init):
    i, j = pl.program_id(1), pl.program_id(2)
    tq, tk = q_ref.shape[0], kt_ref.shape[2]
    dv = 2 * HEAD_DIM

    @pl.when(j == 0)
    def _():
        _init_softmax_state(m_ref, l_ref, acc_ref)

    def block(masked):
        kb = kt_ref[0].astype(BF)
        vb = v_ref[...].astype(BF)
        groups = [slice(gi * HEAD_DIM, (gi + 1) * HEAD_DIM) for gi in range(GROUPS)]
        scores = [_dg(q_ref[:, gs], kb[gs, :], _NN) for gs in groups]
        if masked:
            visible = ((lax.broadcasted_iota(jnp.int32, (tq, tk), 1) // CHUNK)
                       <= (lax.broadcasted_iota(jnp.int32, (tq, tk), 0) // CHUNK))
        for gi in range(GROUPS):
            s = jnp.where(visible, scores[gi], NEG_BIG) if masked else scores[gi]
            p, alpha = _softmax_update_rep(s, m_ref, l_ref, gi)
            h = gi // 2
            acc_ref[gi] = alpha * acc_ref[gi] + _dg(p, vb[:, h * dv:(h + 1) * dv], _NN)

    _causal_blocks(i, j, block)

    @pl.when(j == pl.num_programs(2) - 1)
    def _():
        lam = _diff_lambda(lam_ref, lam_init)
        for h in range(DIFF_HEADS):
            o_ref[:, h * dv:(h + 1) * dv] = _diff_combine(
                acc_ref[2 * h], l_ref[2 * h], acc_ref[2 * h + 1], l_ref[2 * h + 1], lam, subln_ref[...], lam_init)


def _diff_prompt(q, kt, v, lam_p, subln, lam_init, b, tq):
    n, w = q.shape
    t = n // b
    nq = t // tq
    qspec = pl.BlockSpec((tq, w), lambda bi, i, j: (bi * nq + i, 0))
    ktspec = pl.BlockSpec((1, w, tq), lambda bi, i, j: (bi, 0, jnp.minimum(j, i)))
    vspec = pl.BlockSpec((tq, w), lambda bi, i, j: (bi * nq + jnp.minimum(j, i), 0))
    return pl.pallas_call(
        functools.partial(_diff_prompt_kernel, lam_init=lam_init),
        grid=(b, nq, nq),
        in_specs=[qspec, ktspec, vspec, _const_spec((4, HEAD_DIM)), _const_spec((1, 2 * HEAD_DIM))],
        out_specs=qspec,
        out_shape=jax.ShapeDtypeStruct((n, w), F32),
        scratch_shapes=[pltpu.VMEM((GROUPS, tq, LANES), F32), pltpu.VMEM((GROUPS, tq, LANES), F32),
                        pltpu.VMEM((GROUPS, tq, 2 * HEAD_DIM), F32)],
        compiler_params=_cparams("parallel", "parallel", "arbitrary"),
    )(q, kt, v, lam_p, subln)


def _sample_attn_kernel(*refs, fox, t_new, lam_init):
    if fox:
        q_ref, kc_ref, vc_ref, kn_ref, vn_ref, dq_ref, dkc_ref, dkn_ref, o_ref, m_ref, l_ref, acc_ref = refs
    else:
        q_ref, kc_ref, vc_ref, kn_ref, vn_ref, lam_ref, subln_ref, o_ref, m_ref, l_ref, acc_ref = refs
    j = pl.program_id(1)
    dv = 2 * HEAD_DIM
    pair = 2 * t_new

    @pl.when(j == 0)
    def _():
        _init_softmax_state(m_ref, l_ref, acc_ref)

    def expand(dk):
        return jnp.concatenate([jnp.broadcast_to(dk[gi:gi + 1, :], (t_new, dk.shape[1])) for gi in range(GROUPS)],
                               axis=0)

    def scores(kt, dk, causal):
        s = _dg(q_ref[0], kt.astype(BF), _NN)
        if fox:
            s = s + dq_ref[0] - expand(dk)
        if causal:
            qi = lax.broadcasted_iota(jnp.int32, s.shape, 0) % t_new
            kj = lax.broadcasted_iota(jnp.int32, s.shape, 1)
            s = jnp.where(kj <= qi, s, NEG_BIG)
        return _softmax_update(s, m_ref, l_ref, 0)

    def accumulate_diff(p, alpha, value_of_head):
        for h in range(DIFF_HEADS):
            rs = slice(h * pair, (h + 1) * pair)
            acc_ref[0, rs, :] = alpha[rs] * acc_ref[0, rs, :] + _dg(p[rs], value_of_head(h).astype(BF), _NN)

    tk = kc_ref.shape[3]
    if fox:
        p, alpha = scores(kc_ref[0, 0], dkc_ref[0], False)
        acc_ref[0] = alpha * acc_ref[0] + _dg(p, vc_ref[0, 0].astype(BF), _NT)
    else:
        p, alpha = scores(kc_ref[0, 0], None, False)
        accumulate_diff(p, alpha, lambda h: vc_ref[0, 0, pl.ds(h, tk, stride=DIFF_HEADS), :])

    @pl.when(j == pl.num_programs(1) - 1)
    def _():
        if fox:
            p, alpha = scores(kn_ref[0], dkn_ref[0][:, 0:t_new], True)
            acc = alpha * acc_ref[0] + _dg(p, vn_ref[0].astype(BF), _NT)
            l = l_ref[0]
            for gi in range(GROUPS):
                rs = slice(gi * t_new, (gi + 1) * t_new)
                cs = slice(gi * HEAD_DIM, (gi + 1) * HEAD_DIM)
                o_ref[:, cs] = acc[rs, cs] / l[rs]
        else:
            p, alpha = scores(kn_ref[0], None, False)
            accumulate_diff(p, alpha, lambda h: vn_ref[:, h * dv:(h + 1) * dv])
            acc = acc_ref[0]
            l = l_ref[0]
            lam = _diff_lambda(lam_ref, lam_init)
            for h in range(DIFF_HEADS):
                r0 = slice(h * pair, h * pair + t_new)
                r1 = slice(h * pair + t_new, (h + 1) * pair)
                o_ref[:, h * dv:(h + 1) * dv] = _diff_combine(acc[r0], l[r0], acc[r1], l[r1], lam, subln_ref[...],
                                                              lam_init)


def _sample_attn(qbd, kc, vc, kn, vn, extra, *, layer, fox, b, t_new, past, tk, lam_init=0.0):
    w = BRANCH_W
    rows = GROUPS * t_new
    nk = past // tk
    qspec = pl.BlockSpec((1, rows, w), lambda bi, j: (bi, 0, 0))
    kcspec = pl.BlockSpec((1, 1, w, tk), lambda bi, j: (layer, bi, 0, j))
    knspec = pl.BlockSpec((1, w, t_new), lambda bi, j: (bi, 0, 0))
    ospec = pl.BlockSpec((t_new, w), lambda bi, j: (bi, 0))
    if fox:
        dq, d_row = extra
        vcspec, vnspec, acc_w = kcspec, knspec, w
        especs = [pl.BlockSpec((1, rows, 1), lambda bi, j: (bi, 0, 0)),
                  pl.BlockSpec((1, GROUPS, tk), lambda bi, j: (bi, 0, j)),
                  pl.BlockSpec((1, GROUPS, LANES), lambda bi, j: (bi, 0, past // LANES))]
        eargs = [dq, d_row, d_row]
    else:
        vcspec = pl.BlockSpec((1, 1, DIFF_HEADS * tk, 2 * HEAD_DIM), lambda bi, j: (layer, bi, j, 0))
        vnspec, acc_w = ospec, 2 * HEAD_DIM
        especs = [_const_spec((4, HEAD_DIM)), _const_spec((1, 2 * HEAD_DIM))]
        eargs = list(extra)
    return pl.pallas_call(
        functools.partial(_sample_attn_kernel, fox=fox, t_new=t_new, lam_init=lam_init),
        grid=(b, nk),
        in_specs=[qspec, kcspec, vcspec, knspec, vnspec] + especs,
        out_specs=ospec,
        out_shape=jax.ShapeDtypeStruct((b * t_new, w), F32),
        scratch_shapes=[pltpu.VMEM((1, rows, 1), F32), pltpu.VMEM((1, rows, 1), F32),
                        pltpu.VMEM((1, rows, acc_w), F32)],
        compiler_params=_cparams("parallel", "arbitrary"),
    )(qbd, kc, vc, kn, vn, *eargs)


def _block_diag_queries(q, b, t_new):
    q4 = q.reshape(b, t_new, GROUPS, HEAD_DIM)
    eye = jnp.eye(GROUPS, dtype=q.dtype)
    qbd = q4.transpose(0, 2, 1, 3)[:, :, :, None, :] * eye[None, :, None, :, None]
    return qbd.reshape(b, GROUPS * t_new, GROUPS * HEAD_DIM)


def _rwkv_scan_kernel(r_ref, lw_ref, k_ref, v_ref, kk_ref, kka_ref, gg_ref, bonus_ref, gain_ref, bias_ref,
                      s0_ref, o_ref, sfin_ref, s_ref):
    c = pl.program_id(1)
    ch = r_ref.shape[0]
    n = HEAD_DIM

    @pl.when(c == 0)
    def _():
        s_ref[...] = s0_ref[0]

    ri = lax.broadcasted_iota(jnp.int32, (ch, ch), 0)
    ci = lax.broadcasted_iota(jnp.int32, (ch, ch), 1)
    tri = (ci <= ri).astype(BF)
    ri2 = lax.broadcasted_iota(jnp.int32, (ch, 2 * ch), 0)
    ci2 = lax.broadcasted_iota(jnp.int32, (ch, 2 * ch), 1)
    ci2m = jnp.where(ci2 >= ch, ci2 - ch, ci2)
    strict_right = (ci2 >= ch) & (ci2m < ri2)
    strict_left = ci < ri
    is_x = lax.broadcasted_iota(jnp.int32, (ch, ch + n), 1) >= ch
    incl_both = ci2m <= ri2
    steps = max(1, int(math.ceil(math.log2(ch))))
    heads = [slice(h * n, (h + 1) * n) for h in range(RWKV_HEADS)]

    lw = lw_ref[...]
    cum = _dot_sel(tri, lw)
    cum_end = cum[ch - 1:ch, :]
    kk, kka, kmod, v_all = kk_ref[...], kka_ref[...], k_ref[...], v_ref[...]
    e_neg = jnp.exp(-cum)
    e_tail = jnp.exp(cum_end - cum)
    ar = _split2(jnp.concatenate([-kk * jnp.exp(cum - lw), r_ref[...] * jnp.exp(cum)], axis=0))
    bk = _split2(jnp.concatenate([kka * e_neg, kmod * e_neg], axis=0))
    bk_tail = _split2(jnp.concatenate([kka * e_tail, kmod * e_tail], axis=0))
    vv = _split2(jnp.concatenate([v_all, v_all], axis=0))
    decay_end = jnp.exp(cum_end)
    s_old = [s_ref[h] for h in range(RWKV_HEADS)]

    def dot3s(a2, b2, dims):
        (ah, al), (bh, bl) = a2, b2
        return _dg(ah, bh, dims) + (_dg(ah, bl, dims) + _dg(al, bh, dims))

    cols = lambda a2, hs: (a2[0][:, hs], a2[1][:, hs])
    ga = []
    for h, hs in enumerate(heads):
        sh, sl = _split2(s_old[h])
        rhs = (jnp.concatenate([bk[0][:, hs], sh], axis=0), jnp.concatenate([bk[1][:, hs], sl], axis=0))
        ga.append(dot3s(cols(ar, hs), rhs, _NT))
    z = []
    for h, hs in enumerate(heads):
        g_top = ga[h][0:ch, 0:2 * ch]
        x0 = ga[h][0:ch, 2 * ch:] + dot3s(_split2(jnp.where(strict_right, g_top, 0.0)), cols(vv, hs), _NN)
        z.append(jnp.concatenate([jnp.where(strict_left, g_top[:, 0:ch], 0.0), x0], axis=1))
    for _ in range(steps):
        z = [dot3s(_split2(zh[:, 0:ch]), _split2(zh), _NN) + jnp.where(is_x, zh, 0.0) for zh in z]
    for h, hs in enumerate(heads):
        x = z[h][:, ch:]
        xv = _split2(jnp.concatenate([x, v_all[:, hs]], axis=0))
        y = ga[h][ch:, 2 * ch:] + dot3s(_split2(jnp.where(incl_both, ga[h][ch:, 0:2 * ch], 0.0)), xv, _NN)
        s_ref[h] = s_old[h] * decay_end[:, hs] + dot3s(xv, cols(bk_tail, hs), _TN)

        mu = jnp.mean(y, axis=-1, keepdims=True)
        yc = y - mu
        var = jnp.mean(yc * yc, axis=-1, keepdims=True)
        yn = yc * lax.rsqrt(var + RWKV_GN_EPS) * gain_ref[:, hs] + bias_ref[:, hs]
        o_ref[:, hs] = (yn + bonus_ref[:, hs]) * gg_ref[:, hs]

    @pl.when(c == pl.num_programs(1) - 1)
    def _():
        sfin_ref[0] = s_ref[...]


def _rwkv_scan(r, lw, km, v, kk, kka, gg, bonus, gain, bias, s0, b, ch):
    n, w = r.shape
    t = n // b
    nc = t // ch
    row = pl.BlockSpec((ch, w), lambda bi, ci: (bi * nc + ci, 0))
    st = pl.BlockSpec((1, RWKV_HEADS, HEAD_DIM, HEAD_DIM), lambda bi, ci: (bi, 0, 0, 0))
    return pl.pallas_call(
        _rwkv_scan_kernel,
        grid=(b, nc),
        in_specs=[row] * 8 + [_const_spec((1, w)), _const_spec((1, w)), st],
        out_specs=[row, st],
        out_shape=[jax.ShapeDtypeStruct((n, w), F32), jax.ShapeDtypeStruct(s0.shape, F32)],
        scratch_shapes=[pltpu.VMEM((RWKV_HEADS, HEAD_DIM, HEAD_DIM), F32)],
        compiler_params=_cparams("parallel", "arbitrary"),
    )(r, lw, km, v, kk, kka, gg, bonus, gain, bias, s0)


def _merge_kernel(x_ref, oa_ref, ob_ref, oc_ref, gpre_ref, wg_ref, wb_ref, wo_ref, gpost_ref, y_ref):
    x = x_ref[...]
    d = x.shape[1]
    xn = _rms(x, gpre_ref[...]).astype(BF)
    acc = jnp.zeros(x.shape, F32)
    for i, o_ref in enumerate((oa_ref, ob_ref, oc_ref)):
        gate = jax.nn.sigmoid(_dg(xn, wg_ref[:, i * d:(i + 1) * d], _NN))
        acc = acc + gate * _dot(o_ref[...], wb_ref[i])
    y = _dot(acc, wo_ref[...])
    y_ref[...] = x + _rms(y, gpost_ref[...])


def _merge(x, oa, ob, oc, gpre, wg, wb, wo, gpost, tm):
    n, d = x.shape
    w = BRANCH_W
    row = lambda width: pl.BlockSpec((tm, width), lambda i: (i, 0))
    return pl.pallas_call(
        _merge_kernel,
        grid=(n // tm,),
        in_specs=[row(d), row(w), row(w), row(w), _const_spec((1, d)), _const_spec((d, 3 * d)),
                  _const_spec((3, w, d)), _const_spec((d, d)), _const_spec((1, d))],
        out_specs=row(d),
        out_shape=jax.ShapeDtypeStruct((n, d), F32),
        compiler_params=_cparams("parallel"),
    )(x, oa, ob, oc, gpre, wg, wb, wo, gpost)


def _memkv_kernel(m_ref, g_ref, w_ref, k_ref, v_ref):
    xn = _rms(m_ref[...], g_ref[...]).astype(BF)
    w = k_ref.shape[1]
    k_ref[...] = _dg(xn, w_ref[:, 0:w], _NN)
    v_ref[...] = _dg(xn, w_ref[:, w:2 * w], _NN)


def _memkv(mem, g, w_xkv, tm):
    n, d = mem.shape
    w = w_xkv.shape[1] // 2
    row = lambda width: pl.BlockSpec((tm, width), lambda i: (i, 0))
    return pl.pallas_call(
        _memkv_kernel,
        grid=(n // tm,),
        in_specs=[row(d), _const_spec((1, d)), _const_spec((d, 2 * w))],
        out_specs=[row(w), row(w)],
        out_shape=[jax.ShapeDtypeStruct((n, w), F32)] * 2,
        compiler_params=_cparams("parallel"),
    )(mem, g, w_xkv)


def _xattn_kernel(x_ref, gpre_ref, wq_ref, mk_ref, mv_ref, wo_ref, gpost_ref, y_ref):
    x = x_ref[...]
    xn = _rms(x, gpre_ref[...])
    q = _dot(xn, wq_ref[...])
    hd = XATTN_HEAD_DIM
    outs = []
    for h in range(XATTN_HEADS):
        hs = slice(h * hd, (h + 1) * hd)
        s = _dot_nt(q[:, hs], mk_ref[0, 0, :, hs]) * (hd ** -0.5)
        p = jnp.exp(s - jnp.max(s, axis=-1, keepdims=True))
        p = p / jnp.sum(p, axis=-1, keepdims=True)
        outs.append(_dot(p, mv_ref[0, 0, :, hs]))
    o = jnp.concatenate(outs, axis=-1)
    y_ref[...] = x + _rms(_dot(o, wo_ref[...]), gpost_ref[...])


def _xattn(x, gpre, wq, mk, mv, wo, gpost, layer, b, tm):
    n, d = x.shape
    t = n // b
    nt = t // tm
    n_mem, w = mk.shape[2], mk.shape[3]
    row = pl.BlockSpec((tm, d), lambda bi, ti: (bi * nt + ti, 0))
    mem = pl.BlockSpec((1, 1, n_mem, w), lambda bi, ti: (layer, bi, 0, 0))
    return pl.pallas_call(
        _xattn_kernel,
        grid=(b, nt),
        in_specs=[row, _const_spec((1, d)), _const_spec((d, w)), mem, mem, _const_spec((w, d)),
                  _const_spec((1, d))],
        out_specs=row,
        out_shape=jax.ShapeDtypeStruct((n, d), F32),
        compiler_params=_cparams("parallel", "parallel"),
    )(x, gpre, wq, mk, mv, wo, gpost)


def _ffn_kernel(x_ref, gpre_ref, wu_ref, wv_ref, wo_ref, gpost_ref, y_ref, xn_ref, acc_ref):
    j = pl.program_id(1)

    @pl.when(j == 0)
    def _():
        xn_ref[...] = _rms(x_ref[...], gpre_ref[...]).astype(BF)
        acc_ref[...] = jnp.zeros(acc_ref.shape, F32)

    xn = xn_ref[...]
    u = _dg(xn, wu_ref[...], _NN)
    v = _dg(xn, wv_ref[...], _NN)
    acc_ref[...] += _dot(u * jax.nn.sigmoid(u) * v, wo_ref[...])

    @pl.when(j == pl.num_programs(1) - 1)
    def _():
        y_ref[...] = x_ref[...] + _rms(acc_ref[...], gpost_ref[...])


def _ffn(x, gpre, w_in, w_out, gpost, tm, th):
    n, d = x.shape
    hidden = w_out.shape[0]
    nh = hidden // th
    row = pl.BlockSpec((tm, d), lambda i, j: (i, 0))
    return pl.pallas_call(
        _ffn_kernel,
        grid=(n // tm, nh),
        in_specs=[row, _const_spec((1, d)),
                  pl.BlockSpec((d, th), lambda i, j: (0, j)),
                  pl.BlockSpec((d, th), lambda i, j: (0, nh + j)),
                  pl.BlockSpec((th, d), lambda i, j: (j, 0)),
                  _const_spec((1, d))],
        out_specs=row,
        out_shape=jax.ShapeDtypeStruct((n, d), F32),
        scratch_shapes=[pltpu.VMEM((tm, d), BF), pltpu.VMEM((tm, d), F32)],
        compiler_params=_cparams("parallel", "arbitrary"),
    )(x, gpre, w_in, w_in, w_out, gpost)


def _tile(n, pref):
    t = min(n, pref)
    assert n % t == 0, (n, t)
    return t


def _layer_weights(l, wts):
    p = {k: v[l] for k, v in wts.items()}
    w = BRANCH_W
    w_in = p["w_in"].astype(BF)
    w_in_t = p["w_in"].T.astype(BF)
    gate_w = w_in.shape[1] - (6 * w + FOX_HEADS + RWKV_IN)
    o_fox, o_f, o_diff = 0, 3 * w, 3 * w + FOX_HEADS
    o_rwkv = o_diff + 3 * w
    o_gate = o_rwkv + RWKV_IN
    row = lambda a: a.reshape(1, -1)
    seg = np.kron(np.eye(RWKV_HEADS, dtype=np.float32), np.ones((HEAD_DIM, HEAD_DIM), np.float32))
    return dict(
        g_mix_pre=row(p["norm_mix_pre"]), g_mix_post=row(p["norm_mix_post"]),
        w_fox_q=w_in[:, o_fox:o_fox + w], w_fox_kv_t=w_in_t[o_fox + w:o_fox + 3 * w],
        w_fox_f_t=w_in_t[o_f:o_f + FOX_HEADS],
        w_diff_q=w_in[:, o_diff:o_diff + w], w_diff_k_t=w_in_t[o_diff + w:o_diff + 2 * w],
        w_diff_v=w_in[:, o_diff + 2 * w:o_diff + 3 * w],
        w_rwkv=w_in[:, o_rwkv:o_rwkv + RWKV_IN], w_gate=w_in[:, o_gate:o_gate + gate_w],
        fox_bias=p["fox_forget_bias"].reshape(FOX_HEADS, 1),
        diff_lambda=p["diff_lambda"], diff_subln=row(p["diff_subln"]),
        rwkv_vecs=tuple(row(p[k]) for k in ("rwkv_mu", "rwkv_w0", "rwkv_a0", "rwkv_kk_scale", "rwkv_ka",
                                            "rwkv_rk")),
        rwkv_mats=tuple(p[k].astype(BF) for k in ("rwkv_w2", "rwkv_a2", "rwkv_g2")),
        rwkv_seg=jnp.asarray(seg, BF),
        rwkv_gain=row(p["rwkv_ln_gain"]), rwkv_bias=row(p["rwkv_ln_bias"]),
        w_branch=p["w_branch"].astype(BF), w_out=p["w_out"].astype(BF),
        g_x_pre=row(p["norm_x_pre"]), g_x_post=row(p["norm_x_post"]), g_mem=row(p["norm_mem"]),
        w_xq=p["w_xq"].astype(BF), w_xkv=p["w_xkv"].astype(BF), w_xo=p["w_xo"].astype(BF),
        g_ffn_pre=row(p["norm_ffn_pre"]), g_ffn_post=row(p["norm_ffn_post"]),
        w_ffn_in=p["w_ffn_in"].astype(BF), w_ffn_out=p["w_ffn_out"].astype(BF),
    )


def _mixer(x, lw, l, b, t, past, tabs):
    n = x.shape[0]
    lam_init = 0.8 - 0.6 * math.exp(-0.3 * l)
    tm = _tile(t, 512)
    qf, kf_t, vf_t, f_bt = _fox_inproj(x, lw["g_mix_pre"], lw["w_fox_q"], lw["w_fox_kv_t"], lw["w_fox_f_t"], b, tm)
    qd, kd_t, vd = _diff_inproj(x, lw["g_mix_pre"], lw["w_diff_q"], lw["w_diff_k_t"], lw["w_diff_v"], tabs, b, tm)

    if past is None:
        p_len = 0
        z = f_bt
        shift0 = jnp.zeros((b, 1, RWKV_IN), F32)
        s0 = jnp.zeros((b, RWKV_HEADS, HEAD_DIM, HEAD_DIM), F32)
    else:
        fox_kt, fox_vt, fox_lf_t, diff_kt, diff_v, s0, shift0 = past
        p_len = fox_kt.shape[3]
        pad = (-(p_len + t)) % LANES
        z = jnp.concatenate([fox_lf_t[l], f_bt, jnp.zeros((b, FOX_HEADS, pad), F32)], axis=2)
        s0 = s0[l]
        shift0 = shift0[l].reshape(b, 1, RWKV_IN)
    lf, d_row = _logf_cumsum(z, lw["fox_bias"], p_len, t)
    log_f = lf[:, :, p_len:p_len + t].transpose(0, 2, 1)

    if past is None:
        tq = _tile(t, 512)
        out_a = _fox_prompt(qf, kf_t, vf_t, d_row, b, tq)
        out_b = _diff_prompt(qd, kd_t, vd, lw["diff_lambda"], lw["diff_subln"], lam_init, b, tq)
    else:
        assert p_len % CHUNK == 0 and t <= CHUNK and p_len % LANES == 0
        tk = _tile(p_len, 1024)
        dq = d_row[:, :, p_len:p_len + t].reshape(b, GROUPS * t, 1)
        out_a = _sample_attn(_block_diag_queries(qf, b, t), fox_kt, fox_vt, kf_t, vf_t, (dq, d_row),
                             layer=l, fox=True, b=b, t_new=t, past=p_len, tk=tk)
        out_b = _sample_attn(_block_diag_queries(qd, b, t), diff_kt, diff_v, kd_t, vd,
                             (lw["diff_lambda"], lw["diff_subln"]),
                             layer=l, fox=False, b=b, t_new=t, past=p_len, tk=tk, lam_init=lam_init)

    r, lgw, km, v, kk, kka, gg, bonus, shift = _rwkv_prep(
        x, lw["g_mix_pre"], lw["w_rwkv"], shift0, lw["rwkv_vecs"], lw["rwkv_mats"], lw["rwkv_seg"], b, tm)
    ch = _tile(t, CHUNK)
    out_c, s_new = _rwkv_scan(r, lgw, km, v, kk, kka, gg, bonus, lw["rwkv_gain"], lw["rwkv_bias"], s0, b, ch)

    y = _merge(x, out_a, out_b, out_c, lw["g_mix_pre"], lw["w_gate"], lw["w_branch"], lw["w_out"],
               lw["g_mix_post"], _tile(n, 256))
    state = (kf_t, vf_t, log_f, kd_t, vd, s_new, shift.reshape(b, RWKV_IN))
    return y, state


def _layer(x, lw, l, b, t, past, mem_k, mem_v, mem_layer, tabs):
    n = x.shape[0]
    x, state = _mixer(x, lw, l, b, t, past, tabs)
    x = _xattn(x, lw["g_x_pre"], lw["w_xq"], mem_k, mem_v, lw["w_xo"], lw["g_x_post"], mem_layer, b,
               _tile(t, 512))
    x = _ffn(x, lw["g_ffn_pre"], lw["w_ffn_in"], lw["w_ffn_out"], lw["g_ffn_post"], _tile(n, 1024), 256)
    return x, state


def _assemble_states(states, b, t):
    kf_t, vf_t, log_f, kd_t, vd, s_new, shift = (jnp.stack(e) for e in zip(*states))
    depth = kf_t.shape[0]
    tok_major = lambda a: a.reshape(depth, b, FOX_HEADS, HEAD_DIM, t).transpose(0, 1, 4, 2, 3)
    fox_k, fox_v = tok_major(kf_t), tok_major(vf_t)
    diff_k = tok_major(kd_t).reshape(depth, b, t, DIFF_HEADS, 2, HEAD_DIM)
    diff_v = vd.reshape(depth, b, t, DIFF_HEADS, 2 * HEAD_DIM)
    return fox_k, fox_v, log_f, diff_k, diff_v, s_new, shift


def kernel(x_prompt, x_sample, mem_prompt, cache_fox_k, cache_fox_v, cache_fox_logf, cache_diff_k, cache_diff_v, state_rwkv, state_rwkv_shift, cache_mem_k, cache_mem_v, norm_mix_pre, norm_mix_post, w_in, fox_forget_bias, diff_lambda, diff_subln, rwkv_mu, rwkv_w0, rwkv_w2, rwkv_a0, rwkv_a2, rwkv_g2, rwkv_kk_scale, rwkv_ka, rwkv_rk, rwkv_ln_gain, rwkv_ln_bias, w_branch, w_out, norm_x_pre, norm_x_post, norm_mem, w_xq, w_xkv, w_xo, norm_ffn_pre, norm_ffn_post, w_ffn_in, w_ffn_out):
    wts = dict(norm_mix_pre=norm_mix_pre, norm_mix_post=norm_mix_post, w_in=w_in, fox_forget_bias=fox_forget_bias,
               diff_lambda=diff_lambda, diff_subln=diff_subln, rwkv_mu=rwkv_mu, rwkv_w0=rwkv_w0, rwkv_w2=rwkv_w2,
               rwkv_a0=rwkv_a0, rwkv_a2=rwkv_a2, rwkv_g2=rwkv_g2, rwkv_kk_scale=rwkv_kk_scale, rwkv_ka=rwkv_ka,
               rwkv_rk=rwkv_rk, rwkv_ln_gain=rwkv_ln_gain, rwkv_ln_bias=rwkv_ln_bias, w_branch=w_branch,
               w_out=w_out, norm_x_pre=norm_x_pre, norm_x_post=norm_x_post, norm_mem=norm_mem, w_xq=w_xq,
               w_xkv=w_xkv, w_xo=w_xo, norm_ffn_pre=norm_ffn_pre, norm_ffn_post=norm_ffn_post,
               w_ffn_in=w_ffn_in, w_ffn_out=w_ffn_out)
    depth = w_in.shape[0]
    bp, tp, d = x_prompt.shape
    bs, ts, _ = x_sample.shape
    p_len = cache_fox_k.shape[2]
    n_mem = mem_prompt.shape[1]
    xw = XATTN_HEADS * XATTN_HEAD_DIM
    w = BRANCH_W

    tabs_p = _rotary_tables(jnp.arange(tp, dtype=jnp.int32))
    tabs_s = _rotary_tables(p_len + jnp.arange(ts, dtype=jnp.int32))

    past_s = (cache_fox_k.transpose(0, 1, 3, 4, 2).reshape(depth, bs, w, p_len),
              cache_fox_v.transpose(0, 1, 3, 4, 2).reshape(depth, bs, w, p_len),
              cache_fox_logf.transpose(0, 1, 3, 2),
              cache_diff_k.transpose(0, 1, 3, 4, 5, 2).reshape(depth, bs, w, p_len),
              cache_diff_v.reshape(depth, bs, p_len * DIFF_HEADS, 2 * HEAD_DIM),
              state_rwkv, state_rwkv_shift)
    mem_k_s = cache_mem_k.reshape(depth, bs, n_mem, xw)
    mem_v_s = cache_mem_v.reshape(depth, bs, n_mem, xw)

    xp = x_prompt.reshape(bp * tp, d)
    xs = x_sample.reshape(bs * ts, d)
    mem = mem_prompt.reshape(bp * n_mem, d)
    p_new, s_new, p_mk, p_mv = [], [], [], []
    for l in range(depth):
        lw = _layer_weights(l, wts)
        mk, mv = _memkv(mem, lw["g_mem"], lw["w_xkv"], _tile(bp * n_mem, 256))
        xp, st_p = _layer(xp, lw, l, bp, tp, None, mk.reshape(1, bp, n_mem, xw), mv.reshape(1, bp, n_mem, xw), 0,
                          tabs_p)
        xs, st_s = _layer(xs, lw, l, bs, ts, past_s, mem_k_s, mem_v_s, l, tabs_s)
        p_new.append(st_p)
        s_new.append(st_s)
        p_mk.append(mk.reshape(bp, n_mem, XATTN_HEADS, XATTN_HEAD_DIM))
        p_mv.append(mv.reshape(bp, n_mem, XATTN_HEADS, XATTN_HEAD_DIM))
    p_out = _assemble_states(p_new, bp, tp)
    s_out = _assemble_states(s_new, bs, ts)
    return (xp.reshape(bp, tp, d), xs.reshape(bs, ts, d)) + p_out + (jnp.stack(p_mk), jnp.stack(p_mv)) + s_out
```

```python
import functools
import math

import numpy as np
import jax
import jax.numpy as jnp
from jax import lax
from jax.experimental import pallas as pl
from jax.experimental.pallas import tpu as pltpu

F32 = jnp.float32
BF = jnp.bfloat16

CHUNK = 64
HEAD_DIM = 64
FOX_HEADS = 8
DIFF_HEADS = 4
RWKV_HEADS = 8
GROUPS = 8
BRANCH_W = 512
DECAY_LORA = 64
AAA_LORA = 64
GATE_LORA = 128
RWKV_IN = 3 * BRANCH_W + DECAY_LORA + AAA_LORA + GATE_LORA
ROT_DIM = HEAD_DIM // 4
ROPE_THETA = 500000.0
XATTN_HEADS = 4
XATTN_HEAD_DIM = 128
RMS_EPS = 1e-6
RWKV_GN_EPS = 64e-5
ATTN_SCALE = HEAD_DIM ** -0.5
NEG_BIG = -1e30

LANES = 128
SUBLANES = 8
VMEM_LIMIT_BYTES = 56 * 1024 * 1024


def _cparams(*sem):
    return pltpu.CompilerParams(dimension_semantics=sem, vmem_limit_bytes=VMEM_LIMIT_BYTES)


def _const_spec(shape):
    nd = len(shape)
    return pl.BlockSpec(shape, lambda *_: (0,) * nd)


def _rms(x, g):
    return x * lax.rsqrt(jnp.mean(x * x, axis=-1, keepdims=True) + RMS_EPS) * g


_NN = ((1,), (0,))
_NT = ((1,), (1,))
_TN = ((0,), (0,))


def _dg(a, b, dims):
    return lax.dot_general(a, b, (dims, ((), ())), preferred_element_type=F32)


def _dot(a, b):
    return _dg(a.astype(BF), b.astype(BF), _NN)


def _dot_nt(a, b):
    return _dg(a.astype(BF), b.astype(BF), _NT)


def _split2(x):
    hi = x.astype(BF)
    lo = (x - hi.astype(F32)).astype(BF)
    return hi, lo


def _split3(x):
    hi = x.astype(BF)
    r1 = x - hi.astype(F32)
    mid = r1.astype(BF)
    lo = (r1 - mid.astype(F32)).astype(BF)
    return hi, mid, lo


def _dot_sel(sel, x):
    hi, mid, lo = _split3(x)
    return _dg(sel, hi, _NN) + (_dg(sel, mid, _NN) + _dg(sel, lo, _NN))


def _dot_sel_right(x, sel):
    hi, mid, lo = _split3(x)
    return _dg(hi, sel, _NN) + (_dg(mid, sel, _NN) + _dg(lo, sel, _NN))


def _softplus(x):
    return jnp.maximum(x, 0.0) + jnp.log(1.0 + jnp.exp(-jnp.abs(x)))


def _fox_inproj_kernel(x_ref, g_ref, wq_ref, wkv_t_ref, wf_t_ref, kbuf_ref, vbuf_ref, q_ref, kt_ref, vt_ref, f_ref):
    del kbuf_ref, vbuf_ref
    xn = _rms(x_ref[...], g_ref[...]).astype(BF)
    w = BRANCH_W
    q_ref[...] = (_dg(xn, wq_ref[...], _NN) * ATTN_SCALE).astype(BF)
    kt_ref[0, 0] = _dg(wkv_t_ref[0:w, :], xn, _NT)
    vt_ref[0, 0] = _dg(wkv_t_ref[w:2 * w, :], xn, _NT)
    f_ref[0] = _dg(wf_t_ref[...], xn, _NT)


def _row_specs(b, t, tm, layer):
    nt = t // tm
    row = lambda width: pl.BlockSpec((tm, width), lambda bi, ti: (bi * nt + ti, 0))
    col = lambda height: pl.BlockSpec((1, height, tm), lambda bi, ti: (bi, 0, ti))
    state = lambda height: pl.BlockSpec((1, 1, height, tm), lambda bi, ti: (layer, bi, 0, ti))
    return nt, row, col, state


_ANY = pl.BlockSpec(memory_space=pl.ANY)


def _fox_inproj(x, g, wq, wkv_t, wf_t, kbuf, vbuf, layer, b, tm):
    n, d = x.shape
    t = n // b
    w = BRANCH_W
    nt, row, col, state = _row_specs(b, t, tm, layer)
    return pl.pallas_call(
        _fox_inproj_kernel,
        grid=(b, nt),
        in_specs=[row(d), _const_spec((1, d)), _const_spec((d, w)), _const_spec((2 * w, d)),
                  _const_spec((FOX_HEADS, d)), _ANY, _ANY],
        out_specs=[row(w), state(w), state(w), col(FOX_HEADS)],
        out_shape=[jax.ShapeDtypeStruct((n, w), BF), jax.ShapeDtypeStruct(kbuf.shape, F32),
                   jax.ShapeDtypeStruct(vbuf.shape, F32), jax.ShapeDtypeStruct((b, FOX_HEADS, t), F32)],
        input_output_aliases={5: 1, 6: 2},
        compiler_params=_cparams("parallel", "parallel"),
    )(x, g, wq, wkv_t, wf_t, kbuf, vbuf)


def _diff_inproj_kernel(x_ref, g_ref, wq_ref, wk_t_ref, wv_ref, cos_ref, sa_ref, sb_ref, cos_t_ref, sin_t_ref,
                        kbuf_ref, vbuf_ref, q_ref, kt_ref, v_ref):
    del kbuf_ref, vbuf_ref
    kt_ref = kt_ref.at[0]
    tm = x_ref.shape[0]
    xn = _rms(x_ref[...], g_ref[...]).astype(BF)
    w = BRANCH_W
    half = ROT_DIM // 2
    cos, sa, sb = cos_ref[...], sa_ref[...], sb_ref[...]
    yq = _dg(xn, wq_ref[...], _NN)
    for j in range(w // LANES):
        yj = yq[:, j * LANES:(j + 1) * LANES]
        rj = yj * cos + pltpu.roll(yj, LANES - half, 1) * sa + pltpu.roll(yj, half, 1) * sb
        q_ref[:, j * LANES:(j + 1) * LANES] = (rj * ATTN_SCALE).astype(BF)
    yk = _dg(wk_t_ref[...], xn, _NT)
    cos_t, sin_t = cos_t_ref[...], sin_t_ref[...]
    for gi in range(GROUPS):
        r0 = gi * HEAD_DIM
        y1, y2 = yk[r0:r0 + half, :], yk[r0 + half:r0 + ROT_DIM, :]
        kt_ref[0, r0:r0 + half, :] = y1 * cos_t - y2 * sin_t
        kt_ref[0, r0 + half:r0 + ROT_DIM, :] = y2 * cos_t + y1 * sin_t
        kt_ref[0, r0 + ROT_DIM:r0 + HEAD_DIM, :] = yk[r0 + ROT_DIM:r0 + HEAD_DIM, :]
    yv = _dg(xn, wv_ref[...], _NN)
    dv = 2 * HEAD_DIM
    for h in range(DIFF_HEADS):
        v_ref[0, pl.ds(h, tm, stride=DIFF_HEADS), :] = yv[:, h * dv:(h + 1) * dv]


def _diff_inproj(x, g, wq, wk_t, wv, tabs, kbuf, vbuf, layer, b, tm):
    n, d = x.shape
    t = n // b
    w = BRANCH_W
    nt, row, col, state = _row_specs(b, t, tm, layer)
    row_tab = pl.BlockSpec((tm, LANES), lambda bi, ti: (ti, 0))
    col_tab = pl.BlockSpec((ROT_DIM // 2, tm), lambda bi, ti: (0, ti))
    vspec = pl.BlockSpec((1, DIFF_HEADS * tm, 2 * HEAD_DIM), lambda bi, ti: (layer, bi * nt + ti, 0))
    return pl.pallas_call(
        _diff_inproj_kernel,
        grid=(b, nt),
        in_specs=[row(d), _const_spec((1, d)), _const_spec((d, w)), _const_spec((w, d)), _const_spec((d, w)),
                  row_tab, row_tab, row_tab, col_tab, col_tab, _ANY, _ANY],
        out_specs=[row(w), state(w), vspec],
        out_shape=[jax.ShapeDtypeStruct((n, w), BF), jax.ShapeDtypeStruct(kbuf.shape, F32),
                   jax.ShapeDtypeStruct(vbuf.shape, F32)],
        input_output_aliases={10: 1, 11: 2},
        compiler_params=_cparams("parallel", "parallel"),
    )(x, g, wq, wk_t, wv, *tabs, kbuf, vbuf)


def _rotary_tables(pos):
    half = ROT_DIM // 2
    inv = ROPE_THETA ** (-np.arange(half, dtype=np.float32) / half)
    ang = pos.astype(F32)[:, None] * jnp.asarray(inv, F32)[None, :]
    cos, sin = jnp.cos(ang), jnp.sin(ang)
    t = pos.shape[0]
    one = jnp.ones((t, HEAD_DIM - ROT_DIM), F32)
    zero = jnp.zeros((t, HEAD_DIM - ROT_DIM), F32)
    zh = jnp.zeros((t, half), F32)
    cos_h = jnp.concatenate([cos, cos, one], axis=1)
    sa_h = jnp.concatenate([-sin, zh, zero], axis=1)
    sb_h = jnp.concatenate([zh, sin, zero], axis=1)
    dup = lambda a: jnp.concatenate([a, a], axis=1)
    return dup(cos_h), dup(sa_h), dup(sb_h), cos.T, sin.T


def _rwkv_prep_kernel(x_ref, g_ref, wc_ref, shift0_ref, mu_ref, w0_ref, w2_ref, a0_ref, a2_ref, g2_ref,
                      kks_ref, ka_ref, rk_ref, seg_ref,
                      r_ref, lw_ref, km_ref, v_ref, kk_ref, kka_ref, gg_ref, bonus_ref, shift_ref,
                      carry_ref):
    t = pl.program_id(1)
    tm = x_ref.shape[0]
    w = BRANCH_W
    xn = _rms(x_ref[...], g_ref[...]).astype(BF)
    c = _dg(xn, wc_ref[...], _NN)

    @pl.when(t == 0)
    def _():
        carry_ref[...] = shift0_ref[0]

    row = lax.broadcasted_iota(jnp.int32, (tm, 1), 0)
    prev = jnp.where(row == 0, carry_ref[...], pltpu.roll(c, 1, 0))
    last = c[tm - 1:tm, :]
    carry_ref[...] = last
    shift_ref[0] = last

    xs = c + mu_ref[...] * (prev - c)
    cr, ck, cv = xs[:, 0:w], xs[:, w:2 * w], xs[:, 2 * w:3 * w]
    o = 3 * w
    xw = xs[:, o:o + DECAY_LORA]
    xa = xs[:, o + DECAY_LORA:o + DECAY_LORA + AAA_LORA]
    xg = xs[:, o + DECAY_LORA + AAA_LORA:]
    z = w0_ref[...] + _dot(jnp.tanh(xw), w2_ref[...])
    w_raw = -_softplus(-z) - 0.5
    a = jax.nn.sigmoid(a0_ref[...] + _dot(xa, a2_ref[...]))
    seg = seg_ref[...]
    kk = ck * kks_ref[...]
    kk = kk / jnp.maximum(jnp.sqrt(_dot_sel_right(kk * kk, seg)), 1e-12)
    kmod = ck * (1.0 + (a - 1.0) * ka_ref[...])
    r_ref[...] = cr
    lw_ref[...] = -jnp.exp(w_raw)
    km_ref[...] = kmod
    v_ref[...] = cv
    kk_ref[...] = kk
    kka_ref[...] = kk * a
    gg_ref[...] = _dot(jax.nn.sigmoid(xg), g2_ref[...])
    bonus_ref[...] = _dot_sel_right(cr * kmod * rk_ref[...], seg) * cv


def _rwkv_prep(x, g, wc, shift0, vecs, mats, seg, b, tm):
    n, d = x.shape
    t = n // b
    w = BRANCH_W
    mu, w0, a0, kks, ka, rk = vecs
    w2, a2, g2 = mats
    nt, row, _, _ = _row_specs(b, t, tm, 0)
    per_b = pl.BlockSpec((1, 1, RWKV_IN), lambda bi, ti: (bi, 0, 0))
    outs = [jax.ShapeDtypeStruct((n, w), F32)] * 8 + [jax.ShapeDtypeStruct((b, 1, RWKV_IN), F32)]
    return pl.pallas_call(
        _rwkv_prep_kernel,
        grid=(b, nt),
        in_specs=[row(d), _const_spec((1, d)), _const_spec((d, RWKV_IN)), per_b,
                  _const_spec((1, RWKV_IN)), _const_spec((1, w)), _const_spec((DECAY_LORA, w)),
                  _const_spec((1, w)), _const_spec((AAA_LORA, w)), _const_spec((GATE_LORA, w)),
                  _const_spec((1, w)), _const_spec((1, w)), _const_spec((1, w)), _const_spec((w, w))],
        out_specs=[row(w)] * 8 + [per_b],
        out_shape=outs,
        scratch_shapes=[pltpu.VMEM((1, RWKV_IN), F32)],
        compiler_params=_cparams("parallel", "arbitrary"),
    )(x, g, wc, shift0, mu, w0, w2, a0, a2, g2, kks, ka, rk, seg)


def _logf_cumsum_kernel(z_ref, bias_ref, lf_ref, d_ref, *, past, new):
    z = z_ref[0]
    width = z.shape[1]
    col = lax.broadcasted_iota(jnp.int32, z.shape, 1)
    zz = z + bias_ref[...]
    log_sig = jnp.minimum(zz, 0.0) - jnp.log(1.0 + jnp.exp(-jnp.abs(zz)))
    lf = jnp.where(col < past, z, jnp.where(col < past + new, log_sig, 0.0))
    lf_ref[0] = lf
    x = lf
    s = 1
    while s < width:
        x = x + jnp.where(col >= s, pltpu.roll(x, s, 1), 0.0)
        s *= 2
    d_ref[0] = x


def _logf_cumsum(z, bias, past, new):
    b, h, width = z.shape
    blk = pl.BlockSpec((1, h, width), lambda i: (i, 0, 0))
    return pl.pallas_call(
        functools.partial(_logf_cumsum_kernel, past=past, new=new),
        grid=(b,),
        in_specs=[blk, _const_spec((h, 1))],
        out_specs=[blk, blk],
        out_shape=[jax.ShapeDtypeStruct(z.shape, F32)] * 2,
        compiler_params=_cparams("parallel"),
    )(z, bias)


def _softmax_update(s, m_ref, l_ref, idx):
    m_prev = m_ref[idx]
    m_new = jnp.maximum(m_prev, jnp.max(s, axis=-1, keepdims=True))
    alpha = jnp.exp(m_prev - m_new)
    p = jnp.exp(s - m_new)
    l_ref[idx] = alpha * l_ref[idx] + jnp.sum(p, axis=-1, keepdims=True)
    m_ref[idx] = m_new
    return p.astype(BF), alpha


def _init_softmax_state(m_ref, l_ref, acc_ref):
    m_ref[...] = jnp.full(m_ref.shape, NEG_BIG, F32)
    l_ref[...] = jnp.zeros(l_ref.shape, F32)
    acc_ref[...] = jnp.zeros(acc_ref.shape, F32)


def _row_max_update(s, m_ref, idx):
    reps = s.shape[1] // LANES
    m_prev = m_ref[idx]
    m_new = jnp.maximum(m_prev, jnp.max(s, axis=-1, keepdims=True))
    alpha = jnp.exp(m_prev - m_new)
    p = jnp.exp(s - jnp.concatenate([m_new] * reps, axis=1))
    m_ref[idx] = m_new
    return p.astype(BF), alpha


def _init_prompt_state(m_ref, acc_ref):
    m_ref[...] = jnp.full(m_ref.shape, NEG_BIG, F32)
    acc_ref[...] = jnp.zeros(acc_ref.shape, F32)


def _causal_blocks(i, j, block):
    pl.when(j < i)(functools.partial(block, False))
    pl.when(j == i)(functools.partial(block, True))


def _fox_prompt_kernel(q_ref, kt_ref, vt_ref, dk_ref, o_ref, m_ref, acc_ref):
    i, j = pl.program_id(1), pl.program_id(2)
    tq, tk = q_ref.shape[0], kt_ref.shape[3]
    heads = [slice(h * HEAD_DIM, (h + 1) * HEAD_DIM) for h in range(FOX_HEADS)]

    @pl.when(j == 0)
    def _():
        _init_prompt_state(m_ref, acc_ref)

    def block(masked):
        kb = kt_ref[0, 0].astype(BF)
        vb = vt_ref[0, 0].astype(BF)
        dk = dk_ref[0]
        ones = jnp.ones((HEAD_DIM, tk), BF)
        scores = [_dg(q_ref[:, hs], kb[hs, :], _NN) for hs in heads]
        if masked:
            visible = (lax.broadcasted_iota(jnp.int32, (tq, tk), 1) <= lax.broadcasted_iota(jnp.int32, (tq, tk), 0))
        for h, hs in enumerate(heads):
            s = scores[h] - dk[h:h + 1, :]
            if masked:
                s = jnp.where(visible, s, NEG_BIG)
            p, alpha = _row_max_update(s, m_ref, h)
            v_ones = jnp.concatenate([vb[hs, :], ones], axis=0)
            acc_ref[h] = alpha * acc_ref[h] + _dg(p, v_ones, _NT)

    _causal_blocks(i, j, block)

    @pl.when(j == pl.num_programs(2) - 1)
    def _():
        for h, hs in enumerate(heads):
            acc = acc_ref[h]
            o_ref[:, hs] = acc[:, 0:HEAD_DIM] / acc[:, HEAD_DIM:2 * HEAD_DIM]


def _fox_prompt(q, kt, vt, d_row, layer, b, tq):
    n, w = q.shape
    t = n // b
    nq = t // tq
    qspec = pl.BlockSpec((tq, w), lambda bi, i, j: (bi * nq + i, 0))
    kspec = pl.BlockSpec((1, 1, w, tq), lambda bi, i, j: (layer, bi, 0, jnp.minimum(j, i)))
    return pl.pallas_call(
        _fox_prompt_kernel,
        grid=(b, nq, nq),
        in_specs=[qspec, kspec, kspec,
                  pl.BlockSpec((1, FOX_HEADS, tq), lambda bi, i, j: (bi, 0, jnp.minimum(j, i)))],
        out_specs=qspec,
        out_shape=jax.ShapeDtypeStruct((n, w), F32),
        scratch_shapes=[pltpu.VMEM((FOX_HEADS, tq, LANES), F32), pltpu.VMEM((FOX_HEADS, tq, 2 * HEAD_DIM), F32)],
        compiler_params=_cparams("parallel", "parallel", "arbitrary"),
    )(q, kt, vt, d_row)


def _diff_lambda(lam_ref, lam_init):
    p = lam_ref[...]
    s1 = jnp.sum(p[0:1, :] * p[1:2, :], axis=-1, keepdims=True)
    s2 = jnp.sum(p[2:3, :] * p[3:4, :], axis=-1, keepdims=True)
    return jnp.exp(s1) - jnp.exp(s2) + lam_init


def _diff_combine(acc0, l0, acc1, l1, lam, subln, lam_init):
    o = acc0 / l0 - lam * (acc1 / l1)
    return _rms(o, subln) * (1.0 - lam_init)


def _diff_prompt_kernel(q_ref, kt_ref, v_ref, lam_ref, subln_ref, o_ref, m_ref, acc_ref, *, lam_init):
    i, j = pl.program_id(1), pl.program_id(2)
    tq, tk = q_ref.shape[0], kt_ref.shape[3]
    dv = 2 * HEAD_DIM

    @pl.when(j == 0)
    def _():
        _init_prompt_state(m_ref, acc_ref)

    def block(masked):
        kb = kt_ref[0, 0].astype(BF)
        ones = jnp.ones((tk, dv), BF)
        groups = [slice(gi * HEAD_DIM, (gi + 1) * HEAD_DIM) for gi in range(GROUPS)]
        scores = [_dg(q_ref[:, gs], kb[gs, :], _NN) for gs in groups]
        if masked:
            visible = ((lax.broadcasted_iota(jnp.int32, (tq, tk), 1) // CHUNK)
                       <= (lax.broadcasted_iota(jnp.int32, (tq, tk), 0) // CHUNK))
        for h in range(DIFF_HEADS):
            vh = v_ref[0, pl.ds(h, tk, stride=DIFF_HEADS), :].astype(BF)
            v_ones = jnp.concatenate([vh, ones], axis=1)
            for gi in (2 * h, 2 * h + 1):
                s = jnp.where(visible, scores[gi], NEG_BIG) if masked else scores[gi]
                p, alpha = _row_max_update(s, m_ref, gi)
                acc_ref[gi] = jnp.concatenate([alpha, alpha], axis=1) * acc_ref[gi] + _dg(p, v_ones, _NN)

    _causal_blocks(i, j, block)

    @pl.when(j == pl.num_programs(2) - 1)
    def _():
        lam = _diff_lambda(lam_ref, lam_init)
        for h in range(DIFF_HEADS):
            a0, a1 = acc_ref[2 * h], acc_ref[2 * h + 1]
            o_ref[:, h * dv:(h + 1) * dv] = _diff_combine(a0[:, 0:dv], a0[:, dv:], a1[:, 0:dv], a1[:, dv:], lam,
                                                          subln_ref[...], lam_init)


def _diff_prompt(q, kt, v, lam_p, subln, lam_init, layer, b, tq):
    n, w = q.shape
    t = n // b
    nq = t // tq
    qspec = pl.BlockSpec((tq, w), lambda bi, i, j: (bi * nq + i, 0))
    ktspec = pl.BlockSpec((1, 1, w, tq), lambda bi, i, j: (layer, bi, 0, jnp.minimum(j, i)))
    vspec = pl.BlockSpec((1, DIFF_HEADS * tq, 2 * HEAD_DIM),
                         lambda bi, i, j: (layer, bi * nq + jnp.minimum(j, i), 0))
    return pl.pallas_call(
        functools.partial(_diff_prompt_kernel, lam_init=lam_init),
        grid=(b, nq, nq),
        in_specs=[qspec, ktspec, vspec, _const_spec((4, HEAD_DIM)), _const_spec((1, 2 * HEAD_DIM))],
        out_specs=qspec,
        out_shape=jax.ShapeDtypeStruct((n, w), F32),
        scratch_shapes=[pltpu.VMEM((GROUPS, tq, LANES), F32), pltpu.VMEM((GROUPS, tq, 4 * HEAD_DIM), F32)],
        compiler_params=_cparams("parallel", "parallel", "arbitrary"),
    )(q, kt, v, lam_p, subln)


def _sample_attn_kernel(*refs, fox, t_new, lam_init):
    if fox:
        q_ref, kc_ref, vc_ref, kn_ref, vn_ref, dq_ref, dkc_ref, dkn_ref, o_ref, m_ref, l_ref, acc_ref = refs
    else:
        q_ref, kc_ref, vc_ref, kn_ref, vn_ref, lam_ref, subln_ref, o_ref, m_ref, l_ref, acc_ref = refs
    j = pl.program_id(1)
    dv = 2 * HEAD_DIM
    pair = 2 * t_new

    @pl.when(j == 0)
    def _():
        _init_softmax_state(m_ref, l_ref, acc_ref)

    def expand(dk):
        return jnp.concatenate([jnp.broadcast_to(dk[gi:gi + 1, :], (t_new, dk.shape[1])) for gi in range(GROUPS)],
                               axis=0)

    def scores(kt, dk, causal):
        s = _dg(q_ref[0], kt.astype(BF), _NN)
        if fox:
            s = s + dq_ref[0] - expand(dk)
        if causal:
            qi = lax.broadcasted_iota(jnp.int32, s.shape, 0) % t_new
            kj = lax.broadcasted_iota(jnp.int32, s.shape, 1)
            s = jnp.where(kj <= qi, s, NEG_BIG)
        return _softmax_update(s, m_ref, l_ref, 0)

    def accumulate_diff(p, alpha, value_of_head):
        for h in range(DIFF_HEADS):
            rs = slice(h * pair, (h + 1) * pair)
            acc_ref[0, rs, :] = alpha[rs] * acc_ref[0, rs, :] + _dg(p[rs], value_of_head(h).astype(BF), _NN)

    tk = kc_ref.shape[3]
    if fox:
        p, alpha = scores(kc_ref[0, 0], dkc_ref[0], False)
        acc_ref[0] = alpha * acc_ref[0] + _dg(p, vc_ref[0, 0].astype(BF), _NT)
    else:
        p, alpha = scores(kc_ref[0, 0], None, False)
        accumulate_diff(p, alpha, lambda h: vc_ref[0, 0, pl.ds(h, tk, stride=DIFF_HEADS), :])

    @pl.when(j == pl.num_programs(1) - 1)
    def _():
        if fox:
            p, alpha = scores(kn_ref[0, 0], dkn_ref[0][:, 0:t_new], True)
            acc = alpha * acc_ref[0] + _dg(p, vn_ref[0, 0].astype(BF), _NT)
            l = l_ref[0]
            for gi in range(GROUPS):
                rs = slice(gi * t_new, (gi + 1) * t_new)
                cs = slice(gi * HEAD_DIM, (gi + 1) * HEAD_DIM)
                o_ref[:, cs] = acc[rs, cs] / l[rs]
        else:
            p, alpha = scores(kn_ref[0, 0], None, False)
            accumulate_diff(p, alpha, lambda h: vn_ref[0, pl.ds(h, t_new, stride=DIFF_HEADS), :])
            acc = acc_ref[0]
            l = l_ref[0]
            lam = _diff_lambda(lam_ref, lam_init)
            for h in range(DIFF_HEADS):
                r0 = slice(h * pair, h * pair + t_new)
                r1 = slice(h * pair + t_new, (h + 1) * pair)
                o_ref[:, h * dv:(h + 1) * dv] = _diff_combine(acc[r0], l[r0], acc[r1], l[r1], lam, subln_ref[...],
                                                              lam_init)


def _sample_attn(qbd, kc, vc, kn, vn, extra, *, layer, fox, b, t_new, past, tk, lam_init=0.0):
    w = BRANCH_W
    rows = GROUPS * t_new
    nk = past // tk
    qspec = pl.BlockSpec((1, rows, w), lambda bi, j: (bi, 0, 0))
    kcspec = pl.BlockSpec((1, 1, w, tk), lambda bi, j: (layer, bi, 0, j))
    knspec = pl.BlockSpec((1, 1, w, t_new), lambda bi, j: (layer, bi, 0, 0))
    ospec = pl.BlockSpec((t_new, w), lambda bi, j: (bi, 0))
    if fox:
        dq, d_row = extra
        vcspec, vnspec, acc_w = kcspec, knspec, w
        especs = [pl.BlockSpec((1, rows, 1), lambda bi, j: (bi, 0, 0)),
                  pl.BlockSpec((1, GROUPS, tk), lambda bi, j: (bi, 0, j)),
                  pl.BlockSpec((1, GROUPS, LANES), lambda bi, j: (bi, 0, past // LANES))]
        eargs = [dq, d_row, d_row]
    else:
        vcspec = pl.BlockSpec((1, 1, DIFF_HEADS * tk, 2 * HEAD_DIM), lambda bi, j: (layer, bi, j, 0))
        vnspec = pl.BlockSpec((1, DIFF_HEADS * t_new, 2 * HEAD_DIM), lambda bi, j: (layer, bi, 0))
        acc_w = 2 * HEAD_DIM
        especs = [_const_spec((4, HEAD_DIM)), _const_spec((1, 2 * HEAD_DIM))]
        eargs = list(extra)
    return pl.pallas_call(
        functools.partial(_sample_attn_kernel, fox=fox, t_new=t_new, lam_init=lam_init),
        grid=(b, nk),
        in_specs=[qspec, kcspec, vcspec, knspec, vnspec] + especs,
        out_specs=ospec,
        out_shape=jax.ShapeDtypeStruct((b * t_new, w), F32),
        scratch_shapes=[pltpu.VMEM((1, rows, 1), F32), pltpu.VMEM((1, rows, 1), F32),
                        pltpu.VMEM((1, rows, acc_w), F32)],
        compiler_params=_cparams("parallel", "arbitrary"),
    )(qbd, kc, vc, kn, vn, *eargs)


def _block_diag_queries(q, b, t_new):
    q4 = q.reshape(b, t_new, GROUPS, HEAD_DIM)
    eye = jnp.eye(GROUPS, dtype=q.dtype)
    qbd = q4.transpose(0, 2, 1, 3)[:, :, :, None, :] * eye[None, :, None, :, None]
    return qbd.reshape(b, GROUPS * t_new, GROUPS * HEAD_DIM)


def _rwkv_scan_kernel(r_ref, lw_ref, k_ref, v_ref, kk_ref, kka_ref, gg_ref, bonus_ref, gain_ref, bias_ref,
                      s0_ref, o_ref, sfin_ref, s_ref):
    c = pl.program_id(1)
    ch = r_ref.shape[0]
    n = HEAD_DIM

    @pl.when(c == 0)
    def _():
        s_ref[...] = s0_ref[0]

    ri = lax.broadcasted_iota(jnp.int32, (ch, ch), 0)
    ci = lax.broadcasted_iota(jnp.int32, (ch, ch), 1)
    tri = (ci <= ri).astype(BF)
    ri2 = lax.broadcasted_iota(jnp.int32, (ch, 2 * ch), 0)
    ci2 = lax.broadcasted_iota(jnp.int32, (ch, 2 * ch), 1)
    ci2m = jnp.where(ci2 >= ch, ci2 - ch, ci2)
    strict_right = (ci2 >= ch) & (ci2m < ri2)
    strict_left = ci < ri
    is_x = lax.broadcasted_iota(jnp.int32, (ch, ch + n), 1) >= ch
    incl_both = ci2m <= ri2
    steps = max(1, int(math.ceil(math.log2(ch))))
    heads = [slice(h * n, (h + 1) * n) for h in range(RWKV_HEADS)]

    lw = lw_ref[...]
    cum = _dot_sel(tri, lw)
    cum_end = cum[ch - 1:ch, :]
    kk, kka, kmod, v_all = kk_ref[...], kka_ref[...], k_ref[...], v_ref[...]
    e_neg = jnp.exp(-cum)
    e_tail = jnp.exp(cum_end - cum)
    ar = _split2(jnp.concatenate([-kk * jnp.exp(cum - lw), r_ref[...] * jnp.exp(cum)], axis=0))
    bk = _split2(jnp.concatenate([kka * e_neg, kmod * e_neg], axis=0))
    bk_tail = _split2(jnp.concatenate([kka * e_tail, kmod * e_tail], axis=0))
    vv = _split2(jnp.concatenate([v_all, v_all], axis=0))
    decay_end = jnp.exp(cum_end)
    s_old = [s_ref[h] for h in range(RWKV_HEADS)]

    def dot3s(a2, b2, dims):
        (ah, al), (bh, bl) = a2, b2
        m = ah.shape[0]
        r = _dg(jnp.concatenate([ah, al], axis=0), bh, dims)
        return r[0:m] + r[m:] + _dg(ah, bl, dims)

    def dot3_tn(a2, b2):
        (ah, al), (bh, bl) = a2, b2
        return _dg(ah, bh, _TN) + (_dg(ah, bl, _TN) + _dg(al, bh, _TN))

    def nil_times(zh):
        a2, b2 = _split2(zh[:, 0:ch]), _split2(zh)
        wz = zh.shape[1]
        if wz % LANES:
            return dot3s(a2, b2, _NN)
        r = _dg(jnp.concatenate(a2, axis=0), jnp.concatenate(b2, axis=1), _NN)
        return (r[0:ch, 0:wz] + r[ch:, 0:wz]) + (r[0:ch, wz:] + r[ch:, wz:])

    cols = lambda a2, hs: (a2[0][:, hs], a2[1][:, hs])
    ga = []
    for h, hs in enumerate(heads):
        sh, sl = _split2(s_old[h])
        rhs = (jnp.concatenate([bk[0][:, hs], sh], axis=0), jnp.concatenate([bk[1][:, hs], sl], axis=0))
        ga.append(dot3s(cols(ar, hs), rhs, _NT))
    z = []
    for h, hs in enumerate(heads):
        g_top = ga[h][0:ch, 0:2 * ch]
        x0 = ga[h][0:ch, 2 * ch:] + dot3s(_split2(jnp.where(strict_right, g_top, 0.0)), cols(vv, hs), _NN)
        z.append(jnp.concatenate([jnp.where(strict_left, g_top[:, 0:ch], 0.0), x0], axis=1))
    for _ in range(steps):
        z = [nil_times(zh) + jnp.where(is_x, zh, 0.0) for zh in z]
    for h, hs in enumerate(heads):
        x = z[h][:, ch:]
        xv = _split2(jnp.concatenate([x, v_all[:, hs]], axis=0))
        y = ga[h][ch:, 2 * ch:] + dot3s(_split2(jnp.where(incl_both, ga[h][ch:, 0:2 * ch], 0.0)), xv, _NN)
        s_ref[h] = s_old[h] * decay_end[:, hs] + dot3_tn(xv, cols(bk_tail, hs))

        mu = jnp.mean(y, axis=-1, keepdims=True)
        yc = y - mu
        var = jnp.mean(yc * yc, axis=-1, keepdims=True)
        yn = yc * lax.rsqrt(var + RWKV_GN_EPS) * gain_ref[:, hs] + bias_ref[:, hs]
        o_ref[:, hs] = (yn + bonus_ref[:, hs]) * gg_ref[:, hs]

    @pl.when(c == pl.num_programs(1) - 1)
    def _():
        sfin_ref[0] = s_ref[...]


def _rwkv_scan(r, lw, km, v, kk, kka, gg, bonus, gain, bias, s0, b, ch):
    n, w = r.shape
    t = n // b
    nc = t // ch
    row = pl.BlockSpec((ch, w), lambda bi, ci: (bi * nc + ci, 0))
    st = pl.BlockSpec((1, RWKV_HEADS, HEAD_DIM, HEAD_DIM), lambda bi, ci: (bi, 0, 0, 0))
    return pl.pallas_call(
        _rwkv_scan_kernel,
        grid=(b, nc),
        in_specs=[row] * 8 + [_const_spec((1, w)), _const_spec((1, w)), st],
        out_specs=[row, st],
        out_shape=[jax.ShapeDtypeStruct((n, w), F32), jax.ShapeDtypeStruct(s0.shape, F32)],
        scratch_shapes=[pltpu.VMEM((RWKV_HEADS, HEAD_DIM, HEAD_DIM), F32)],
        compiler_params=_cparams("parallel", "arbitrary"),
    )(r, lw, km, v, kk, kka, gg, bonus, gain, bias, s0)


def _merge_kernel(x_ref, oa_ref, ob_ref, oc_ref, gpre_ref, wg_ref, wb_ref, wo_ref, gpost_ref, y_ref):
    x = x_ref[...]
    d = x.shape[1]
    xn = _rms(x, gpre_ref[...]).astype(BF)
    acc = jnp.zeros(x.shape, F32)
    for i, o_ref in enumerate((oa_ref, ob_ref, oc_ref)):
        gate = jax.nn.sigmoid(_dg(xn, wg_ref[:, i * d:(i + 1) * d], _NN))
        acc = acc + gate * _dot(o_ref[...], wb_ref[i])
    y = _dot(acc, wo_ref[...])
    y_ref[...] = x + _rms(y, gpost_ref[...])


def _merge(x, oa, ob, oc, gpre, wg, wb, wo, gpost, tm):
    n, d = x.shape
    w = BRANCH_W
    row = lambda width: pl.BlockSpec((tm, width), lambda i: (i, 0))
    return pl.pallas_call(
        _merge_kernel,
        grid=(n // tm,),
        in_specs=[row(d), row(w), row(w), row(w), _const_spec((1, d)), _const_spec((d, 3 * d)),
                  _const_spec((3, w, d)), _const_spec((d, d)), _const_spec((1, d))],
        out_specs=row(d),
        out_shape=jax.ShapeDtypeStruct((n, d), F32),
        compiler_params=_cparams("parallel"),
    )(x, oa, ob, oc, gpre, wg, wb, wo, gpost)


def _memkv_kernel(m_ref, g_ref, w_ref, k_ref, v_ref):
    xn = _rms(m_ref[...], g_ref[...]).astype(BF)
    w = k_ref.shape[1]
    k_ref[...] = _dg(xn, w_ref[:, 0:w], _NN)
    v_ref[...] = _dg(xn, w_ref[:, w:2 * w], _NN)


def _memkv(mem, g, w_xkv, tm):
    n, d = mem.shape
    w = w_xkv.shape[1] // 2
    row = lambda width: pl.BlockSpec((tm, width), lambda i: (i, 0))
    return pl.pallas_call(
        _memkv_kernel,
        grid=(n // tm,),
        in_specs=[row(d), _const_spec((1, d)), _const_spec((d, 2 * w))],
        out_specs=[row(w), row(w)],
        out_shape=[jax.ShapeDtypeStruct((n, w), F32)] * 2,
        compiler_params=_cparams("parallel"),
    )(mem, g, w_xkv)


def _xattn_kernel(x_ref, gpre_ref, wq_ref, mk_ref, mv_ref, wo_ref, gpost_ref, y_ref):
    x = x_ref[...]
    xn = _rms(x, gpre_ref[...])
    q = _dot(xn, wq_ref[...])
    hd = XATTN_HEAD_DIM
    outs = []
    for h in range(XATTN_HEADS):
        hs = slice(h * hd, (h + 1) * hd)
        s = _dot_nt(q[:, hs], mk_ref[0, 0, :, hs]) * (hd ** -0.5)
        p = jnp.exp(s - jnp.max(s, axis=-1, keepdims=True))
        p = p / jnp.sum(p, axis=-1, keepdims=True)
        outs.append(_dot(p, mv_ref[0, 0, :, hs]))
    o = jnp.concatenate(outs, axis=-1)
    y_ref[...] = x + _rms(_dot(o, wo_ref[...]), gpost_ref[...])


def _xattn(x, gpre, wq, mk, mv, wo, gpost, layer, b, tm):
    n, d = x.shape
    t = n // b
    nt = t // tm
    n_mem, w = mk.shape[2], mk.shape[3]
    row = pl.BlockSpec((tm, d), lambda bi, ti: (bi * nt + ti, 0))
    mem = pl.BlockSpec((1, 1, n_mem, w), lambda bi, ti: (layer, bi, 0, 0))
    return pl.pallas_call(
        _xattn_kernel,
        grid=(b, nt),
        in_specs=[row, _const_spec((1, d)), _const_spec((d, w)), mem, mem, _const_spec((w, d)),
                  _const_spec((1, d))],
        out_specs=row,
        out_shape=jax.ShapeDtypeStruct((n, d), F32),
        compiler_params=_cparams("parallel", "parallel"),
    )(x, gpre, wq, mk, mv, wo, gpost)


def _ffn_kernel(x_ref, gpre_ref, wu_ref, wv_ref, wo_ref, gpost_ref, y_ref, xn_ref, acc_ref):
    j = pl.program_id(1)

    @pl.when(j == 0)
    def _():
        xn_ref[...] = _rms(x_ref[...], gpre_ref[...]).astype(BF)
        acc_ref[...] = jnp.zeros(acc_ref.shape, F32)

    xn = xn_ref[...]
    u = _dg(xn, wu_ref[...], _NN)
    v = _dg(xn, wv_ref[...], _NN)
    acc_ref[...] += _dot(u * jax.nn.sigmoid(u) * v, wo_ref[...])

    @pl.when(j == pl.num_programs(1) - 1)
    def _():
        y_ref[...] = x_ref[...] + _rms(acc_ref[...], gpost_ref[...])


def _ffn(x, gpre, w_in, w_out, gpost, tm, th):
    n, d = x.shape
    hidden = w_out.shape[0]
    nh = hidden // th
    row = pl.BlockSpec((tm, d), lambda i, j: (i, 0))
    return pl.pallas_call(
        _ffn_kernel,
        grid=(n // tm, nh),
        in_specs=[row, _const_spec((1, d)),
                  pl.BlockSpec((d, th), lambda i, j: (0, j)),
                  pl.BlockSpec((d, th), lambda i, j: (0, nh + j)),
                  pl.BlockSpec((th, d), lambda i, j: (j, 0)),
                  _const_spec((1, d))],
        out_specs=row,
        out_shape=jax.ShapeDtypeStruct((n, d), F32),
        scratch_shapes=[pltpu.VMEM((tm, d), BF), pltpu.VMEM((tm, d), F32)],
        compiler_params=_cparams("parallel", "arbitrary"),
    )(x, gpre, w_in, w_in, w_out, gpost)


def _tile(n, pref):
    t = min(n, pref)
    assert n % t == 0, (n, t)
    return t


def _layer_weights(l, wts):
    p = {k: v[l] for k, v in wts.items()}
    w = BRANCH_W
    w_in = p["w_in"].astype(BF)
    w_in_t = p["w_in"].T.astype(BF)
    gate_w = w_in.shape[1] - (6 * w + FOX_HEADS + RWKV_IN)
    o_fox, o_f, o_diff = 0, 3 * w, 3 * w + FOX_HEADS
    o_rwkv = o_diff + 3 * w
    o_gate = o_rwkv + RWKV_IN
    row = lambda a: a.reshape(1, -1)
    seg = np.kron(np.eye(RWKV_HEADS, dtype=np.float32), np.ones((HEAD_DIM, HEAD_DIM), np.float32))
    return dict(
        g_mix_pre=row(p["norm_mix_pre"]), g_mix_post=row(p["norm_mix_post"]),
        w_fox_q=w_in[:, o_fox:o_fox + w], w_fox_kv_t=w_in_t[o_fox + w:o_fox + 3 * w],
        w_fox_f_t=w_in_t[o_f:o_f + FOX_HEADS],
        w_diff_q=w_in[:, o_diff:o_diff + w], w_diff_k_t=w_in_t[o_diff + w:o_diff + 2 * w],
        w_diff_v=w_in[:, o_diff + 2 * w:o_diff + 3 * w],
        w_rwkv=w_in[:, o_rwkv:o_rwkv + RWKV_IN], w_gate=w_in[:, o_gate:o_gate + gate_w],
        fox_bias=p["fox_forget_bias"].reshape(FOX_HEADS, 1),
        diff_lambda=p["diff_lambda"], diff_subln=row(p["diff_subln"]),
        rwkv_vecs=tuple(row(p[k]) for k in ("rwkv_mu", "rwkv_w0", "rwkv_a0", "rwkv_kk_scale", "rwkv_ka",
                                            "rwkv_rk")),
        rwkv_mats=tuple(p[k].astype(BF) for k in ("rwkv_w2", "rwkv_a2", "rwkv_g2")),
        rwkv_seg=jnp.asarray(seg, BF),
        rwkv_gain=row(p["rwkv_ln_gain"]), rwkv_bias=row(p["rwkv_ln_bias"]),
        w_branch=p["w_branch"].astype(BF), w_out=p["w_out"].astype(BF),
        g_x_pre=row(p["norm_x_pre"]), g_x_post=row(p["norm_x_post"]), g_mem=row(p["norm_mem"]),
        w_xq=p["w_xq"].astype(BF), w_xkv=p["w_xkv"].astype(BF), w_xo=p["w_xo"].astype(BF),
        g_ffn_pre=row(p["norm_ffn_pre"]), g_ffn_post=row(p["norm_ffn_post"]),
        w_ffn_in=p["w_ffn_in"].astype(BF), w_ffn_out=p["w_ffn_out"].astype(BF),
    )


def _new_state_buffers(depth, b, t):
    kt = lambda: jnp.zeros((depth, b, BRANCH_W, t), F32)
    return dict(kf=kt(), vf=kt(), kd=kt(), vd=jnp.zeros((depth, b * t * DIFF_HEADS, 2 * HEAD_DIM), F32))


def _mixer(x, lw, l, b, t, past, tabs, bufs):
    n = x.shape[0]
    lam_init = 0.8 - 0.6 * math.exp(-0.3 * l)
    tm = _tile(t, 512)
    qf, kf_t, vf_t, f_bt = _fox_inproj(x, lw["g_mix_pre"], lw["w_fox_q"], lw["w_fox_kv_t"], lw["w_fox_f_t"],
                                       bufs["kf"], bufs["vf"], l, b, tm)
    qd, kd_t, vd = _diff_inproj(x, lw["g_mix_pre"], lw["w_diff_q"], lw["w_diff_k_t"], lw["w_diff_v"], tabs,
                                bufs["kd"], bufs["vd"], l, b, tm)
    bufs = dict(kf=kf_t, vf=vf_t, kd=kd_t, vd=vd)

    if past is None:
        p_len = 0
        z = f_bt
        shift0 = jnp.zeros((b, 1, RWKV_IN), F32)
        s0 = jnp.zeros((b, RWKV_HEADS, HEAD_DIM, HEAD_DIM), F32)
    else:
        fox_kt, fox_vt, fox_lf_t, diff_kt, diff_v, s0, shift0 = past
        p_len = fox_kt.shape[3]
        pad = (-(p_len + t)) % LANES
        z = jnp.concatenate([fox_lf_t[l], f_bt, jnp.zeros((b, FOX_HEADS, pad), F32)], axis=2)
        s0 = s0[l]
        shift0 = shift0[l].reshape(b, 1, RWKV_IN)
    lf, d_row = _logf_cumsum(z, lw["fox_bias"], p_len, t)
    log_f = lf[:, :, p_len:p_len + t].transpose(0, 2, 1)

    if past is None:
        tq = _tile(t, 512)
        out_a = _fox_prompt(qf, kf_t, vf_t, d_row, l, b, tq)
        out_b = _diff_prompt(qd, kd_t, vd, lw["diff_lambda"], lw["diff_subln"], lam_init, l, b, tq)
    else:
        assert p_len % CHUNK == 0 and t <= CHUNK and p_len % LANES == 0
        tk = _tile(p_len, 1024)
        dq = d_row[:, :, p_len:p_len + t].reshape(b, GROUPS * t, 1)
        out_a = _sample_attn(_block_diag_queries(qf, b, t), fox_kt, fox_vt, kf_t, vf_t, (dq, d_row),
                             layer=l, fox=True, b=b, t_new=t, past=p_len, tk=tk)
        out_b = _sample_attn(_block_diag_queries(qd, b, t), diff_kt, diff_v, kd_t, vd,
                             (lw["diff_lambda"], lw["diff_subln"]),
                             layer=l, fox=False, b=b, t_new=t, past=p_len, tk=tk, lam_init=lam_init)

    r, lgw, km, v, kk, kka, gg, bonus, shift = _rwkv_prep(
        x, lw["g_mix_pre"], lw["w_rwkv"], shift0, lw["rwkv_vecs"], lw["rwkv_mats"], lw["rwkv_seg"], b, tm)
    ch = _tile(t, CHUNK)
    out_c, s_new = _rwkv_scan(r, lgw, km, v, kk, kka, gg, bonus, lw["rwkv_gain"], lw["rwkv_bias"], s0, b, ch)

    y = _merge(x, out_a, out_b, out_c, lw["g_mix_pre"], lw["w_gate"], lw["w_branch"], lw["w_out"],
               lw["g_mix_post"], _tile(n, 256))
    state = (log_f, s_new, shift.reshape(b, RWKV_IN))
    return y, state, bufs


def _layer(x, lw, l, b, t, past, mem_k, mem_v, mem_layer, tabs, bufs):
    n = x.shape[0]
    x, state, bufs = _mixer(x, lw, l, b, t, past, tabs, bufs)
    x = _xattn(x, lw["g_x_pre"], lw["w_xq"], mem_k, mem_v, lw["w_xo"], lw["g_x_post"], mem_layer, b,
               _tile(t, 512))
    x = _ffn(x, lw["g_ffn_pre"], lw["w_ffn_in"], lw["w_ffn_out"], lw["g_ffn_post"], _tile(n, 1024), 256)
    return x, state, bufs


def _assemble_states(states, bufs, b, t):
    log_f, s_new, shift = (jnp.stack(e) for e in zip(*states))
    depth = bufs["kf"].shape[0]
    tok_major = lambda a: a.reshape(depth, b, FOX_HEADS, HEAD_DIM, t).transpose(0, 1, 4, 2, 3)
    fox_k, fox_v = tok_major(bufs["kf"]), tok_major(bufs["vf"])
    diff_k = tok_major(bufs["kd"]).reshape(depth, b, t, DIFF_HEADS, 2, HEAD_DIM)
    diff_v = bufs["vd"].reshape(depth, b, t, DIFF_HEADS, 2 * HEAD_DIM)
    return fox_k, fox_v, log_f, diff_k, diff_v, s_new, shift


def kernel(x_prompt, x_sample, mem_prompt, cache_fox_k, cache_fox_v, cache_fox_logf, cache_diff_k, cache_diff_v, state_rwkv, state_rwkv_shift, cache_mem_k, cache_mem_v, norm_mix_pre, norm_mix_post, w_in, fox_forget_bias, diff_lambda, diff_subln, rwkv_mu, rwkv_w0, rwkv_w2, rwkv_a0, rwkv_a2, rwkv_g2, rwkv_kk_scale, rwkv_ka, rwkv_rk, rwkv_ln_gain, rwkv_ln_bias, w_branch, w_out, norm_x_pre, norm_x_post, norm_mem, w_xq, w_xkv, w_xo, norm_ffn_pre, norm_ffn_post, w_ffn_in, w_ffn_out):
    wts = dict(norm_mix_pre=norm_mix_pre, norm_mix_post=norm_mix_post, w_in=w_in, fox_forget_bias=fox_forget_bias,
               diff_lambda=diff_lambda, diff_subln=diff_subln, rwkv_mu=rwkv_mu, rwkv_w0=rwkv_w0, rwkv_w2=rwkv_w2,
               rwkv_a0=rwkv_a0, rwkv_a2=rwkv_a2, rwkv_g2=rwkv_g2, rwkv_kk_scale=rwkv_kk_scale, rwkv_ka=rwkv_ka,
               rwkv_rk=rwkv_rk, rwkv_ln_gain=rwkv_ln_gain, rwkv_ln_bias=rwkv_ln_bias, w_branch=w_branch,
               w_out=w_out, norm_x_pre=norm_x_pre, norm_x_post=norm_x_post, norm_mem=norm_mem, w_xq=w_xq,
               w_xkv=w_xkv, w_xo=w_xo, norm_ffn_pre=norm_ffn_pre, norm_ffn_post=norm_ffn_post,
               w_ffn_in=w_ffn_in, w_ffn_out=w_ffn_out)
    depth = w_in.shape[0]
    bp, tp, d = x_prompt.shape
    bs, ts, _ = x_sample.shape
    p_len = cache_fox_k.shape[2]
    n_mem = mem_prompt.shape[1]
    xw = XATTN_HEADS * XATTN_HEAD_DIM
    w = BRANCH_W

    tabs_p = _rotary_tables(jnp.arange(tp, dtype=jnp.int32))
    tabs_s = _rotary_tables(p_len + jnp.arange(ts, dtype=jnp.int32))

    past_s = (cache_fox_k.transpose(0, 1, 3, 4, 2).reshape(depth, bs, w, p_len),
              cache_fox_v.transpose(0, 1, 3, 4, 2).reshape(depth, bs, w, p_len),
              cache_fox_logf.transpose(0, 1, 3, 2),
              cache_diff_k.transpose(0, 1, 3, 4, 5, 2).reshape(depth, bs, w, p_len),
              cache_diff_v.reshape(depth, bs, p_len * DIFF_HEADS, 2 * HEAD_DIM),
              state_rwkv, state_rwkv_shift)
    mem_k_s = cache_mem_k.reshape(depth, bs, n_mem, xw)
    mem_v_s = cache_mem_v.reshape(depth, bs, n_mem, xw)

    xp = x_prompt.reshape(bp * tp, d)
    xs = x_sample.reshape(bs * ts, d)
    mem = mem_prompt.reshape(bp * n_mem, d)
    p_new, s_new, p_mk, p_mv = [], [], [], []
    bufs_p = _new_state_buffers(depth, bp, tp)
    bufs_s = _new_state_buffers(depth, bs, ts)
    for l in range(depth):
        lw = _layer_weights(l, wts)
        mk, mv = _memkv(mem, lw["g_mem"], lw["w_xkv"], _tile(bp * n_mem, 256))
        xp, st_p, bufs_p = _layer(xp, lw, l, bp, tp, None, mk.reshape(1, bp, n_mem, xw),
                                  mv.reshape(1, bp, n_mem, xw), 0, tabs_p, bufs_p)
        xs, st_s, bufs_s = _layer(xs, lw, l, bs, ts, past_s, mem_k_s, mem_v_s, l, tabs_s, bufs_s)
        p_new.append(st_p)
        s_new.append(st_s)
        p_mk.append(mk.reshape(bp, n_mem, XATTN_HEADS, XATTN_HEAD_DIM))
        p_mv.append(mv.reshape(bp, n_mem, XATTN_HEADS, XATTN_HEAD_DIM))
    p_out = _assemble_states(p_new, bufs_p, bp, tp)
    s_out = _assemble_states(s_new, bufs_s, bs, ts)
    return (xp.reshape(bp, tp, d), xs.reshape(bs, ts, d)) + p_out + (jnp.stack(p_mk), jnp.stack(p_mv)) + s_out
```

```python
import functools
import math

import numpy as np
import jax
import jax.numpy as jnp
from jax import lax
from jax.experimental import pallas as pl
from jax.experimental.pallas import tpu as pltpu

F32 = jnp.float32
BF = jnp.bfloat16

CHUNK = 64
HEAD_DIM = 64
FOX_HEADS = 8
DIFF_HEADS = 4
RWKV_HEADS = 8
GROUPS = 8
BRANCH_W = 512
DECAY_LORA = 64
AAA_LORA = 64
GATE_LORA = 128
RWKV_IN = 3 * BRANCH_W + DECAY_LORA + AAA_LORA + GATE_LORA
ROT_DIM = HEAD_DIM // 4
ROPE_THETA = 500000.0
XATTN_HEADS = 4
XATTN_HEAD_DIM = 128
RMS_EPS = 1e-6
RWKV_GN_EPS = 64e-5
ATTN_SCALE = HEAD_DIM ** -0.5
LOG2E = math.log2(math.e)
QK_SCALE = ATTN_SCALE * LOG2E
D_PIECES = 3
NEG_BIG = -1e30

LANES = 128
SUBLANES = 8
VMEM_LIMIT_BYTES = 56 * 1024 * 1024


def _cparams(*sem):
    return pltpu.CompilerParams(dimension_semantics=sem, vmem_limit_bytes=VMEM_LIMIT_BYTES)


def _const_spec(shape):
    nd = len(shape)
    return pl.BlockSpec(shape, lambda *_: (0,) * nd)


def _rms(x, g):
    return x * lax.rsqrt(jnp.mean(x * x, axis=-1, keepdims=True) + RMS_EPS) * g


_NN = ((1,), (0,))
_NT = ((1,), (1,))
_TN = ((0,), (0,))


def _dg(a, b, dims):
    return lax.dot_general(a, b, (dims, ((), ())), preferred_element_type=F32)


def _dot(a, b):
    return _dg(a.astype(BF), b.astype(BF), _NN)


def _dot_nt(a, b):
    return _dg(a.astype(BF), b.astype(BF), _NT)


def _split2(x):
    hi = x.astype(BF)
    lo = (x - hi.astype(F32)).astype(BF)
    return hi, lo


def _split3(x):
    hi = x.astype(BF)
    r1 = x - hi.astype(F32)
    mid = r1.astype(BF)
    lo = (r1 - mid.astype(F32)).astype(BF)
    return hi, mid, lo


def _dot_sel(sel, x):
    hi, mid, lo = _split3(x)
    return _dg(sel, hi, _NN) + (_dg(sel, mid, _NN) + _dg(sel, lo, _NN))


def _dot_sel_right(x, sel):
    hi, lo = _split2(x)
    return _dg(hi, sel, _NN) + _dg(lo, sel, _NN)


def _softplus(x):
    return jnp.maximum(x, 0.0) + jnp.log(1.0 + jnp.exp(-jnp.abs(x)))


def _fox_inproj_kernel(x_ref, g_ref, wq_ref, qc_ref, wkv_t_ref, wf_t_ref, kbuf_ref, vbuf_ref,
                       q_ref, kt_ref, vt_ref, f_ref):
    del kbuf_ref, vbuf_ref
    xn = _rms(x_ref[...], g_ref[...]).astype(BF)
    w = BRANCH_W
    q_ref[...] = (_dg(xn, wq_ref[...], _NN) * QK_SCALE + qc_ref[...]).astype(BF)
    kt_ref[0, 0] = _dg(wkv_t_ref[0:w, :], xn, _NT)
    vt_ref[0, 0] = _dg(wkv_t_ref[w:2 * w, :], xn, _NT)
    f_ref[0] = _dg(wf_t_ref[...], xn, _NT)


def _row_specs(b, t, tm, layer):
    nt = t // tm
    row = lambda width: pl.BlockSpec((tm, width), lambda bi, ti: (bi * nt + ti, 0))
    col = lambda height: pl.BlockSpec((1, height, tm), lambda bi, ti: (bi, 0, ti))
    state = lambda height: pl.BlockSpec((1, 1, height, tm), lambda bi, ti: (layer, bi, 0, ti))
    return nt, row, col, state


_ANY = pl.BlockSpec(memory_space=pl.ANY)


def _fox_q_layout(wq):
    d = wq.shape[0]
    w3 = wq.reshape(d, FOX_HEADS, HEAD_DIM)
    wq_slab = jnp.concatenate([w3, jnp.zeros_like(w3)], axis=2).reshape(d, FOX_HEADS * LANES)
    slab = np.zeros((FOX_HEADS, LANES), np.float32)
    slab[:, HEAD_DIM:HEAD_DIM + D_PIECES] = -1.0
    return wq_slab, jnp.asarray(slab.reshape(1, FOX_HEADS * LANES))


def _fox_inproj(x, g, wq_slab, q_const, wkv_t, wf_t, kbuf, vbuf, layer, b, tm):
    n, d = x.shape
    t = n // b
    w = BRANCH_W
    wq_w = wq_slab.shape[1]
    nt, row, col, state = _row_specs(b, t, tm, layer)
    return pl.pallas_call(
        _fox_inproj_kernel,
        grid=(b, nt),
        in_specs=[row(d), _const_spec((1, d)), _const_spec((d, wq_w)), _const_spec((1, wq_w)),
                  _const_spec((2 * w, d)), _const_spec((FOX_HEADS, d)), _ANY, _ANY],
        out_specs=[row(wq_w), state(w), state(w), col(FOX_HEADS)],
        out_shape=[jax.ShapeDtypeStruct((n, wq_w), BF), jax.ShapeDtypeStruct(kbuf.shape, F32),
                   jax.ShapeDtypeStruct(vbuf.shape, F32), jax.ShapeDtypeStruct((b, FOX_HEADS, t), F32)],
        input_output_aliases={6: 1, 7: 2},
        compiler_params=_cparams("parallel", "parallel"),
    )(x, g, wq_slab, q_const, wkv_t, wf_t, kbuf, vbuf)


def _diff_inproj_kernel(x_ref, g_ref, wq_ref, wk_t_ref, wv_ref, cos_ref, sa_ref, sb_ref, cos_t_ref, sin_t_ref,
                        kbuf_ref, vbuf_ref, q_ref, kt_ref, v_ref):
    del kbuf_ref, vbuf_ref
    kt_ref = kt_ref.at[0]
    tm = x_ref.shape[0]
    xn = _rms(x_ref[...], g_ref[...]).astype(BF)
    w = BRANCH_W
    half = ROT_DIM // 2
    cos, sa, sb = cos_ref[...], sa_ref[...], sb_ref[...]
    yq = _dg(xn, wq_ref[...], _NN)
    for j in range(w // LANES):
        yj = yq[:, j * LANES:(j + 1) * LANES]
        rj = yj * cos + pltpu.roll(yj, LANES - half, 1) * sa + pltpu.roll(yj, half, 1) * sb
        q_ref[:, j * LANES:(j + 1) * LANES] = (rj * QK_SCALE).astype(BF)
    yk = _dg(wk_t_ref[...], xn, _NT)
    cos_t, sin_t = cos_t_ref[...], sin_t_ref[...]
    for gi in range(GROUPS):
        r0 = gi * HEAD_DIM
        y1, y2 = yk[r0:r0 + half, :], yk[r0 + half:r0 + ROT_DIM, :]
        kt_ref[0, r0:r0 + half, :] = y1 * cos_t - y2 * sin_t
        kt_ref[0, r0 + half:r0 + ROT_DIM, :] = y2 * cos_t + y1 * sin_t
        kt_ref[0, r0 + ROT_DIM:r0 + HEAD_DIM, :] = yk[r0 + ROT_DIM:r0 + HEAD_DIM, :]
    yv = _dg(xn, wv_ref[...], _NN)
    dv = 2 * HEAD_DIM
    for h in range(DIFF_HEADS):
        v_ref[0, pl.ds(h, tm, stride=DIFF_HEADS), :] = yv[:, h * dv:(h + 1) * dv]


def _diff_inproj(x, g, wq, wk_t, wv, tabs, kbuf, vbuf, layer, b, tm):
    n, d = x.shape
    t = n // b
    w = BRANCH_W
    nt, row, col, state = _row_specs(b, t, tm, layer)
    row_tab = pl.BlockSpec((tm, LANES), lambda bi, ti: (ti, 0))
    col_tab = pl.BlockSpec((ROT_DIM // 2, tm), lambda bi, ti: (0, ti))
    vspec = pl.BlockSpec((1, DIFF_HEADS * tm, 2 * HEAD_DIM), lambda bi, ti: (layer, bi * nt + ti, 0))
    return pl.pallas_call(
        _diff_inproj_kernel,
        grid=(b, nt),
        in_specs=[row(d), _const_spec((1, d)), _const_spec((d, w)), _const_spec((w, d)), _const_spec((d, w)),
                  row_tab, row_tab, row_tab, col_tab, col_tab, _ANY, _ANY],
        out_specs=[row(w), state(w), vspec],
        out_shape=[jax.ShapeDtypeStruct((n, w), BF), jax.ShapeDtypeStruct(kbuf.shape, F32),
                   jax.ShapeDtypeStruct(vbuf.shape, F32)],
        input_output_aliases={10: 1, 11: 2},
        compiler_params=_cparams("parallel", "parallel"),
    )(x, g, wq, wk_t, wv, *tabs, kbuf, vbuf)


def _rotary_tables(pos):
    half = ROT_DIM // 2
    inv = ROPE_THETA ** (-np.arange(half, dtype=np.float32) / half)
    ang = pos.astype(F32)[:, None] * jnp.asarray(inv, F32)[None, :]
    cos, sin = jnp.cos(ang), jnp.sin(ang)
    t = pos.shape[0]
    one = jnp.ones((t, HEAD_DIM - ROT_DIM), F32)
    zero = jnp.zeros((t, HEAD_DIM - ROT_DIM), F32)
    zh = jnp.zeros((t, half), F32)
    cos_h = jnp.concatenate([cos, cos, one], axis=1)
    sa_h = jnp.concatenate([-sin, zh, zero], axis=1)
    sb_h = jnp.concatenate([zh, sin, zero], axis=1)
    dup = lambda a: jnp.concatenate([a, a], axis=1)
    return dup(cos_h), dup(sa_h), dup(sb_h), cos.T, sin.T


def _rwkv_prep_kernel(x_ref, g_ref, wc_ref, shift0_ref, mu_ref, w0_ref, w2_ref, a0_ref, a2_ref, g2_ref,
                      kks_ref, ka_ref, rk_ref, seg_ref,
                      r_ref, lw_ref, km_ref, v_ref, kk_ref, kka_ref, gg_ref, bonus_ref, shift_ref,
                      carry_ref):
    t = pl.program_id(1)
    tm = x_ref.shape[0]
    w = BRANCH_W
    xn = _rms(x_ref[...], g_ref[...]).astype(BF)
    c = _dg(xn, wc_ref[...], _NN)

    @pl.when(t == 0)
    def _():
        carry_ref[...] = shift0_ref[0]

    row = lax.broadcasted_iota(jnp.int32, (tm, 1), 0)
    prev = jnp.where(row == 0, carry_ref[...], pltpu.roll(c, 1, 0))
    last = c[tm - 1:tm, :]
    carry_ref[...] = last
    shift_ref[0] = last

    xs = c + mu_ref[...] * (prev - c)
    cr, ck, cv = xs[:, 0:w], xs[:, w:2 * w], xs[:, 2 * w:3 * w]
    o = 3 * w
    xw = xs[:, o:o + DECAY_LORA]
    xa = xs[:, o + DECAY_LORA:o + DECAY_LORA + AAA_LORA]
    xg = xs[:, o + DECAY_LORA + AAA_LORA:]
    z = w0_ref[...] + _dot(jnp.tanh(xw), w2_ref[...])
    w_raw = -_softplus(-z) - 0.5
    a = jax.nn.sigmoid(a0_ref[...] + _dot(xa, a2_ref[...]))
    seg = seg_ref[...]
    kk = ck * kks_ref[...]
    kk = kk / jnp.maximum(jnp.sqrt(_dot_sel_right(kk * kk, seg)), 1e-12)
    kmod = ck * (1.0 + (a - 1.0) * ka_ref[...])
    r_ref[...] = cr
    lw_ref[...] = -jnp.exp(w_raw)
    km_ref[...] = kmod
    v_ref[...] = cv
    kk_ref[...] = kk
    kka_ref[...] = kk * a
    gg_ref[...] = _dot(jax.nn.sigmoid(xg), g2_ref[...])
    bonus_ref[...] = _dot_sel_right(cr * kmod * rk_ref[...], seg) * cv


def _rwkv_prep(x, g, wc, shift0, vecs, mats, seg, b, tm):
    n, d = x.shape
    t = n // b
    w = BRANCH_W
    mu, w0, a0, kks, ka, rk = vecs
    w2, a2, g2 = mats
    nt, row, _, _ = _row_specs(b, t, tm, 0)
    per_b = pl.BlockSpec((1, 1, RWKV_IN), lambda bi, ti: (bi, 0, 0))
    outs = [jax.ShapeDtypeStruct((n, w), F32)] * 8 + [jax.ShapeDtypeStruct((b, 1, RWKV_IN), F32)]
    return pl.pallas_call(
        _rwkv_prep_kernel,
        grid=(b, nt),
        in_specs=[row(d), _const_spec((1, d)), _const_spec((d, RWKV_IN)), per_b,
                  _const_spec((1, RWKV_IN)), _const_spec((1, w)), _const_spec((DECAY_LORA, w)),
                  _const_spec((1, w)), _const_spec((AAA_LORA, w)), _const_spec((GATE_LORA, w)),
                  _const_spec((1, w)), _const_spec((1, w)), _const_spec((1, w)), _const_spec((w, w))],
        out_specs=[row(w)] * 8 + [per_b],
        out_shape=outs,
        scratch_shapes=[pltpu.VMEM((1, RWKV_IN), F32)],
        compiler_params=_cparams("parallel", "arbitrary"),
    )(x, g, wc, shift0, mu, w0, w2, a0, a2, g2, kks, ka, rk, seg)


def _logf_cumsum_kernel(z_ref, bias_ref, lf_ref, d_ref, *, past, new):
    z = z_ref[0]
    width = z.shape[1]
    col = lax.broadcasted_iota(jnp.int32, z.shape, 1)
    zz = z + bias_ref[...]
    log_sig = jnp.minimum(zz, 0.0) - jnp.log(1.0 + jnp.exp(-jnp.abs(zz)))
    lf = jnp.where(col < past, z, jnp.where(col < past + new, log_sig, 0.0))
    lf_ref[0] = lf
    x = lf
    s = 1
    while s < width:
        x = x + jnp.where(col >= s, pltpu.roll(x, s, 1), 0.0)
        s *= 2
    d_ref[0] = x


def _logf_cumsum(z, bias, past, new):
    b, h, width = z.shape
    blk = pl.BlockSpec((1, h, width), lambda i: (i, 0, 0))
    return pl.pallas_call(
        functools.partial(_logf_cumsum_kernel, past=past, new=new),
        grid=(b,),
        in_specs=[blk, _const_spec((h, 1))],
        out_specs=[blk, blk],
        out_shape=[jax.ShapeDtypeStruct(z.shape, F32)] * 2,
        compiler_params=_cparams("parallel"),
    )(z, bias)


def _softmax_update(s, m_ref, l_ref, idx):
    m_prev = m_ref[idx]
    m_new = jnp.maximum(m_prev, jnp.max(s, axis=-1, keepdims=True))
    alpha = jnp.exp2(m_prev - m_new)
    p = jnp.exp2(s - m_new)
    l_ref[idx] = alpha * l_ref[idx] + jnp.sum(p, axis=-1, keepdims=True)
    m_ref[idx] = m_new
    return p.astype(BF), alpha


def _init_softmax_state(m_ref, l_ref, acc_ref):
    m_ref[...] = jnp.full(m_ref.shape, NEG_BIG, F32)
    l_ref[...] = jnp.zeros(l_ref.shape, F32)
    acc_ref[...] = jnp.zeros(acc_ref.shape, F32)


def _row_max_update(s, m_ref, idx):
    reps = s.shape[1] // LANES
    m_prev = m_ref[idx]
    m_new = jnp.maximum(m_prev, jnp.max(s, axis=-1, keepdims=True))
    alpha = jnp.exp2(m_prev - m_new)
    p = jnp.exp2(s - jnp.concatenate([m_new] * reps, axis=1))
    m_ref[idx] = m_new
    return p.astype(BF), alpha


def _init_prompt_state(m_ref, acc_ref):
    m_ref[...] = jnp.full(m_ref.shape, NEG_BIG, F32)
    acc_ref[...] = jnp.zeros(acc_ref.shape, F32)


def _causal_tile_pairs(nq):
    pairs = [(i, j) for i in range(nq) for j in range(i + 1)]
    return (jnp.asarray([p[0] for p in pairs], jnp.int32), jnp.asarray([p[1] for p in pairs], jnp.int32))


def _causal_blocks(i, j, block):
    pl.when(j < i)(functools.partial(block, False))
    pl.when(j == i)(functools.partial(block, True))


def _fox_prompt_kernel(it_ref, jt_ref, q_ref, kt_ref, vt_ref, dk_ref, o_ref, m_ref, acc_ref):
    pair = pl.program_id(1)
    i, j = it_ref[pair], jt_ref[pair]
    tq, tk = q_ref.shape[0], kt_ref.shape[3]
    heads = [slice(h * HEAD_DIM, (h + 1) * HEAD_DIM) for h in range(FOX_HEADS)]

    @pl.when(j == 0)
    def _():
        _init_prompt_state(m_ref, acc_ref)

    def block(masked):
        kb = kt_ref[0, 0].astype(BF)
        vb = vt_ref[0, 0].astype(BF)
        ones = jnp.ones((HEAD_DIM, tk), BF)
        d_hi, d_mid, d_lo = _split3(dk_ref[0] * LOG2E)
        rows = 2 * SUBLANES
        row = lax.broadcasted_iota(jnp.int32, (rows, tk), 0)
        pad = jnp.zeros((LANES - HEAD_DIM - rows, tk), BF)
        k_ops = []
        for h, hs in enumerate(heads):
            pieces = [jnp.broadcast_to(a[h:h + 1, :].astype(F32), (rows, tk)) for a in (d_hi, d_mid, d_lo)]
            extra = jnp.where(row == 0, pieces[0], jnp.where(row == 1, pieces[1], jnp.where(row == 2, pieces[2], 0.0)))
            k_ops.append(jnp.concatenate([kb[hs, :], extra.astype(BF), pad], axis=0))
        scores = [_dg(q_ref[:, h * LANES:(h + 1) * LANES], k_ops[h], _NN) for h in range(FOX_HEADS)]
        if masked:
            visible = (lax.broadcasted_iota(jnp.int32, (tq, tk), 1) <= lax.broadcasted_iota(jnp.int32, (tq, tk), 0))
        for h, hs in enumerate(heads):
            s = jnp.where(visible, scores[h], NEG_BIG) if masked else scores[h]
            p, alpha = _row_max_update(s, m_ref, h)
            v_ones = jnp.concatenate([vb[hs, :], ones], axis=0)
            acc_ref[h] = alpha * acc_ref[h] + _dg(p, v_ones, _NT)

    _causal_blocks(i, j, block)

    @pl.when(j == i)
    def _():
        for h, hs in enumerate(heads):
            acc = acc_ref[h]
            o_ref[:, hs] = acc[:, 0:HEAD_DIM] / acc[:, HEAD_DIM:2 * HEAD_DIM]


def _fox_prompt(q, kt, vt, d_row, layer, b, tq):
    n, wq = q.shape
    w = BRANCH_W
    t = n // b
    nq = t // tq
    i_tab, j_tab = _causal_tile_pairs(nq)
    qspec = lambda width: pl.BlockSpec((tq, width), lambda bi, p, it, jt: (bi * nq + it[p], 0))
    kspec = pl.BlockSpec((1, 1, w, tq), lambda bi, p, it, jt: (layer, bi, 0, jt[p]))
    grid_spec = pltpu.PrefetchScalarGridSpec(
        num_scalar_prefetch=2,
        grid=(b, i_tab.shape[0]),
        in_specs=[qspec(wq), kspec, kspec,
                  pl.BlockSpec((1, FOX_HEADS, tq), lambda bi, p, it, jt: (bi, 0, jt[p]))],
        out_specs=qspec(w),
        scratch_shapes=[pltpu.VMEM((FOX_HEADS, tq, LANES), F32), pltpu.VMEM((FOX_HEADS, tq, 2 * HEAD_DIM), F32)],
    )
    return pl.pallas_call(
        _fox_prompt_kernel,
        grid_spec=grid_spec,
        out_shape=jax.ShapeDtypeStruct((n, w), F32),
        compiler_params=_cparams("parallel", "arbitrary"),
    )(i_tab, j_tab, q, kt, vt, d_row)


def _diff_lambda(lam_ref, lam_init):
    p = lam_ref[...]
    s1 = jnp.sum(p[0:1, :] * p[1:2, :], axis=-1, keepdims=True)
    s2 = jnp.sum(p[2:3, :] * p[3:4, :], axis=-1, keepdims=True)
    return jnp.exp(s1) - jnp.exp(s2) + lam_init


def _diff_combine(acc0, l0, acc1, l1, lam, subln, lam_init):
    o = acc0 / l0 - lam * (acc1 / l1)
    return _rms(o, subln) * (1.0 - lam_init)


def _diff_prompt_kernel(it_ref, jt_ref, q_ref, kt_ref, v_ref, lam_ref, subln_ref, o_ref, m_ref, acc_ref, *,
                        lam_init):
    pair = pl.program_id(1)
    i, j = it_ref[pair], jt_ref[pair]
    tq, tk = q_ref.shape[0], kt_ref.shape[3]
    dv = 2 * HEAD_DIM

    @pl.when(j == 0)
    def _():
        _init_prompt_state(m_ref, acc_ref)

    def block(masked):
        kb = kt_ref[0, 0].astype(BF)
        ones = jnp.ones((tk, dv), BF)
        groups = [slice(gi * HEAD_DIM, (gi + 1) * HEAD_DIM) for gi in range(GROUPS)]
        scores = [_dg(q_ref[:, gs], kb[gs, :], _NN) for gs in groups]
        if masked:
            visible = ((lax.broadcasted_iota(jnp.int32, (tq, tk), 1) // CHUNK)
                       <= (lax.broadcasted_iota(jnp.int32, (tq, tk), 0) // CHUNK))
        for h in range(DIFF_HEADS):
            vh = v_ref[0, pl.ds(h, tk, stride=DIFF_HEADS), :].astype(BF)
            v_ones = jnp.concatenate([vh, ones], axis=1)
            for gi in (2 * h, 2 * h + 1):
                s = jnp.where(visible, scores[gi], NEG_BIG) if masked else scores[gi]
                p, alpha = _row_max_update(s, m_ref, gi)
                acc_ref[gi] = jnp.concatenate([alpha, alpha], axis=1) * acc_ref[gi] + _dg(p, v_ones, _NN)

    _causal_blocks(i, j, block)

    @pl.when(j == i)
    def _():
        lam = _diff_lambda(lam_ref, lam_init)
        for h in range(DIFF_HEADS):
            a0, a1 = acc_ref[2 * h], acc_ref[2 * h + 1]
            o_ref[:, h * dv:(h + 1) * dv] = _diff_combine(a0[:, 0:dv], a0[:, dv:], a1[:, 0:dv], a1[:, dv:], lam,
                                                          subln_ref[...], lam_init)


def _diff_prompt(q, kt, v, lam_p, subln, lam_init, layer, b, tq):
    n, w = q.shape
    t = n // b
    nq = t // tq
    i_tab, j_tab = _causal_tile_pairs(nq)
    qspec = pl.BlockSpec((tq, w), lambda bi, p, it, jt: (bi * nq + it[p], 0))
    ktspec = pl.BlockSpec((1, 1, w, tq), lambda bi, p, it, jt: (layer, bi, 0, jt[p]))
    vspec = pl.BlockSpec((1, DIFF_HEADS * tq, 2 * HEAD_DIM), lambda bi, p, it, jt: (layer, bi * nq + jt[p], 0))
    grid_spec = pltpu.PrefetchScalarGridSpec(
        num_scalar_prefetch=2,
        grid=(b, i_tab.shape[0]),
        in_specs=[qspec, ktspec, vspec, _const_spec((4, HEAD_DIM)), _const_spec((1, 2 * HEAD_DIM))],
        out_specs=qspec,
        scratch_shapes=[pltpu.VMEM((GROUPS, tq, LANES), F32), pltpu.VMEM((GROUPS, tq, 4 * HEAD_DIM), F32)],
    )
    return pl.pallas_call(
        functools.partial(_diff_prompt_kernel, lam_init=lam_init),
        grid_spec=grid_spec,
        out_shape=jax.ShapeDtypeStruct((n, w), F32),
        compiler_params=_cparams("parallel", "arbitrary"),
    )(i_tab, j_tab, q, kt, v, lam_p, subln)


def _sample_attn_kernel(*refs, fox, t_new, lam_init):
    if fox:
        q_ref, kc_ref, vc_ref, kn_ref, vn_ref, dq_ref, dkc_ref, dkn_ref, o_ref, m_ref, l_ref, acc_ref = refs
    else:
        q_ref, kc_ref, vc_ref, kn_ref, vn_ref, lam_ref, subln_ref, o_ref, m_ref, l_ref, acc_ref = refs
    j = pl.program_id(1)
    dv = 2 * HEAD_DIM
    pair = 2 * t_new

    @pl.when(j == 0)
    def _():
        _init_softmax_state(m_ref, l_ref, acc_ref)

    def expand(dk):
        return jnp.concatenate([jnp.broadcast_to(dk[gi:gi + 1, :], (t_new, dk.shape[1])) for gi in range(GROUPS)],
                               axis=0)

    def scores(kt, dk, causal):
        s = _dg(q_ref[0], kt.astype(BF), _NN)
        if fox:
            s = s + dq_ref[0] * LOG2E - expand(dk * LOG2E)
        if causal:
            qi = lax.broadcasted_iota(jnp.int32, s.shape, 0) % t_new
            kj = lax.broadcasted_iota(jnp.int32, s.shape, 1)
            s = jnp.where(kj <= qi, s, NEG_BIG)
        return _softmax_update(s, m_ref, l_ref, 0)

    def accumulate_diff(p, alpha, value_of_head):
        for h in range(DIFF_HEADS):
            rs = slice(h * pair, (h + 1) * pair)
            acc_ref[0, rs, :] = alpha[rs] * acc_ref[0, rs, :] + _dg(p[rs], value_of_head(h).astype(BF), _NN)

    tk = kc_ref.shape[3]
    if fox:
        p, alpha = scores(kc_ref[0, 0], dkc_ref[0], False)
        acc_ref[0] = alpha * acc_ref[0] + _dg(p, vc_ref[0, 0].astype(BF), _NT)
    else:
        p, alpha = scores(kc_ref[0, 0], None, False)
        accumulate_diff(p, alpha, lambda h: vc_ref[0, 0, pl.ds(h, tk, stride=DIFF_HEADS), :])

    @pl.when(j == pl.num_programs(1) - 1)
    def _():
        if fox:
            p, alpha = scores(kn_ref[0, 0], dkn_ref[0][:, 0:t_new], True)
            acc = alpha * acc_ref[0] + _dg(p, vn_ref[0, 0].astype(BF), _NT)
            l = l_ref[0]
            for gi in range(GROUPS):
                rs = slice(gi * t_new, (gi + 1) * t_new)
                cs = slice(gi * HEAD_DIM, (gi + 1) * HEAD_DIM)
                o_ref[:, cs] = acc[rs, cs] / l[rs]
        else:
            p, alpha = scores(kn_ref[0, 0], None, False)
            accumulate_diff(p, alpha, lambda h: vn_ref[0, pl.ds(h, t_new, stride=DIFF_HEADS), :])
            acc = acc_ref[0]
            l = l_ref[0]
            lam = _diff_lambda(lam_ref, lam_init)
            for h in range(DIFF_HEADS):
                r0 = slice(h * pair, h * pair + t_new)
                r1 = slice(h * pair + t_new, (h + 1) * pair)
                o_ref[:, h * dv:(h + 1) * dv] = _diff_combine(acc[r0], l[r0], acc[r1], l[r1], lam, subln_ref[...],
                                                              lam_init)


def _sample_attn(qbd, kc, vc, kn, vn, extra, *, layer, fox, b, t_new, past, tk, lam_init=0.0):
    w = BRANCH_W
    rows = GROUPS * t_new
    nk = past // tk
    qspec = pl.BlockSpec((1, rows, w), lambda bi, j: (bi, 0, 0))
    kcspec = pl.BlockSpec((1, 1, w, tk), lambda bi, j: (layer, bi, 0, j))
    knspec = pl.BlockSpec((1, 1, w, t_new), lambda bi, j: (layer, bi, 0, 0))
    ospec = pl.BlockSpec((t_new, w), lambda bi, j: (bi, 0))
    if fox:
        dq, d_row = extra
        vcspec, vnspec, acc_w = kcspec, knspec, w
        especs = [pl.BlockSpec((1, rows, 1), lambda bi, j: (bi, 0, 0)),
                  pl.BlockSpec((1, GROUPS, tk), lambda bi, j: (bi, 0, j)),
                  pl.BlockSpec((1, GROUPS, LANES), lambda bi, j: (bi, 0, past // LANES))]
        eargs = [dq, d_row, d_row]
    else:
        vcspec = pl.BlockSpec((1, 1, DIFF_HEADS * tk, 2 * HEAD_DIM), lambda bi, j: (layer, bi, j, 0))
        vnspec = pl.BlockSpec((1, DIFF_HEADS * t_new, 2 * HEAD_DIM), lambda bi, j: (layer, bi, 0))
        acc_w = 2 * HEAD_DIM
        especs = [_const_spec((4, HEAD_DIM)), _const_spec((1, 2 * HEAD_DIM))]
        eargs = list(extra)
    return pl.pallas_call(
        functools.partial(_sample_attn_kernel, fox=fox, t_new=t_new, lam_init=lam_init),
        grid=(b, nk),
        in_specs=[qspec, kcspec, vcspec, knspec, vnspec] + especs,
        out_specs=ospec,
        out_shape=jax.ShapeDtypeStruct((b * t_new, w), F32),
        scratch_shapes=[pltpu.VMEM((1, rows, 1), F32), pltpu.VMEM((1, rows, 1), F32),
                        pltpu.VMEM((1, rows, acc_w), F32)],
        compiler_params=_cparams("parallel", "arbitrary"),
    )(qbd, kc, vc, kn, vn, *eargs)


def _block_diag_queries(q, b, t_new):
    q4 = q.reshape(b, t_new, GROUPS, HEAD_DIM)
    eye = jnp.eye(GROUPS, dtype=q.dtype)
    qbd = q4.transpose(0, 2, 1, 3)[:, :, :, None, :] * eye[None, :, None, :, None]
    return qbd.reshape(b, GROUPS * t_new, GROUPS * HEAD_DIM)


def _rwkv_scan_kernel(r_ref, lw_ref, k_ref, v_ref, kk_ref, kka_ref, gg_ref, bonus_ref, gain_ref, bias_ref,
                      s0_ref, o_ref, sfin_ref, s_ref):
    c = pl.program_id(1)
    nb, ch = r_ref.shape[0], r_ref.shape[1]
    n = HEAD_DIM

    @pl.when(c == 0)
    def _():
        s_ref[...] = s0_ref[...]

    ri = lax.broadcasted_iota(jnp.int32, (ch, ch), 0)
    ci = lax.broadcasted_iota(jnp.int32, (ch, ch), 1)
    tri = (ci <= ri).astype(BF)
    ri2 = lax.broadcasted_iota(jnp.int32, (ch, 2 * ch), 0)
    ci2 = lax.broadcasted_iota(jnp.int32, (ch, 2 * ch), 1)
    ci2m = jnp.where(ci2 >= ch, ci2 - ch, ci2)
    strict_right = (ci2 >= ch) & (ci2m < ri2)
    strict_left = ci < ri
    is_x = lax.broadcasted_iota(jnp.int32, (ch, ch + n), 1) >= ch
    incl_both = ci2m <= ri2
    steps = max(1, int(math.ceil(math.log2(ch))))
    heads = [slice(h * n, (h + 1) * n) for h in range(RWKV_HEADS)]

    prep = []
    for bi in range(nb):
        lw = lw_ref[bi]
        cum = _dot_sel(tri, lw)
        cum_end = cum[ch - 1:ch, :]
        kk, kka, kmod, v_all = kk_ref[bi], kka_ref[bi], k_ref[bi], v_ref[bi]
        e_neg = jnp.exp(-cum)
        e_tail = jnp.exp(cum_end - cum)
        prep.append(dict(
            ar=_split2(jnp.concatenate([-kk * jnp.exp(cum - lw), r_ref[bi] * jnp.exp(cum)], axis=0)),
            bk=_split2(jnp.concatenate([kka * e_neg, kmod * e_neg], axis=0)),
            bk_tail=_split2(jnp.concatenate([kka * e_tail, kmod * e_tail], axis=0)),
            vv=_split2(jnp.concatenate([v_all, v_all], axis=0)),
            v=v_all, decay_end=jnp.exp(cum_end)))
    units = [(bi, h, heads[h]) for bi in range(nb) for h in range(RWKV_HEADS)]
    s_old = [s_ref[bi, h] for bi, h, _ in units]

    def dot3s(a2, b2, dims):
        (ah, al), (bh, bl) = a2, b2
        m = ah.shape[0]
        r = _dg(jnp.concatenate([ah, al], axis=0), bh, dims)
        return r[0:m] + r[m:] + _dg(ah, bl, dims)

    def dot3_tn(a2, b2):
        (ah, al), (bh, bl) = a2, b2
        return _dg(ah, bh, _TN) + (_dg(ah, bl, _TN) + _dg(al, bh, _TN))

    def nil_times(zh):
        a2, b2 = _split2(zh[:, 0:ch]), _split2(zh)
        wz = zh.shape[1]
        if wz % LANES:
            return dot3s(a2, b2, _NN)
        r = _dg(jnp.concatenate(a2, axis=0), jnp.concatenate(b2, axis=1), _NN)
        return (r[0:ch, 0:wz] + r[ch:, 0:wz]) + (r[0:ch, wz:] + r[ch:, wz:])

    cols = lambda a2, hs: (a2[0][:, hs], a2[1][:, hs])
    ga = []
    for u, (bi, h, hs) in enumerate(units):
        sh, sl = _split2(s_old[u])
        bk = prep[bi]["bk"]
        rhs = (jnp.concatenate([bk[0][:, hs], sh], axis=0), jnp.concatenate([bk[1][:, hs], sl], axis=0))
        ga.append(dot3s(cols(prep[bi]["ar"], hs), rhs, _NT))
    z = []
    for u, (bi, h, hs) in enumerate(units):
        g_top = ga[u][0:ch, 0:2 * ch]
        x0 = ga[u][0:ch, 2 * ch:] + dot3s(_split2(jnp.where(strict_right, g_top, 0.0)), cols(prep[bi]["vv"], hs), _NN)
        z.append(jnp.concatenate([jnp.where(strict_left, g_top[:, 0:ch], 0.0), x0], axis=1))
    for _ in range(steps):
        z = [nil_times(zh) + jnp.where(is_x, zh, 0.0) for zh in z]
    for u, (bi, h, hs) in enumerate(units):
        x = z[u][:, ch:]
        xv = _split2(jnp.concatenate([x, prep[bi]["v"][:, hs]], axis=0))
        y = ga[u][ch:, 2 * ch:] + dot3s(_split2(jnp.where(incl_both, ga[u][ch:, 0:2 * ch], 0.0)), xv, _NN)
        s_ref[bi, h] = s_old[u] * prep[bi]["decay_end"][:, hs] + dot3_tn(xv, cols(prep[bi]["bk_tail"], hs))

        mu = jnp.mean(y, axis=-1, keepdims=True)
        yc = y - mu
        var = jnp.mean(yc * yc, axis=-1, keepdims=True)
        yn = yc * lax.rsqrt(var + RWKV_GN_EPS) * gain_ref[:, hs] + bias_ref[:, hs]
        o_ref[bi, :, hs] = (yn + bonus_ref[bi, :, hs]) * gg_ref[bi, :, hs]

    @pl.when(c == pl.num_programs(1) - 1)
    def _():
        sfin_ref[...] = s_ref[...]


def _rwkv_scan(r, lw, km, v, kk, kka, gg, bonus, gain, bias, s0, b, ch, nb):
    n, w = r.shape
    t = n // b
    rows = [a.reshape(b, t, w) for a in (r, lw, km, v, kk, kka, gg, bonus)]
    row = pl.BlockSpec((nb, ch, w), lambda bi, ci: (bi, ci, 0))
    st = pl.BlockSpec((nb, RWKV_HEADS, HEAD_DIM, HEAD_DIM), lambda bi, ci: (bi, 0, 0, 0))
    out, s_fin = pl.pallas_call(
        _rwkv_scan_kernel,
        grid=(b // nb, t // ch),
        in_specs=[row] * 8 + [_const_spec((1, w)), _const_spec((1, w)), st],
        out_specs=[row, st],
        out_shape=[jax.ShapeDtypeStruct((b, t, w), F32), jax.ShapeDtypeStruct(s0.shape, F32)],
        scratch_shapes=[pltpu.VMEM((nb, RWKV_HEADS, HEAD_DIM, HEAD_DIM), F32)],
        compiler_params=_cparams("parallel", "arbitrary"),
    )(*rows, gain, bias, s0)
    return out.reshape(n, w), s_fin


def _merge_kernel(x_ref, oa_ref, ob_ref, oc_ref, gpre_ref, wg_ref, wb_ref, wo_ref, gpost_ref, y_ref):
    x = x_ref[...]
    d = x.shape[1]
    xn = _rms(x, gpre_ref[...]).astype(BF)
    acc = jnp.zeros(x.shape, F32)
    for i, o_ref in enumerate((oa_ref, ob_ref, oc_ref)):
        gate = jax.nn.sigmoid(_dg(xn, wg_ref[:, i * d:(i + 1) * d], _NN))
        acc = acc + gate * _dot(o_ref[...], wb_ref[i])
    y = _dot(acc, wo_ref[...])
    y_ref[...] = x + _rms(y, gpost_ref[...])


def _merge(x, oa, ob, oc, gpre, wg, wb, wo, gpost, tm):
    n, d = x.shape
    w = BRANCH_W
    row = lambda width: pl.BlockSpec((tm, width), lambda i: (i, 0))
    return pl.pallas_call(
        _merge_kernel,
        grid=(n // tm,),
        in_specs=[row(d), row(w), row(w), row(w), _const_spec((1, d)), _const_spec((d, 3 * d)),
                  _const_spec((3, w, d)), _const_spec((d, d)), _const_spec((1, d))],
        out_specs=row(d),
        out_shape=jax.ShapeDtypeStruct((n, d), F32),
        compiler_params=_cparams("parallel"),
    )(x, oa, ob, oc, gpre, wg, wb, wo, gpost)


def _memkv_kernel(m_ref, g_ref, w_ref, k_ref, v_ref):
    xn = _rms(m_ref[...], g_ref[...]).astype(BF)
    w = k_ref.shape[1]
    k_ref[...] = _dg(xn, w_ref[:, 0:w], _NN)
    v_ref[...] = _dg(xn, w_ref[:, w:2 * w], _NN)


def _memkv(mem, g, w_xkv, tm):
    n, d = mem.shape
    w = w_xkv.shape[1] // 2
    row = lambda width: pl.BlockSpec((tm, width), lambda i: (i, 0))
    return pl.pallas_call(
        _memkv_kernel,
        grid=(n // tm,),
        in_specs=[row(d), _const_spec((1, d)), _const_spec((d, 2 * w))],
        out_specs=[row(w), row(w)],
        out_shape=[jax.ShapeDtypeStruct((n, w), F32)] * 2,
        compiler_params=_cparams("parallel"),
    )(mem, g, w_xkv)


def _xattn_kernel(x_ref, gpre_ref, wq_ref, mk_ref, mv_ref, wo_ref, gpost_ref, y_ref):
    x = x_ref[...]
    xn = _rms(x, gpre_ref[...])
    q = _dot(xn, wq_ref[...])
    hd = XATTN_HEAD_DIM
    outs = []
    for h in range(XATTN_HEADS):
        hs = slice(h * hd, (h + 1) * hd)
        s = _dot_nt(q[:, hs], mk_ref[0, 0, :, hs]) * (hd ** -0.5)
        p = jnp.exp(s - jnp.max(s, axis=-1, keepdims=True))
        p = p / jnp.sum(p, axis=-1, keepdims=True)
        outs.append(_dot(p, mv_ref[0, 0, :, hs]))
    o = jnp.concatenate(outs, axis=-1)
    y_ref[...] = x + _rms(_dot(o, wo_ref[...]), gpost_ref[...])


def _xattn(x, gpre, wq, mk, mv, wo, gpost, layer, b, tm):
    n, d = x.shape
    t = n // b
    nt = t // tm
    n_mem, w = mk.shape[2], mk.shape[3]
    row = pl.BlockSpec((tm, d), lambda bi, ti: (bi * nt + ti, 0))
    mem = pl.BlockSpec((1, 1, n_mem, w), lambda bi, ti: (layer, bi, 0, 0))
    return pl.pallas_call(
        _xattn_kernel,
        grid=(b, nt),
        in_specs=[row, _const_spec((1, d)), _const_spec((d, w)), mem, mem, _const_spec((w, d)),
                  _const_spec((1, d))],
        out_specs=row,
        out_shape=jax.ShapeDtypeStruct((n, d), F32),
        compiler_params=_cparams("parallel", "parallel"),
    )(x, gpre, wq, mk, mv, wo, gpost)


def _ffn_kernel(x_ref, gpre_ref, wu_ref, wv_ref, wo_ref, gpost_ref, y_ref, xn_ref, acc_ref):
    j = pl.program_id(1)

    @pl.when(j == 0)
    def _():
        xn_ref[...] = _rms(x_ref[...], gpre_ref[...]).astype(BF)
        acc_ref[...] = jnp.zeros(acc_ref.shape, F32)

    xn = xn_ref[...]
    u = _dg(xn, wu_ref[...], _NN)
    v = _dg(xn, wv_ref[...], _NN)
    acc_ref[...] += _dot(u * jax.nn.sigmoid(u) * v, wo_ref[...])

    @pl.when(j == pl.num_programs(1) - 1)
    def _():
        y_ref[...] = x_ref[...] + _rms(acc_ref[...], gpost_ref[...])


def _ffn(x, gpre, w_in, w_out, gpost, tm, th):
    n, d = x.shape
    hidden = w_out.shape[0]
    nh = hidden // th
    row = pl.BlockSpec((tm, d), lambda i, j: (i, 0))
    return pl.pallas_call(
        _ffn_kernel,
        grid=(n // tm, nh),
        in_specs=[row, _const_spec((1, d)),
                  pl.BlockSpec((d, th), lambda i, j: (0, j)),
                  pl.BlockSpec((d, th), lambda i, j: (0, nh + j)),
                  pl.BlockSpec((th, d), lambda i, j: (j, 0)),
                  _const_spec((1, d))],
        out_specs=row,
        out_shape=jax.ShapeDtypeStruct((n, d), F32),
        scratch_shapes=[pltpu.VMEM((tm, d), BF), pltpu.VMEM((tm, d), F32)],
        compiler_params=_cparams("parallel", "arbitrary"),
    )(x, gpre, w_in, w_in, w_out, gpost)


def _tile(n, pref):
    t = min(n, pref)
    assert n % t == 0, (n, t)
    return t


def _layer_weights(l, wts):
    p = {k: v[l] for k, v in wts.items()}
    w = BRANCH_W
    w_in = p["w_in"].astype(BF)
    w_in_t = p["w_in"].T.astype(BF)
    gate_w = w_in.shape[1] - (6 * w + FOX_HEADS + RWKV_IN)
    o_fox, o_f, o_diff = 0, 3 * w, 3 * w + FOX_HEADS
    o_rwkv = o_diff + 3 * w
    o_gate = o_rwkv + RWKV_IN
    row = lambda a: a.reshape(1, -1)
    seg = np.kron(np.eye(RWKV_HEADS, dtype=np.float32), np.ones((HEAD_DIM, HEAD_DIM), np.float32))
    return dict(
        g_mix_pre=row(p["norm_mix_pre"]), g_mix_post=row(p["norm_mix_post"]),
        w_fox_q=_fox_q_layout(w_in[:, o_fox:o_fox + w]), w_fox_kv_t=w_in_t[o_fox + w:o_fox + 3 * w],
        w_fox_f_t=w_in_t[o_f:o_f + FOX_HEADS],
        w_diff_q=w_in[:, o_diff:o_diff + w], w_diff_k_t=w_in_t[o_diff + w:o_diff + 2 * w],
        w_diff_v=w_in[:, o_diff + 2 * w:o_diff + 3 * w],
        w_rwkv=w_in[:, o_rwkv:o_rwkv + RWKV_IN], w_gate=w_in[:, o_gate:o_gate + gate_w],
        fox_bias=p["fox_forget_bias"].reshape(FOX_HEADS, 1),
        diff_lambda=p["diff_lambda"], diff_subln=row(p["diff_subln"]),
        rwkv_vecs=tuple(row(p[k]) for k in ("rwkv_mu", "rwkv_w0", "rwkv_a0", "rwkv_kk_scale", "rwkv_ka",
                                            "rwkv_rk")),
        rwkv_mats=tuple(p[k].astype(BF) for k in ("rwkv_w2", "rwkv_a2", "rwkv_g2")),
        rwkv_seg=jnp.asarray(seg, BF),
        rwkv_gain=row(p["rwkv_ln_gain"]), rwkv_bias=row(p["rwkv_ln_bias"]),
        w_branch=p["w_branch"].astype(BF), w_out=p["w_out"].astype(BF),
        g_x_pre=row(p["norm_x_pre"]), g_x_post=row(p["norm_x_post"]), g_mem=row(p["norm_mem"]),
        w_xq=p["w_xq"].astype(BF), w_xkv=p["w_xkv"].astype(BF), w_xo=p["w_xo"].astype(BF),
        g_ffn_pre=row(p["norm_ffn_pre"]), g_ffn_post=row(p["norm_ffn_post"]),
        w_ffn_in=p["w_ffn_in"].astype(BF), w_ffn_out=p["w_ffn_out"].astype(BF),
    )


def _new_state_buffers(depth, b, t):
    kt = lambda: jnp.zeros((depth, b, BRANCH_W, t), F32)
    return dict(kf=kt(), vf=kt(), kd=kt(), vd=jnp.zeros((depth, b * t * DIFF_HEADS, 2 * HEAD_DIM), F32))


def _mixer(x, lw, l, b, t, past, tabs, bufs):
    n = x.shape[0]
    lam_init = 0.8 - 0.6 * math.exp(-0.3 * l)
    tm = _tile(t, 512)
    qf, kf_t, vf_t, f_bt = _fox_inproj(x, lw["g_mix_pre"], *lw["w_fox_q"], lw["w_fox_kv_t"], lw["w_fox_f_t"],
                                       bufs["kf"], bufs["vf"], l, b, tm)
    qd, kd_t, vd = _diff_inproj(x, lw["g_mix_pre"], lw["w_diff_q"], lw["w_diff_k_t"], lw["w_diff_v"], tabs,
                                bufs["kd"], bufs["vd"], l, b, tm)
    bufs = dict(kf=kf_t, vf=vf_t, kd=kd_t, vd=vd)

    if past is None:
        p_len = 0
        z = f_bt
        shift0 = jnp.zeros((b, 1, RWKV_IN), F32)
        s0 = jnp.zeros((b, RWKV_HEADS, HEAD_DIM, HEAD_DIM), F32)
    else:
        fox_kt, fox_vt, fox_lf_t, diff_kt, diff_v, s0, shift0 = past
        p_len = fox_kt.shape[3]
        pad = (-(p_len + t)) % LANES
        z = jnp.concatenate([fox_lf_t[l], f_bt, jnp.zeros((b, FOX_HEADS, pad), F32)], axis=2)
        s0 = s0[l]
        shift0 = shift0[l].reshape(b, 1, RWKV_IN)
    lf, d_row = _logf_cumsum(z, lw["fox_bias"], p_len, t)
    log_f = lf[:, :, p_len:p_len + t].transpose(0, 2, 1)

    if past is None:
        tq = _tile(t, 512)
        out_a = _fox_prompt(qf, kf_t, vf_t, d_row, l, b, tq)
        out_b = _diff_prompt(qd, kd_t, vd, lw["diff_lambda"], lw["diff_subln"], lam_init, l, b, tq)
    else:
        assert p_len % CHUNK == 0 and t <= CHUNK and p_len % LANES == 0
        tk = _tile(p_len, 2048)
        dq = d_row[:, :, p_len:p_len + t].reshape(b, GROUPS * t, 1)
        qf_feat = qf.reshape(n, FOX_HEADS, LANES)[:, :, 0:HEAD_DIM].reshape(n, BRANCH_W)
        out_a = _sample_attn(_block_diag_queries(qf_feat, b, t), fox_kt, fox_vt, kf_t, vf_t, (dq, d_row),
                             layer=l, fox=True, b=b, t_new=t, past=p_len, tk=tk)
        out_b = _sample_attn(_block_diag_queries(qd, b, t), diff_kt, diff_v, kd_t, vd,
                             (lw["diff_lambda"], lw["diff_subln"]),
                             layer=l, fox=False, b=b, t_new=t, past=p_len, tk=tk, lam_init=lam_init)

    r, lgw, km, v, kk, kka, gg, bonus, shift = _rwkv_prep(
        x, lw["g_mix_pre"], lw["w_rwkv"], shift0, lw["rwkv_vecs"], lw["rwkv_mats"], lw["rwkv_seg"], b, tm)
    ch = _tile(t, CHUNK)
    out_c, s_new = _rwkv_scan(r, lgw, km, v, kk, kka, gg, bonus, lw["rwkv_gain"], lw["rwkv_bias"], s0, b, ch,
                              _tile(b, 4))

    y = _merge(x, out_a, out_b, out_c, lw["g_mix_pre"], lw["w_gate"], lw["w_branch"], lw["w_out"],
               lw["g_mix_post"], _tile(n, 256))
    state = (log_f, s_new, shift.reshape(b, RWKV_IN))
    return y, state, bufs


def _layer(x, lw, l, b, t, past, mem_k, mem_v, mem_layer, tabs, bufs):
    n = x.shape[0]
    x, state, bufs = _mixer(x, lw, l, b, t, past, tabs, bufs)
    x = _xattn(x, lw["g_x_pre"], lw["w_xq"], mem_k, mem_v, lw["w_xo"], lw["g_x_post"], mem_layer, b,
               _tile(t, 512))
    x = _ffn(x, lw["g_ffn_pre"], lw["w_ffn_in"], lw["w_ffn_out"], lw["g_ffn_post"], _tile(n, 1024), 256)
    return x, state, bufs


def _assemble_states(states, bufs, b, t):
    log_f, s_new, shift = (jnp.stack(e) for e in zip(*states))
    depth = bufs["kf"].shape[0]
    tok_major = lambda a: a.reshape(depth, b, FOX_HEADS, HEAD_DIM, t).transpose(0, 1, 4, 2, 3)
    fox_k, fox_v = tok_major(bufs["kf"]), tok_major(bufs["vf"])
    diff_k = tok_major(bufs["kd"]).reshape(depth, b, t, DIFF_HEADS, 2, HEAD_DIM)
    diff_v = bufs["vd"].reshape(depth, b, t, DIFF_HEADS, 2 * HEAD_DIM)
    return fox_k, fox_v, log_f, diff_k, diff_v, s_new, shift


def kernel(x_prompt, x_sample, mem_prompt, cache_fox_k, cache_fox_v, cache_fox_logf, cache_diff_k, cache_diff_v, state_rwkv, state_rwkv_shift, cache_mem_k, cache_mem_v, norm_mix_pre, norm_mix_post, w_in, fox_forget_bias, diff_lambda, diff_subln, rwkv_mu, rwkv_w0, rwkv_w2, rwkv_a0, rwkv_a2, rwkv_g2, rwkv_kk_scale, rwkv_ka, rwkv_rk, rwkv_ln_gain, rwkv_ln_bias, w_branch, w_out, norm_x_pre, norm_x_post, norm_mem, w_xq, w_xkv, w_xo, norm_ffn_pre, norm_ffn_post, w_ffn_in, w_ffn_out):
    wts = dict(norm_mix_pre=norm_mix_pre, norm_mix_post=norm_mix_post, w_in=w_in, fox_forget_bias=fox_forget_bias,
               diff_lambda=diff_lambda, diff_subln=diff_subln, rwkv_mu=rwkv_mu, rwkv_w0=rwkv_w0, rwkv_w2=rwkv_w2,
               rwkv_a0=rwkv_a0, rwkv_a2=rwkv_a2, rwkv_g2=rwkv_g2, rwkv_kk_scale=rwkv_kk_scale, rwkv_ka=rwkv_ka,
               rwkv_rk=rwkv_rk, rwkv_ln_gain=rwkv_ln_gain, rwkv_ln_bias=rwkv_ln_bias, w_branch=w_branch,
               w_out=w_out, norm_x_pre=norm_x_pre, norm_x_post=norm_x_post, norm_mem=norm_mem, w_xq=w_xq,
               w_xkv=w_xkv, w_xo=w_xo, norm_ffn_pre=norm_ffn_pre, norm_ffn_post=norm_ffn_post,
               w_ffn_in=w_ffn_in, w_ffn_out=w_ffn_out)
    depth = w_in.shape[0]
    bp, tp, d = x_prompt.shape
    bs, ts, _ = x_sample.shape
    p_len = cache_fox_k.shape[2]
    n_mem = mem_prompt.shape[1]
    xw = XATTN_HEADS * XATTN_HEAD_DIM
    w = BRANCH_W

    tabs_p = _rotary_tables(jnp.arange(tp, dtype=jnp.int32))
    tabs_s = _rotary_tables(p_len + jnp.arange(ts, dtype=jnp.int32))

    past_s = (cache_fox_k.transpose(0, 1, 3, 4, 2).reshape(depth, bs, w, p_len),
              cache_fox_v.transpose(0, 1, 3, 4, 2).reshape(depth, bs, w, p_len),
              cache_fox_logf.transpose(0, 1, 3, 2),
              cache_diff_k.transpose(0, 1, 3, 4, 5, 2).reshape(depth, bs, w, p_len),
              cache_diff_v.reshape(depth, bs, p_len * DIFF_HEADS, 2 * HEAD_DIM),
              state_rwkv, state_rwkv_shift)
    mem_k_s = cache_mem_k.reshape(depth, bs, n_mem, xw)
    mem_v_s = cache_mem_v.reshape(depth, bs, n_mem, xw)

    xp = x_prompt.reshape(bp * tp, d)
    xs = x_sample.reshape(bs * ts, d)
    mem = mem_prompt.reshape(bp * n_mem, d)
    p_new, s_new, p_mk, p_mv = [], [], [], []
    bufs_p = _new_state_buffers(depth, bp, tp)
    bufs_s = _new_state_buffers(depth, bs, ts)
    for l in range(depth):
        lw = _layer_weights(l, wts)
        mk, mv = _memkv(mem, lw["g_mem"], lw["w_xkv"], _tile(bp * n_mem, 256))
        xp, st_p, bufs_p = _layer(xp, lw, l, bp, tp, None, mk.reshape(1, bp, n_mem, xw),
                                  mv.reshape(1, bp, n_mem, xw), 0, tabs_p, bufs_p)
        xs, st_s, bufs_s = _layer(xs, lw, l, bs, ts, past_s, mem_k_s, mem_v_s, l, tabs_s, bufs_s)
        p_new.append(st_p)
        s_new.append(st_s)
        p_mk.append(mk.reshape(bp, n_mem, XATTN_HEADS, XATTN_HEAD_DIM))
        p_mv.append(mv.reshape(bp, n_mem, XATTN_HEADS, XATTN_HEAD_DIM))
    p_out = _assemble_states(p_new, bufs_p, bp, tp)
    s_out = _assemble_states(s_new, bufs_s, bs, ts)
    return (xp.reshape(bp, tp, d), xs.reshape(bs, ts, d)) + p_out + (jnp.stack(p_mk), jnp.stack(p_mv)) + s_out
```

```python
import functools
import math

import numpy as np
import jax
import jax.numpy as jnp
from jax import lax
from jax.experimental import pallas as pl
from jax.experimental.pallas import tpu as pltpu

F32 = jnp.float32
BF = jnp.bfloat16

CHUNK = 64
HEAD_DIM = 64
FOX_HEADS = 8
DIFF_HEADS = 4
RWKV_HEADS = 8
GROUPS = 8
BRANCH_W = 512
DECAY_LORA = 64
AAA_LORA = 64
GATE_LORA = 128
RWKV_IN = 3 * BRANCH_W + DECAY_LORA + AAA_LORA + GATE_LORA
ROT_DIM = HEAD_DIM // 4
ROPE_THETA = 500000.0
XATTN_HEADS = 4
XATTN_HEAD_DIM = 128
RMS_EPS = 1e-6
RWKV_GN_EPS = 64e-5
ATTN_SCALE = HEAD_DIM ** -0.5
LOG2E = math.log2(math.e)
QK_SCALE = ATTN_SCALE * LOG2E
D_PIECES = 3
NEG_BIG = -1e30

LANES = 128
SUBLANES = 8
VMEM_LIMIT_BYTES = 56 * 1024 * 1024


def _cparams(*sem):
    return pltpu.CompilerParams(dimension_semantics=sem, vmem_limit_bytes=VMEM_LIMIT_BYTES)


def _const_spec(shape):
    nd = len(shape)
    return pl.BlockSpec(shape, lambda *_: (0,) * nd)


def _rms(x, g):
    return x * lax.rsqrt(jnp.mean(x * x, axis=-1, keepdims=True) + RMS_EPS) * g


_NN = ((1,), (0,))
_NT = ((1,), (1,))
_TN = ((0,), (0,))


def _dg(a, b, dims):
    return lax.dot_general(a, b, (dims, ((), ())), preferred_element_type=F32)


def _dot(a, b):
    return _dg(a.astype(BF), b.astype(BF), _NN)


def _dot_nt(a, b):
    return _dg(a.astype(BF), b.astype(BF), _NT)


def _split2(x):
    hi = x.astype(BF)
    lo = (x - hi.astype(F32)).astype(BF)
    return hi, lo


def _split3(x):
    hi = x.astype(BF)
    r1 = x - hi.astype(F32)
    mid = r1.astype(BF)
    lo = (r1 - mid.astype(F32)).astype(BF)
    return hi, mid, lo


def _dot_sel(sel, x):
    hi, mid, lo = _split3(x)
    return _dg(sel, hi, _NN) + (_dg(sel, mid, _NN) + _dg(sel, lo, _NN))


def _dot_sel_right(x, sel):
    hi, lo = _split2(x)
    return _dg(hi, sel, _NN) + _dg(lo, sel, _NN)


def _softplus(x):
    return jnp.maximum(x, 0.0) + jnp.log(1.0 + jnp.exp(-jnp.abs(x)))


def _fox_inproj_kernel(x_ref, g_ref, wq_ref, qc_ref, wkv_t_ref, wf_t_ref, kbuf_ref, vbuf_ref,
                       q_ref, kt_ref, vt_ref, f_ref):
    del kbuf_ref, vbuf_ref
    xn = _rms(x_ref[...], g_ref[...]).astype(BF)
    w = BRANCH_W
    q_ref[...] = (_dg(xn, wq_ref[...], _NN) * QK_SCALE + qc_ref[...]).astype(BF)
    kt_ref[0, 0] = _dg(wkv_t_ref[0:w, :], xn, _NT)
    vt_ref[0, 0] = _dg(wkv_t_ref[w:2 * w, :], xn, _NT)
    f_ref[0] = _dg(wf_t_ref[...], xn, _NT)


def _row_specs(b, t, tm, layer):
    nt = t // tm
    row = lambda width: pl.BlockSpec((tm, width), lambda bi, ti: (bi * nt + ti, 0))
    col = lambda height: pl.BlockSpec((1, height, tm), lambda bi, ti: (bi, 0, ti))
    state = lambda height: pl.BlockSpec((1, 1, height, tm), lambda bi, ti: (layer, bi, 0, ti))
    return nt, row, col, state


_ANY = pl.BlockSpec(memory_space=pl.ANY)


def _fox_q_layout(wq):
    d = wq.shape[0]
    w3 = wq.reshape(d, FOX_HEADS, HEAD_DIM)
    wq_slab = jnp.concatenate([w3, jnp.zeros_like(w3)], axis=2).reshape(d, FOX_HEADS * LANES)
    slab = np.zeros((FOX_HEADS, LANES), np.float32)
    slab[:, HEAD_DIM:HEAD_DIM + D_PIECES] = -1.0
    return wq_slab, jnp.asarray(slab.reshape(1, FOX_HEADS * LANES))


def _fox_inproj(x, g, wq_slab, q_const, wkv_t, wf_t, kbuf, vbuf, layer, b, tm):
    n, d = x.shape
    t = n // b
    w = BRANCH_W
    wq_w = wq_slab.shape[1]
    nt, row, col, state = _row_specs(b, t, tm, layer)
    return pl.pallas_call(
        _fox_inproj_kernel,
        grid=(b, nt),
        in_specs=[row(d), _const_spec((1, d)), _const_spec((d, wq_w)), _const_spec((1, wq_w)),
                  _const_spec((2 * w, d)), _const_spec((FOX_HEADS, d)), _ANY, _ANY],
        out_specs=[row(wq_w), state(w), state(w), col(FOX_HEADS)],
        out_shape=[jax.ShapeDtypeStruct((n, wq_w), BF), jax.ShapeDtypeStruct(kbuf.shape, F32),
                   jax.ShapeDtypeStruct(vbuf.shape, F32), jax.ShapeDtypeStruct((b, FOX_HEADS, t), F32)],
        input_output_aliases={6: 1, 7: 2},
        compiler_params=_cparams("parallel", "parallel"),
    )(x, g, wq_slab, q_const, wkv_t, wf_t, kbuf, vbuf)


def _diff_inproj_kernel(x_ref, g_ref, wq_ref, wk_t_ref, wv_ref, cos_ref, sa_ref, sb_ref, cos_t_ref, sin_t_ref,
                        kbuf_ref, vbuf_ref, q_ref, kt_ref, v_ref):
    del kbuf_ref, vbuf_ref
    kt_ref = kt_ref.at[0]
    tm = x_ref.shape[0]
    xn = _rms(x_ref[...], g_ref[...]).astype(BF)
    w = BRANCH_W
    half = ROT_DIM // 2
    cos, sa, sb = cos_ref[...], sa_ref[...], sb_ref[...]
    yq = _dg(xn, wq_ref[...], _NN)
    for j in range(w // LANES):
        yj = yq[:, j * LANES:(j + 1) * LANES]
        rj = yj * cos + pltpu.roll(yj, LANES - half, 1) * sa + pltpu.roll(yj, half, 1) * sb
        q_ref[:, j * LANES:(j + 1) * LANES] = (rj * QK_SCALE).astype(BF)
    yk = _dg(wk_t_ref[...], xn, _NT)
    cos_t, sin_t = cos_t_ref[...], sin_t_ref[...]
    for gi in range(GROUPS):
        r0 = gi * HEAD_DIM
        y1, y2 = yk[r0:r0 + half, :], yk[r0 + half:r0 + ROT_DIM, :]
        kt_ref[0, r0:r0 + half, :] = y1 * cos_t - y2 * sin_t
        kt_ref[0, r0 + half:r0 + ROT_DIM, :] = y2 * cos_t + y1 * sin_t
        kt_ref[0, r0 + ROT_DIM:r0 + HEAD_DIM, :] = yk[r0 + ROT_DIM:r0 + HEAD_DIM, :]
    yv = _dg(xn, wv_ref[...], _NN)
    dv = 2 * HEAD_DIM
    for h in range(DIFF_HEADS):
        v_ref[0, pl.ds(h, tm, stride=DIFF_HEADS), :] = yv[:, h * dv:(h + 1) * dv]


def _diff_inproj(x, g, wq, wk_t, wv, tabs, kbuf, vbuf, layer, b, tm):
    n, d = x.shape
    t = n // b
    w = BRANCH_W
    nt, row, col, state = _row_specs(b, t, tm, layer)
    row_tab = pl.BlockSpec((tm, LANES), lambda bi, ti: (ti, 0))
    col_tab = pl.BlockSpec((ROT_DIM // 2, tm), lambda bi, ti: (0, ti))
    vspec = pl.BlockSpec((1, DIFF_HEADS * tm, 2 * HEAD_DIM), lambda bi, ti: (layer, bi * nt + ti, 0))
    return pl.pallas_call(
        _diff_inproj_kernel,
        grid=(b, nt),
        in_specs=[row(d), _const_spec((1, d)), _const_spec((d, w)), _const_spec((w, d)), _const_spec((d, w)),
                  row_tab, row_tab, row_tab, col_tab, col_tab, _ANY, _ANY],
        out_specs=[row(w), state(w), vspec],
        out_shape=[jax.ShapeDtypeStruct((n, w), BF), jax.ShapeDtypeStruct(kbuf.shape, F32),
                   jax.ShapeDtypeStruct(vbuf.shape, F32)],
        input_output_aliases={10: 1, 11: 2},
        compiler_params=_cparams("parallel", "parallel"),
    )(x, g, wq, wk_t, wv, *tabs, kbuf, vbuf)


def _rotary_tables(pos):
    half = ROT_DIM // 2
    inv = ROPE_THETA ** (-np.arange(half, dtype=np.float32) / half)
    ang = pos.astype(F32)[:, None] * jnp.asarray(inv, F32)[None, :]
    cos, sin = jnp.cos(ang), jnp.sin(ang)
    t = pos.shape[0]
    one = jnp.ones((t, HEAD_DIM - ROT_DIM), F32)
    zero = jnp.zeros((t, HEAD_DIM - ROT_DIM), F32)
    zh = jnp.zeros((t, half), F32)
    cos_h = jnp.concatenate([cos, cos, one], axis=1)
    sa_h = jnp.concatenate([-sin, zh, zero], axis=1)
    sb_h = jnp.concatenate([zh, sin, zero], axis=1)
    dup = lambda a: jnp.concatenate([a, a], axis=1)
    return dup(cos_h), dup(sa_h), dup(sb_h), cos.T, sin.T


def _rwkv_prep_kernel(x_ref, g_ref, wc_ref, shift0_ref, mu_ref, w0_ref, w2_ref, a0_ref, a2_ref, g2_ref,
                      kks_ref, ka_ref, rk_ref, seg_ref,
                      r_ref, lw_ref, km_ref, v_ref, kk_ref, kka_ref, gg_ref, bonus_ref, shift_ref,
                      carry_ref):
    t = pl.program_id(1)
    tm = x_ref.shape[0]
    w = BRANCH_W
    xn = _rms(x_ref[...], g_ref[...]).astype(BF)
    c = _dg(xn, wc_ref[...], _NN)

    @pl.when(t == 0)
    def _():
        carry_ref[...] = shift0_ref[0]

    row = lax.broadcasted_iota(jnp.int32, (tm, 1), 0)
    prev = jnp.where(row == 0, carry_ref[...], pltpu.roll(c, 1, 0))
    last = c[tm - 1:tm, :]
    carry_ref[...] = last
    shift_ref[0] = last

    xs = c + mu_ref[...] * (prev - c)
    cr, ck, cv = xs[:, 0:w], xs[:, w:2 * w], xs[:, 2 * w:3 * w]
    o = 3 * w
    xw = xs[:, o:o + DECAY_LORA]
    xa = xs[:, o + DECAY_LORA:o + DECAY_LORA + AAA_LORA]
    xg = xs[:, o + DECAY_LORA + AAA_LORA:]
    z = w0_ref[...] + _dot(jnp.tanh(xw), w2_ref[...])
    w_raw = -_softplus(-z) - 0.5
    a = jax.nn.sigmoid(a0_ref[...] + _dot(xa, a2_ref[...]))
    seg = seg_ref[...]
    kk = ck * kks_ref[...]
    kk = kk / jnp.maximum(jnp.sqrt(_dot_sel_right(kk * kk, seg)), 1e-12)
    kmod = ck * (1.0 + (a - 1.0) * ka_ref[...])
    r_ref[...] = cr
    lw_ref[...] = -jnp.exp(w_raw)
    km_ref[...] = kmod
    v_ref[...] = cv
    kk_ref[...] = kk
    kka_ref[...] = kk * a
    gg_ref[...] = _dot(jax.nn.sigmoid(xg), g2_ref[...])
    bonus_ref[...] = _dot_sel_right(cr * kmod * rk_ref[...], seg) * cv


def _rwkv_prep(x, g, wc, shift0, vecs, mats, seg, b, tm):
    n, d = x.shape
    t = n // b
    w = BRANCH_W
    mu, w0, a0, kks, ka, rk = vecs
    w2, a2, g2 = mats
    nt, row, _, _ = _row_specs(b, t, tm, 0)
    per_b = pl.BlockSpec((1, 1, RWKV_IN), lambda bi, ti: (bi, 0, 0))
    outs = [jax.ShapeDtypeStruct((n, w), F32)] * 8 + [jax.ShapeDtypeStruct((b, 1, RWKV_IN), F32)]
    return pl.pallas_call(
        _rwkv_prep_kernel,
        grid=(b, nt),
        in_specs=[row(d), _const_spec((1, d)), _const_spec((d, RWKV_IN)), per_b,
                  _const_spec((1, RWKV_IN)), _const_spec((1, w)), _const_spec((DECAY_LORA, w)),
                  _const_spec((1, w)), _const_spec((AAA_LORA, w)), _const_spec((GATE_LORA, w)),
                  _const_spec((1, w)), _const_spec((1, w)), _const_spec((1, w)), _const_spec((w, w))],
        out_specs=[row(w)] * 8 + [per_b],
        out_shape=outs,
        scratch_shapes=[pltpu.VMEM((1, RWKV_IN), F32)],
        compiler_params=_cparams("parallel", "arbitrary"),
    )(x, g, wc, shift0, mu, w0, w2, a0, a2, g2, kks, ka, rk, seg)


def _logf_cumsum_kernel(z_ref, bias_ref, lf_ref, d_ref, *, past, new):
    z = z_ref[0]
    width = z.shape[1]
    col = lax.broadcasted_iota(jnp.int32, z.shape, 1)
    zz = z + bias_ref[...]
    log_sig = jnp.minimum(zz, 0.0) - jnp.log(1.0 + jnp.exp(-jnp.abs(zz)))
    lf = jnp.where(col < past, z, jnp.where(col < past + new, log_sig, 0.0))
    lf_ref[0] = lf
    x = lf
    s = 1
    while s < width:
        x = x + jnp.where(col >= s, pltpu.roll(x, s, 1), 0.0)
        s *= 2
    d_ref[0] = x


def _logf_cumsum(z, bias, past, new):
    b, h, width = z.shape
    blk = pl.BlockSpec((1, h, width), lambda i: (i, 0, 0))
    return pl.pallas_call(
        functools.partial(_logf_cumsum_kernel, past=past, new=new),
        grid=(b,),
        in_specs=[blk, _const_spec((h, 1))],
        out_specs=[blk, blk],
        out_shape=[jax.ShapeDtypeStruct(z.shape, F32)] * 2,
        compiler_params=_cparams("parallel"),
    )(z, bias)


def _softmax_update(s, m_ref, l_ref, idx):
    m_prev = m_ref[idx]
    m_new = jnp.maximum(m_prev, jnp.max(s, axis=-1, keepdims=True))
    alpha = jnp.exp2(m_prev - m_new)
    p = jnp.exp2(s - m_new)
    l_ref[idx] = alpha * l_ref[idx] + jnp.sum(p, axis=-1, keepdims=True)
    m_ref[idx] = m_new
    return p.astype(BF), alpha


def _init_softmax_state(m_ref, l_ref, acc_ref):
    m_ref[...] = jnp.full(m_ref.shape, NEG_BIG, F32)
    l_ref[...] = jnp.zeros(l_ref.shape, F32)
    acc_ref[...] = jnp.zeros(acc_ref.shape, F32)


def _row_max_update(s, m_ref, idx):
    reps = s.shape[1] // LANES
    m_prev = m_ref[idx]
    m_new = jnp.maximum(m_prev, jnp.max(s, axis=-1, keepdims=True))
    alpha = jnp.exp2(m_prev - m_new)
    p = jnp.exp2(s - jnp.concatenate([m_new] * reps, axis=1))
    m_ref[idx] = m_new
    return p.astype(BF), alpha


def _init_prompt_state(m_ref, acc_ref):
    m_ref[...] = jnp.full(m_ref.shape, NEG_BIG, F32)
    acc_ref[...] = jnp.zeros(acc_ref.shape, F32)


def _causal_tile_pairs(nq):
    pairs = [(i, j) for i in range(nq) for j in range(i + 1)]
    return (jnp.asarray([p[0] for p in pairs], jnp.int32), jnp.asarray([p[1] for p in pairs], jnp.int32))


def _causal_blocks(i, j, block):
    pl.when(j < i)(functools.partial(block, False))
    pl.when(j == i)(functools.partial(block, True))


def _fox_prompt_kernel(it_ref, jt_ref, q_ref, kt_ref, vt_ref, dk_ref, o_ref, m_ref, acc_ref):
    pair = pl.program_id(1)
    i, j = it_ref[pair], jt_ref[pair]
    tq, tk = q_ref.shape[0], kt_ref.shape[3]
    heads = [slice(h * HEAD_DIM, (h + 1) * HEAD_DIM) for h in range(FOX_HEADS)]

    @pl.when(j == 0)
    def _():
        _init_prompt_state(m_ref, acc_ref)

    def block(masked):
        kb = kt_ref[0, 0].astype(BF)
        vb = vt_ref[0, 0].astype(BF)
        ones = jnp.ones((HEAD_DIM, tk), BF)
        d_hi, d_mid, d_lo = _split3(dk_ref[0] * LOG2E)
        rows = 2 * SUBLANES
        row = lax.broadcasted_iota(jnp.int32, (rows, tk), 0)
        pad = jnp.zeros((LANES - HEAD_DIM - rows, tk), BF)
        k_ops = []
        for h, hs in enumerate(heads):
            pieces = [jnp.broadcast_to(a[h:h + 1, :].astype(F32), (rows, tk)) for a in (d_hi, d_mid, d_lo)]
            extra = jnp.where(row == 0, pieces[0], jnp.where(row == 1, pieces[1], jnp.where(row == 2, pieces[2], 0.0)))
            k_ops.append(jnp.concatenate([kb[hs, :], extra.astype(BF), pad], axis=0))
        scores = [_dg(q_ref[:, h * LANES:(h + 1) * LANES], k_ops[h], _NN) for h in range(FOX_HEADS)]
        if masked:
            visible = (lax.broadcasted_iota(jnp.int32, (tq, tk), 1) <= lax.broadcasted_iota(jnp.int32, (tq, tk), 0))
        for h, hs in enumerate(heads):
            s = jnp.where(visible, scores[h], NEG_BIG) if masked else scores[h]
            p, alpha = _row_max_update(s, m_ref, h)
            v_ones = jnp.concatenate([vb[hs, :], ones], axis=0)
            acc_ref[h] = alpha * acc_ref[h] + _dg(p, v_ones, _NT)

    _causal_blocks(i, j, block)

    @pl.when(j == i)
    def _():
        for h, hs in enumerate(heads):
            acc = acc_ref[h]
            o_ref[:, hs] = acc[:, 0:HEAD_DIM] / acc[:, HEAD_DIM:2 * HEAD_DIM]


def _fox_prompt(q, kt, vt, d_row, layer, b, tq):
    n, wq = q.shape
    w = BRANCH_W
    t = n // b
    nq = t // tq
    i_tab, j_tab = _causal_tile_pairs(nq)
    qspec = lambda width: pl.BlockSpec((tq, width), lambda bi, p, it, jt: (bi * nq + it[p], 0))
    kspec = pl.BlockSpec((1, 1, w, tq), lambda bi, p, it, jt: (layer, bi, 0, jt[p]))
    grid_spec = pltpu.PrefetchScalarGridSpec(
        num_scalar_prefetch=2,
        grid=(b, i_tab.shape[0]),
        in_specs=[qspec(wq), kspec, kspec,
                  pl.BlockSpec((1, FOX_HEADS, tq), lambda bi, p, it, jt: (bi, 0, jt[p]))],
        out_specs=qspec(w),
        scratch_shapes=[pltpu.VMEM((FOX_HEADS, tq, LANES), F32), pltpu.VMEM((FOX_HEADS, tq, 2 * HEAD_DIM), F32)],
    )
    return pl.pallas_call(
        _fox_prompt_kernel,
        grid_spec=grid_spec,
        out_shape=jax.ShapeDtypeStruct((n, w), F32),
        compiler_params=_cparams("parallel", "arbitrary"),
    )(i_tab, j_tab, q, kt, vt, d_row)


def _diff_lambda(lam_ref, lam_init):
    p = lam_ref[...]
    s1 = jnp.sum(p[0:1, :] * p[1:2, :], axis=-1, keepdims=True)
    s2 = jnp.sum(p[2:3, :] * p[3:4, :], axis=-1, keepdims=True)
    return jnp.exp(s1) - jnp.exp(s2) + lam_init


def _diff_combine(acc0, l0, acc1, l1, lam, subln, lam_init):
    o = acc0 / l0 - lam * (acc1 / l1)
    return _rms(o, subln) * (1.0 - lam_init)


def _diff_prompt_kernel(it_ref, jt_ref, q_ref, kt_ref, v_ref, lam_ref, subln_ref, o_ref, m_ref, acc_ref, *,
                        lam_init):
    pair = pl.program_id(1)
    i, j = it_ref[pair], jt_ref[pair]
    tq, tk = q_ref.shape[0], kt_ref.shape[3]
    dv = 2 * HEAD_DIM

    @pl.when(j == 0)
    def _():
        _init_prompt_state(m_ref, acc_ref)

    def block(masked):
        kb = kt_ref[0, 0].astype(BF)
        ones = jnp.ones((tk, dv), BF)
        groups = [slice(gi * HEAD_DIM, (gi + 1) * HEAD_DIM) for gi in range(GROUPS)]
        scores = [_dg(q_ref[:, gs], kb[gs, :], _NN) for gs in groups]
        if masked:
            visible = ((lax.broadcasted_iota(jnp.int32, (tq, tk), 1) // CHUNK)
                       <= (lax.broadcasted_iota(jnp.int32, (tq, tk), 0) // CHUNK))
        for h in range(DIFF_HEADS):
            vh = v_ref[0, pl.ds(h, tk, stride=DIFF_HEADS), :].astype(BF)
            v_ones = jnp.concatenate([vh, ones], axis=1)
            for gi in (2 * h, 2 * h + 1):
                s = jnp.where(visible, scores[gi], NEG_BIG) if masked else scores[gi]
                p, alpha = _row_max_update(s, m_ref, gi)
                acc_ref[gi] = jnp.concatenate([alpha, alpha], axis=1) * acc_ref[gi] + _dg(p, v_ones, _NN)

    _causal_blocks(i, j, block)

    @pl.when(j == i)
    def _():
        lam = _diff_lambda(lam_ref, lam_init)
        for h in range(DIFF_HEADS):
            a0, a1 = acc_ref[2 * h], acc_ref[2 * h + 1]
            o_ref[:, h * dv:(h + 1) * dv] = _diff_combine(a0[:, 0:dv], a0[:, dv:], a1[:, 0:dv], a1[:, dv:], lam,
                                                          subln_ref[...], lam_init)


def _diff_prompt(q, kt, v, lam_p, subln, lam_init, layer, b, tq):
    n, w = q.shape
    t = n // b
    nq = t // tq
    i_tab, j_tab = _causal_tile_pairs(nq)
    qspec = pl.BlockSpec((tq, w), lambda bi, p, it, jt: (bi * nq + it[p], 0))
    ktspec = pl.BlockSpec((1, 1, w, tq), lambda bi, p, it, jt: (layer, bi, 0, jt[p]))
    vspec = pl.BlockSpec((1, DIFF_HEADS * tq, 2 * HEAD_DIM), lambda bi, p, it, jt: (layer, bi * nq + jt[p], 0))
    grid_spec = pltpu.PrefetchScalarGridSpec(
        num_scalar_prefetch=2,
        grid=(b, i_tab.shape[0]),
        in_specs=[qspec, ktspec, vspec, _const_spec((4, HEAD_DIM)), _const_spec((1, 2 * HEAD_DIM))],
        out_specs=qspec,
        scratch_shapes=[pltpu.VMEM((GROUPS, tq, LANES), F32), pltpu.VMEM((GROUPS, tq, 4 * HEAD_DIM), F32)],
    )
    return pl.pallas_call(
        functools.partial(_diff_prompt_kernel, lam_init=lam_init),
        grid_spec=grid_spec,
        out_shape=jax.ShapeDtypeStruct((n, w), F32),
        compiler_params=_cparams("parallel", "arbitrary"),
    )(i_tab, j_tab, q, kt, v, lam_p, subln)


def _sample_attn_kernel(*refs, fox, t_new, lam_init):
    if fox:
        q_ref, kc_ref, vc_ref, kn_ref, vn_ref, dq_ref, dkc_ref, dkn_ref, o_ref, m_ref, l_ref, acc_ref = refs
    else:
        q_ref, kc_ref, vc_ref, kn_ref, vn_ref, lam_ref, subln_ref, o_ref, m_ref, l_ref, acc_ref = refs
    j = pl.program_id(1)
    dv = 2 * HEAD_DIM
    pair = 2 * t_new

    @pl.when(j == 0)
    def _():
        _init_softmax_state(m_ref, l_ref, acc_ref)

    def expand(dk):
        return jnp.concatenate([jnp.broadcast_to(dk[gi:gi + 1, :], (t_new, dk.shape[1])) for gi in range(GROUPS)],
                               axis=0)

    def scores(kt, dk, causal):
        s = _dg(q_ref[0], kt.astype(BF), _NN)
        if fox:
            s = s + dq_ref[0] * LOG2E - expand(dk * LOG2E)
        if causal:
            qi = lax.broadcasted_iota(jnp.int32, s.shape, 0) % t_new
            kj = lax.broadcasted_iota(jnp.int32, s.shape, 1)
            s = jnp.where(kj <= qi, s, NEG_BIG)
        return _softmax_update(s, m_ref, l_ref, 0)

    def accumulate_diff(p, alpha, value_of_head):
        for h in range(DIFF_HEADS):
            rs = slice(h * pair, (h + 1) * pair)
            acc_ref[0, rs, :] = alpha[rs] * acc_ref[0, rs, :] + _dg(p[rs], value_of_head(h).astype(BF), _NN)

    tk = kc_ref.shape[3]
    if fox:
        p, alpha = scores(kc_ref[0, 0], dkc_ref[0], False)
        acc_ref[0] = alpha * acc_ref[0] + _dg(p, vc_ref[0, 0].astype(BF), _NT)
    else:
        p, alpha = scores(kc_ref[0, 0], None, False)
        accumulate_diff(p, alpha, lambda h: vc_ref[0, 0, pl.ds(h, tk, stride=DIFF_HEADS), :])

    @pl.when(j == pl.num_programs(1) - 1)
    def _():
        if fox:
            p, alpha = scores(kn_ref[0, 0], dkn_ref[0][:, 0:t_new], True)
            acc = alpha * acc_ref[0] + _dg(p, vn_ref[0, 0].astype(BF), _NT)
            l = l_ref[0]
            for gi in range(GROUPS):
                rs = slice(gi * t_new, (gi + 1) * t_new)
                cs = slice(gi * HEAD_DIM, (gi + 1) * HEAD_DIM)
                o_ref[:, cs] = acc[rs, cs] / l[rs]
        else:
            p, alpha = scores(kn_ref[0, 0], None, False)
            accumulate_diff(p, alpha, lambda h: vn_ref[0, pl.ds(h, t_new, stride=DIFF_HEADS), :])
            acc = acc_ref[0]
            l = l_ref[0]
            lam = _diff_lambda(lam_ref, lam_init)
            for h in range(DIFF_HEADS):
                r0 = slice(h * pair, h * pair + t_new)
                r1 = slice(h * pair + t_new, (h + 1) * pair)
                o_ref[:, h * dv:(h + 1) * dv] = _diff_combine(acc[r0], l[r0], acc[r1], l[r1], lam, subln_ref[...],
                                                              lam_init)


def _sample_attn(qbd, kc, vc, kn, vn, extra, *, layer, fox, b, t_new, past, tk, lam_init=0.0):
    w = BRANCH_W
    rows = GROUPS * t_new
    nk = past // tk
    qspec = pl.BlockSpec((1, rows, w), lambda bi, j: (bi, 0, 0))
    kcspec = pl.BlockSpec((1, 1, w, tk), lambda bi, j: (layer, bi, 0, j))
    knspec = pl.BlockSpec((1, 1, w, t_new), lambda bi, j: (layer, bi, 0, 0))
    ospec = pl.BlockSpec((t_new, w), lambda bi, j: (bi, 0))
    if fox:
        dq, d_row = extra
        vcspec, vnspec, acc_w = kcspec, knspec, w
        especs = [pl.BlockSpec((1, rows, 1), lambda bi, j: (bi, 0, 0)),
                  pl.BlockSpec((1, GROUPS, tk), lambda bi, j: (bi, 0, j)),
                  pl.BlockSpec((1, GROUPS, LANES), lambda bi, j: (bi, 0, past // LANES))]
        eargs = [dq, d_row, d_row]
    else:
        vcspec = pl.BlockSpec((1, 1, DIFF_HEADS * tk, 2 * HEAD_DIM), lambda bi, j: (layer, bi, j, 0))
        vnspec = pl.BlockSpec((1, DIFF_HEADS * t_new, 2 * HEAD_DIM), lambda bi, j: (layer, bi, 0))
        acc_w = 2 * HEAD_DIM
        especs = [_const_spec((4, HEAD_DIM)), _const_spec((1, 2 * HEAD_DIM))]
        eargs = list(extra)
    return pl.pallas_call(
        functools.partial(_sample_attn_kernel, fox=fox, t_new=t_new, lam_init=lam_init),
        grid=(b, nk),
        in_specs=[qspec, kcspec, vcspec, knspec, vnspec] + especs,
        out_specs=ospec,
        out_shape=jax.ShapeDtypeStruct((b * t_new, w), F32),
        scratch_shapes=[pltpu.VMEM((1, rows, 1), F32), pltpu.VMEM((1, rows, 1), F32),
                        pltpu.VMEM((1, rows, acc_w), F32)],
        compiler_params=_cparams("parallel", "arbitrary"),
    )(qbd, kc, vc, kn, vn, *eargs)


def _block_diag_queries(q, b, t_new):
    q4 = q.reshape(b, t_new, GROUPS, HEAD_DIM)
    eye = jnp.eye(GROUPS, dtype=q.dtype)
    qbd = q4.transpose(0, 2, 1, 3)[:, :, :, None, :] * eye[None, :, None, :, None]
    return qbd.reshape(b, GROUPS * t_new, GROUPS * HEAD_DIM)


def _rwkv_scan_kernel(r_ref, lw_ref, k_ref, v_ref, kk_ref, kka_ref, gg_ref, bonus_ref, gain_ref, bias_ref,
                      s0_ref, o_ref, sfin_ref, s_ref):
    c = pl.program_id(1)
    nb, ch = r_ref.shape[0], r_ref.shape[1]
    n = HEAD_DIM

    @pl.when(c == 0)
    def _():
        s_ref[...] = s0_ref[...]

    ri = lax.broadcasted_iota(jnp.int32, (ch, ch), 0)
    ci = lax.broadcasted_iota(jnp.int32, (ch, ch), 1)
    tri = (ci <= ri).astype(BF)
    ri2 = lax.broadcasted_iota(jnp.int32, (ch, 2 * ch), 0)
    ci2 = lax.broadcasted_iota(jnp.int32, (ch, 2 * ch), 1)
    ci2m = jnp.where(ci2 >= ch, ci2 - ch, ci2)
    strict_right = (ci2 >= ch) & (ci2m < ri2)
    strict_left = ci < ri
    is_x = lax.broadcasted_iota(jnp.int32, (ch, ch + n), 1) >= ch
    incl_both = ci2m <= ri2
    steps = max(1, int(math.ceil(math.log2(ch))))
    heads = [slice(h * n, (h + 1) * n) for h in range(RWKV_HEADS)]

    prep = []
    for bi in range(nb):
        lw = lw_ref[bi]
        cum = _dot_sel(tri, lw)
        cum_end = cum[ch - 1:ch, :]
        kk, kka, kmod, v_all = kk_ref[bi], kka_ref[bi], k_ref[bi], v_ref[bi]
        e_neg = jnp.exp(-cum)
        e_tail = jnp.exp(cum_end - cum)
        prep.append(dict(
            ar=_split2(jnp.concatenate([-kk * jnp.exp(cum - lw), r_ref[bi] * jnp.exp(cum)], axis=0)),
            bk=_split2(jnp.concatenate([kka * e_neg, kmod * e_neg], axis=0)),
            bk_tail=_split2(jnp.concatenate([kka * e_tail, kmod * e_tail], axis=0)),
            vv=_split2(jnp.concatenate([v_all, v_all], axis=0)),
            v=v_all, decay_end=jnp.exp(cum_end)))
    units = [(bi, h, heads[h]) for bi in range(nb) for h in range(RWKV_HEADS)]
    s_old = [s_ref[bi, h] for bi, h, _ in units]

    def dot3s(a2, b2, dims):
        (ah, al), (bh, bl) = a2, b2
        m = ah.shape[0]
        r = _dg(jnp.concatenate([ah, al], axis=0), bh, dims)
        return r[0:m] + r[m:] + _dg(ah, bl, dims)

    def dot3_tn(a2, b2):
        (ah, al), (bh, bl) = a2, b2
        return _dg(ah, bh, _TN) + (_dg(ah, bl, _TN) + _dg(al, bh, _TN))

    def nil_times(zh):
        a2, b2 = _split2(zh[:, 0:ch]), _split2(zh)
        wz = zh.shape[1]
        if wz % LANES:
            return dot3s(a2, b2, _NN)
        r = _dg(jnp.concatenate(a2, axis=0), jnp.concatenate(b2, axis=1), _NN)
        return (r[0:ch, 0:wz] + r[ch:, 0:wz]) + (r[0:ch, wz:] + r[ch:, wz:])

    cols = lambda a2, hs: (a2[0][:, hs], a2[1][:, hs])
    ga = []
    for u, (bi, h, hs) in enumerate(units):
        sh, sl = _split2(s_old[u])
        bk = prep[bi]["bk"]
        rhs = (jnp.concatenate([bk[0][:, hs], sh], axis=0), jnp.concatenate([bk[1][:, hs], sl], axis=0))
        ga.append(dot3s(cols(prep[bi]["ar"], hs), rhs, _NT))
    z = []
    for u, (bi, h, hs) in enumerate(units):
        g_top = ga[u][0:ch, 0:2 * ch]
        x0 = ga[u][0:ch, 2 * ch:] + dot3s(_split2(jnp.where(strict_right, g_top, 0.0)), cols(prep[bi]["vv"], hs), _NN)
        z.append(jnp.concatenate([jnp.where(strict_left, g_top[:, 0:ch], 0.0), x0], axis=1))
    for _ in range(steps):
        z = [nil_times(zh) + jnp.where(is_x, zh, 0.0) for zh in z]
    for u, (bi, h, hs) in enumerate(units):
        x = z[u][:, ch:]
        xv = _split2(jnp.concatenate([x, prep[bi]["v"][:, hs]], axis=0))
        y = ga[u][ch:, 2 * ch:] + dot3s(_split2(jnp.where(incl_both, ga[u][ch:, 0:2 * ch], 0.0)), xv, _NN)
        s_ref[bi, h] = s_old[u] * prep[bi]["decay_end"][:, hs] + dot3_tn(xv, cols(prep[bi]["bk_tail"], hs))

        mu = jnp.mean(y, axis=-1, keepdims=True)
        yc = y - mu
        var = jnp.mean(yc * yc, axis=-1, keepdims=True)
        yn = yc * lax.rsqrt(var + RWKV_GN_EPS) * gain_ref[:, hs] + bias_ref[:, hs]
        o_ref[bi, :, hs] = (yn + bonus_ref[bi, :, hs]) * gg_ref[bi, :, hs]

    @pl.when(c == pl.num_programs(1) - 1)
    def _():
        sfin_ref[...] = s_ref[...]


def _rwkv_scan(r, lw, km, v, kk, kka, gg, bonus, gain, bias, s0, b, ch, nb):
    n, w = r.shape
    t = n // b
    rows = [a.reshape(b, t, w) for a in (r, lw, km, v, kk, kka, gg, bonus)]
    row = pl.BlockSpec((nb, ch, w), lambda bi, ci: (bi, ci, 0))
    st = pl.BlockSpec((nb, RWKV_HEADS, HEAD_DIM, HEAD_DIM), lambda bi, ci: (bi, 0, 0, 0))
    out, s_fin = pl.pallas_call(
        _rwkv_scan_kernel,
        grid=(b // nb, t // ch),
        in_specs=[row] * 8 + [_const_spec((1, w)), _const_spec((1, w)), st],
        out_specs=[row, st],
        out_shape=[jax.ShapeDtypeStruct((b, t, w), F32), jax.ShapeDtypeStruct(s0.shape, F32)],
        scratch_shapes=[pltpu.VMEM((nb, RWKV_HEADS, HEAD_DIM, HEAD_DIM), F32)],
        compiler_params=_cparams("parallel", "arbitrary"),
    )(*rows, gain, bias, s0)
    return out.reshape(n, w), s_fin


def _merge_kernel(x_ref, oa_ref, ob_ref, oc_ref, gpre_ref, wg_ref, wb_ref, wo_ref, gpost_ref, y_ref):
    x = x_ref[...]
    d = x.shape[1]
    xn = _rms(x, gpre_ref[...]).astype(BF)
    acc = jnp.zeros(x.shape, F32)
    for i, o_ref in enumerate((oa_ref, ob_ref, oc_ref)):
        gate = jax.nn.sigmoid(_dg(xn, wg_ref[:, i * d:(i + 1) * d], _NN))
        acc = acc + gate * _dot(o_ref[...], wb_ref[i])
    y = _dot(acc, wo_ref[...])
    y_ref[...] = x + _rms(y, gpost_ref[...])


def _merge(x, oa, ob, oc, gpre, wg, wb, wo, gpost, tm):
    n, d = x.shape
    w = BRANCH_W
    row = lambda width: pl.BlockSpec((tm, width), lambda i: (i, 0))
    return pl.pallas_call(
        _merge_kernel,
        grid=(n // tm,),
        in_specs=[row(d), row(w), row(w), row(w), _const_spec((1, d)), _const_spec((d, 3 * d)),
                  _const_spec((3, w, d)), _const_spec((d, d)), _const_spec((1, d))],
        out_specs=row(d),
        out_shape=jax.ShapeDtypeStruct((n, d), F32),
        compiler_params=_cparams("parallel"),
    )(x, oa, ob, oc, gpre, wg, wb, wo, gpost)


def _memkv_kernel(m_ref, g_ref, w_ref, k_ref, v_ref):
    xn = _rms(m_ref[...], g_ref[...]).astype(BF)
    tm = m_ref.shape[0]
    hd = XATTN_HEAD_DIM
    w = XATTN_HEADS * hd
    yk = _dg(xn, w_ref[:, 0:w], _NN)
    yv = _dg(xn, w_ref[:, w:2 * w], _NN)
    for h in range(XATTN_HEADS):
        k_ref[pl.ds(h, tm, stride=XATTN_HEADS), :] = yk[:, h * hd:(h + 1) * hd]
        v_ref[pl.ds(h, tm, stride=XATTN_HEADS), :] = yv[:, h * hd:(h + 1) * hd]


def _memkv(mem, g, w_xkv, tm):
    n, d = mem.shape
    w = w_xkv.shape[1] // 2
    out = pl.BlockSpec((XATTN_HEADS * tm, XATTN_HEAD_DIM), lambda i: (i, 0))
    return pl.pallas_call(
        _memkv_kernel,
        grid=(n // tm,),
        in_specs=[pl.BlockSpec((tm, d), lambda i: (i, 0)), _const_spec((1, d)), _const_spec((d, 2 * w))],
        out_specs=[out, out],
        out_shape=[jax.ShapeDtypeStruct((XATTN_HEADS * n, XATTN_HEAD_DIM), F32)] * 2,
        compiler_params=_cparams("parallel"),
    )(mem, g, w_xkv)


def _xattn_kernel(x_ref, gpre_ref, wq_ref, mk_ref, mv_ref, wo_ref, gpost_ref, y_ref):
    x = x_ref[...]
    xn = _rms(x, gpre_ref[...])
    q = _dot(xn, wq_ref[...])
    hd = XATTN_HEAD_DIM
    n_mem = mk_ref.shape[2] // XATTN_HEADS
    outs = []
    for h in range(XATTN_HEADS):
        hs = slice(h * hd, (h + 1) * hd)
        mem_rows = pl.ds(h, n_mem, stride=XATTN_HEADS)
        s = _dot_nt(q[:, hs], mk_ref[0, 0, mem_rows, :]) * (hd ** -0.5)
        p = jnp.exp(s - jnp.max(s, axis=-1, keepdims=True))
        p = p / jnp.sum(p, axis=-1, keepdims=True)
        outs.append(_dot(p, mv_ref[0, 0, mem_rows, :]))
    o = jnp.concatenate(outs, axis=-1)
    y_ref[...] = x + _rms(_dot(o, wo_ref[...]), gpost_ref[...])


def _xattn(x, gpre, wq, mk, mv, wo, gpost, layer, b, tm):
    n, d = x.shape
    t = n // b
    nt = t // tm
    w = wq.shape[1]
    row = pl.BlockSpec((tm, d), lambda bi, ti: (bi * nt + ti, 0))
    mem = pl.BlockSpec((1, 1) + mk.shape[2:], lambda bi, ti: (layer, bi, 0, 0))
    return pl.pallas_call(
        _xattn_kernel,
        grid=(b, nt),
        in_specs=[row, _const_spec((1, d)), _const_spec((d, w)), mem, mem, _const_spec((w, d)),
                  _const_spec((1, d))],
        out_specs=row,
        out_shape=jax.ShapeDtypeStruct((n, d), F32),
        compiler_params=_cparams("parallel", "parallel"),
    )(x, gpre, wq, mk, mv, wo, gpost)


def _ffn_kernel(x_ref, gpre_ref, wu_ref, wv_ref, wo_ref, gpost_ref, y_ref, xn_ref, acc_ref):
    j = pl.program_id(1)

    @pl.when(j == 0)
    def _():
        xn_ref[...] = _rms(x_ref[...], gpre_ref[...]).astype(BF)
        acc_ref[...] = jnp.zeros(acc_ref.shape, F32)

    xn = xn_ref[...]
    u = _dg(xn, wu_ref[...], _NN)
    v = _dg(xn, wv_ref[...], _NN)
    acc_ref[...] += _dot(u * jax.nn.sigmoid(u) * v, wo_ref[...])

    @pl.when(j == pl.num_programs(1) - 1)
    def _():
        y_ref[...] = x_ref[...] + _rms(acc_ref[...], gpost_ref[...])


def _ffn(x, gpre, w_in, w_out, gpost, tm, th):
    n, d = x.shape
    hidden = w_out.shape[0]
    nh = hidden // th
    row = pl.BlockSpec((tm, d), lambda i, j: (i, 0))
    return pl.pallas_call(
        _ffn_kernel,
        grid=(n // tm, nh),
        in_specs=[row, _const_spec((1, d)),
                  pl.BlockSpec((d, th), lambda i, j: (0, j)),
                  pl.BlockSpec((d, th), lambda i, j: (0, nh + j)),
                  pl.BlockSpec((th, d), lambda i, j: (j, 0)),
                  _const_spec((1, d))],
        out_specs=row,
        out_shape=jax.ShapeDtypeStruct((n, d), F32),
        scratch_shapes=[pltpu.VMEM((tm, d), BF), pltpu.VMEM((tm, d), F32)],
        compiler_params=_cparams("parallel", "arbitrary"),
    )(x, gpre, w_in, w_in, w_out, gpost)


def _tile(n, pref):
    t = min(n, pref)
    assert n % t == 0, (n, t)
    return t


def _layer_weights(l, wts):
    p = {k: v[l] for k, v in wts.items()}
    w = BRANCH_W
    w_in = p["w_in"].astype(BF)
    w_in_t = p["w_in"].T.astype(BF)
    gate_w = w_in.shape[1] - (6 * w + FOX_HEADS + RWKV_IN)
    o_fox, o_f, o_diff = 0, 3 * w, 3 * w + FOX_HEADS
    o_rwkv = o_diff + 3 * w
    o_gate = o_rwkv + RWKV_IN
    row = lambda a: a.reshape(1, -1)
    seg = np.kron(np.eye(RWKV_HEADS, dtype=np.float32), np.ones((HEAD_DIM, HEAD_DIM), np.float32))
    return dict(
        g_mix_pre=row(p["norm_mix_pre"]), g_mix_post=row(p["norm_mix_post"]),
        w_fox_q=_fox_q_layout(w_in[:, o_fox:o_fox + w]), w_fox_kv_t=w_in_t[o_fox + w:o_fox + 3 * w],
        w_fox_f_t=w_in_t[o_f:o_f + FOX_HEADS],
        w_diff_q=w_in[:, o_diff:o_diff + w], w_diff_k_t=w_in_t[o_diff + w:o_diff + 2 * w],
        w_diff_v=w_in[:, o_diff + 2 * w:o_diff + 3 * w],
        w_rwkv=w_in[:, o_rwkv:o_rwkv + RWKV_IN], w_gate=w_in[:, o_gate:o_gate + gate_w],
        fox_bias=p["fox_forget_bias"].reshape(FOX_HEADS, 1),
        diff_lambda=p["diff_lambda"], diff_subln=row(p["diff_subln"]),
        rwkv_vecs=tuple(row(p[k]) for k in ("rwkv_mu", "rwkv_w0", "rwkv_a0", "rwkv_kk_scale", "rwkv_ka",
                                            "rwkv_rk")),
        rwkv_mats=tuple(p[k].astype(BF) for k in ("rwkv_w2", "rwkv_a2", "rwkv_g2")),
        rwkv_seg=jnp.asarray(seg, BF),
        rwkv_gain=row(p["rwkv_ln_gain"]), rwkv_bias=row(p["rwkv_ln_bias"]),
        w_branch=p["w_branch"].astype(BF), w_out=p["w_out"].astype(BF),
        g_x_pre=row(p["norm_x_pre"]), g_x_post=row(p["norm_x_post"]), g_mem=row(p["norm_mem"]),
        w_xq=p["w_xq"].astype(BF), w_xkv=p["w_xkv"].astype(BF), w_xo=p["w_xo"].astype(BF),
        g_ffn_pre=row(p["norm_ffn_pre"]), g_ffn_post=row(p["norm_ffn_post"]),
        w_ffn_in=p["w_ffn_in"].astype(BF), w_ffn_out=p["w_ffn_out"].astype(BF),
    )


def _new_state_buffers(depth, b, t):
    kt = lambda: jnp.zeros((depth, b, BRANCH_W, t), F32)
    return dict(kf=kt(), vf=kt(), kd=kt(), vd=jnp.zeros((depth, b * t * DIFF_HEADS, 2 * HEAD_DIM), F32))


def _mixer(x, lw, l, b, t, past, tabs, bufs):
    n = x.shape[0]
    lam_init = 0.8 - 0.6 * math.exp(-0.3 * l)
    tm = _tile(t, 512)
    qf, kf_t, vf_t, f_bt = _fox_inproj(x, lw["g_mix_pre"], *lw["w_fox_q"], lw["w_fox_kv_t"], lw["w_fox_f_t"],
                                       bufs["kf"], bufs["vf"], l, b, tm)
    qd, kd_t, vd = _diff_inproj(x, lw["g_mix_pre"], lw["w_diff_q"], lw["w_diff_k_t"], lw["w_diff_v"], tabs,
                                bufs["kd"], bufs["vd"], l, b, tm)
    bufs = dict(kf=kf_t, vf=vf_t, kd=kd_t, vd=vd)

    if past is None:
        p_len = 0
        z = f_bt
        shift0 = jnp.zeros((b, 1, RWKV_IN), F32)
        s0 = jnp.zeros((b, RWKV_HEADS, HEAD_DIM, HEAD_DIM), F32)
    else:
        fox_kt, fox_vt, fox_lf_t, diff_kt, diff_v, s0, shift0 = past
        p_len = fox_kt.shape[3]
        pad = (-(p_len + t)) % LANES
        z = jnp.concatenate([fox_lf_t[l], f_bt, jnp.zeros((b, FOX_HEADS, pad), F32)], axis=2)
        s0 = s0[l]
        shift0 = shift0[l].reshape(b, 1, RWKV_IN)
    lf, d_row = _logf_cumsum(z, lw["fox_bias"], p_len, t)
    log_f = lf[:, :, p_len:p_len + t].transpose(0, 2, 1)

    if past is None:
        tq = _tile(t, 512)
        out_a = _fox_prompt(qf, kf_t, vf_t, d_row, l, b, tq)
        out_b = _diff_prompt(qd, kd_t, vd, lw["diff_lambda"], lw["diff_subln"], lam_init, l, b, tq)
    else:
        assert p_len % CHUNK == 0 and t <= CHUNK and p_len % LANES == 0
        tk = _tile(p_len, 2048)
        dq = d_row[:, :, p_len:p_len + t].reshape(b, GROUPS * t, 1)
        qf_feat = qf.reshape(n, FOX_HEADS, LANES)[:, :, 0:HEAD_DIM].reshape(n, BRANCH_W)
        out_a = _sample_attn(_block_diag_queries(qf_feat, b, t), fox_kt, fox_vt, kf_t, vf_t, (dq, d_row),
                             layer=l, fox=True, b=b, t_new=t, past=p_len, tk=tk)
        out_b = _sample_attn(_block_diag_queries(qd, b, t), diff_kt, diff_v, kd_t, vd,
                             (lw["diff_lambda"], lw["diff_subln"]),
                             layer=l, fox=False, b=b, t_new=t, past=p_len, tk=tk, lam_init=lam_init)

    r, lgw, km, v, kk, kka, gg, bonus, shift = _rwkv_prep(
        x, lw["g_mix_pre"], lw["w_rwkv"], shift0, lw["rwkv_vecs"], lw["rwkv_mats"], lw["rwkv_seg"], b, tm)
    ch = _tile(t, CHUNK)
    out_c, s_new = _rwkv_scan(r, lgw, km, v, kk, kka, gg, bonus, lw["rwkv_gain"], lw["rwkv_bias"], s0, b, ch,
                              _tile(b, 4))

    y = _merge(x, out_a, out_b, out_c, lw["g_mix_pre"], lw["w_gate"], lw["w_branch"], lw["w_out"],
               lw["g_mix_post"], _tile(n, 512))
    state = (log_f, s_new, shift.reshape(b, RWKV_IN))
    return y, state, bufs


def _layer(x, lw, l, b, t, past, mem_k, mem_v, mem_layer, tabs, bufs):
    n = x.shape[0]
    x, state, bufs = _mixer(x, lw, l, b, t, past, tabs, bufs)
    x = _xattn(x, lw["g_x_pre"], lw["w_xq"], mem_k, mem_v, lw["w_xo"], lw["g_x_post"], mem_layer, b,
               _tile(t, 512))
    hidden = lw["w_ffn_out"].shape[0]
    th = hidden // 2 if (hidden // 2) % LANES == 0 else 2 * LANES
    x = _ffn(x, lw["g_ffn_pre"], lw["w_ffn_in"], lw["w_ffn_out"], lw["g_ffn_post"], _tile(n, 512), th)
    return x, state, bufs


def _assemble_states(states, bufs, b, t):
    log_f, s_new, shift = (jnp.stack(e) for e in zip(*states))
    depth = bufs["kf"].shape[0]
    tok_major = lambda a: a.reshape(depth, b, FOX_HEADS, HEAD_DIM, t).transpose(0, 1, 4, 2, 3)
    fox_k, fox_v = tok_major(bufs["kf"]), tok_major(bufs["vf"])
    diff_k = tok_major(bufs["kd"]).reshape(depth, b, t, DIFF_HEADS, 2, HEAD_DIM)
    diff_v = bufs["vd"].reshape(depth, b, t, DIFF_HEADS, 2 * HEAD_DIM)
    return fox_k, fox_v, log_f, diff_k, diff_v, s_new, shift


def kernel(x_prompt, x_sample, mem_prompt, cache_fox_k, cache_fox_v, cache_fox_logf, cache_diff_k, cache_diff_v, state_rwkv, state_rwkv_shift, cache_mem_k, cache_mem_v, norm_mix_pre, norm_mix_post, w_in, fox_forget_bias, diff_lambda, diff_subln, rwkv_mu, rwkv_w0, rwkv_w2, rwkv_a0, rwkv_a2, rwkv_g2, rwkv_kk_scale, rwkv_ka, rwkv_rk, rwkv_ln_gain, rwkv_ln_bias, w_branch, w_out, norm_x_pre, norm_x_post, norm_mem, w_xq, w_xkv, w_xo, norm_ffn_pre, norm_ffn_post, w_ffn_in, w_ffn_out):
    wts = dict(norm_mix_pre=norm_mix_pre, norm_mix_post=norm_mix_post, w_in=w_in, fox_forget_bias=fox_forget_bias,
               diff_lambda=diff_lambda, diff_subln=diff_subln, rwkv_mu=rwkv_mu, rwkv_w0=rwkv_w0, rwkv_w2=rwkv_w2,
               rwkv_a0=rwkv_a0, rwkv_a2=rwkv_a2, rwkv_g2=rwkv_g2, rwkv_kk_scale=rwkv_kk_scale, rwkv_ka=rwkv_ka,
               rwkv_rk=rwkv_rk, rwkv_ln_gain=rwkv_ln_gain, rwkv_ln_bias=rwkv_ln_bias, w_branch=w_branch,
               w_out=w_out, norm_x_pre=norm_x_pre, norm_x_post=norm_x_post, norm_mem=norm_mem, w_xq=w_xq,
               w_xkv=w_xkv, w_xo=w_xo, norm_ffn_pre=norm_ffn_pre, norm_ffn_post=norm_ffn_post,
               w_ffn_in=w_ffn_in, w_ffn_out=w_ffn_out)
    depth = w_in.shape[0]
    bp, tp, d = x_prompt.shape
    bs, ts, _ = x_sample.shape
    p_len = cache_fox_k.shape[2]
    n_mem = mem_prompt.shape[1]
    w = BRANCH_W

    tabs_p = _rotary_tables(jnp.arange(tp, dtype=jnp.int32))
    tabs_s = _rotary_tables(p_len + jnp.arange(ts, dtype=jnp.int32))

    past_s = (cache_fox_k.transpose(0, 1, 3, 4, 2).reshape(depth, bs, w, p_len),
              cache_fox_v.transpose(0, 1, 3, 4, 2).reshape(depth, bs, w, p_len),
              cache_fox_logf.transpose(0, 1, 3, 2),
              cache_diff_k.transpose(0, 1, 3, 4, 5, 2).reshape(depth, bs, w, p_len),
              cache_diff_v.reshape(depth, bs, p_len * DIFF_HEADS, 2 * HEAD_DIM),
              state_rwkv, state_rwkv_shift)
    mem_rows = (n_mem * XATTN_HEADS, XATTN_HEAD_DIM)
    mem_k_s = cache_mem_k.reshape((depth, bs) + mem_rows)
    mem_v_s = cache_mem_v.reshape((depth, bs) + mem_rows)

    xp = x_prompt.reshape(bp * tp, d)
    xs = x_sample.reshape(bs * ts, d)
    mem = mem_prompt.reshape(bp * n_mem, d)
    p_new, s_new, p_mk, p_mv = [], [], [], []
    bufs_p = _new_state_buffers(depth, bp, tp)
    bufs_s = _new_state_buffers(depth, bs, ts)
    for l in range(depth):
        lw = _layer_weights(l, wts)
        mk, mv = _memkv(mem, lw["g_mem"], lw["w_xkv"], _tile(bp * n_mem, 256))
        xp, st_p, bufs_p = _layer(xp, lw, l, bp, tp, None, mk.reshape((1, bp) + mem_rows),
                                  mv.reshape((1, bp) + mem_rows), 0, tabs_p, bufs_p)
        xs, st_s, bufs_s = _layer(xs, lw, l, bs, ts, past_s, mem_k_s, mem_v_s, l, tabs_s, bufs_s)
        p_new.append(st_p)
        s_new.append(st_s)
        p_mk.append(mk.reshape(bp, n_mem, XATTN_HEADS, XATTN_HEAD_DIM))
        p_mv.append(mv.reshape(bp, n_mem, XATTN_HEADS, XATTN_HEAD_DIM))
    p_out = _assemble_states(p_new, bufs_p, bp, tp)
    s_out = _assemble_states(s_new, bufs_s, bs, ts)
    return (xp.reshape(bp, tp, d), xs.reshape(bs, ts, d)) + p_out + (jnp.stack(p_mk), jnp.stack(p_mv)) + s_out
```

```python
import functools
import math

import numpy as np
import jax
import jax.numpy as jnp
from jax import lax
from jax.experimental import pallas as pl
from jax.experimental.pallas import tpu as pltpu

F32 = jnp.float32
BF = jnp.bfloat16

CHUNK = 64
HEAD_DIM = 64
FOX_HEADS = 8
DIFF_HEADS = 4
RWKV_HEADS = 8
GROUPS = 8
BRANCH_W = 512
DECAY_LORA = 64
AAA_LORA = 64
GATE_LORA = 128
RWKV_IN = 3 * BRANCH_W + DECAY_LORA + AAA_LORA + GATE_LORA
ROT_DIM = HEAD_DIM // 4
ROPE_THETA = 500000.0
XATTN_HEADS = 4
XATTN_HEAD_DIM = 128
RMS_EPS = 1e-6
RWKV_GN_EPS = 64e-5
ATTN_SCALE = HEAD_DIM ** -0.5
LOG2E = math.log2(math.e)
QK_SCALE = ATTN_SCALE * LOG2E
D_PIECES = 3
NEG_BIG = -1e30

LANES = 128
SUBLANES = 8
VMEM_LIMIT_BYTES = 56 * 1024 * 1024


def _cparams(*sem):
    return pltpu.CompilerParams(dimension_semantics=sem, vmem_limit_bytes=VMEM_LIMIT_BYTES)


def _const_spec(shape):
    nd = len(shape)
    return pl.BlockSpec(shape, lambda *_: (0,) * nd)


def _rms(x, g):
    return x * lax.rsqrt(jnp.mean(x * x, axis=-1, keepdims=True) + RMS_EPS) * g


_NN = ((1,), (0,))
_NT = ((1,), (1,))
_TN = ((0,), (0,))


def _dg(a, b, dims):
    return lax.dot_general(a, b, (dims, ((), ())), preferred_element_type=F32)


def _dot(a, b):
    return _dg(a.astype(BF), b.astype(BF), _NN)


def _dot_nt(a, b):
    return _dg(a.astype(BF), b.astype(BF), _NT)


def _split2(x):
    hi = x.astype(BF)
    lo = (x - hi.astype(F32)).astype(BF)
    return hi, lo


def _split3(x):
    hi = x.astype(BF)
    r1 = x - hi.astype(F32)
    mid = r1.astype(BF)
    lo = (r1 - mid.astype(F32)).astype(BF)
    return hi, mid, lo


def _dot_sel(sel, x):
    hi, mid, lo = _split3(x)
    return _dg(sel, hi, _NN) + (_dg(sel, mid, _NN) + _dg(sel, lo, _NN))


def _dot_sel_right(x, sel):
    hi, lo = _split2(x)
    return _dg(hi, sel, _NN) + _dg(lo, sel, _NN)


def _softplus(x):
    return jnp.maximum(x, 0.0) + jnp.log(1.0 + jnp.exp(-jnp.abs(x)))


def _fox_inproj_kernel(x_ref, g_ref, wq_ref, qc_ref, wkv_t_ref, wf_t_ref, kbuf_ref, vbuf_ref,
                       q_ref, kt_ref, vt_ref, f_ref):
    del kbuf_ref, vbuf_ref
    xn = _rms(x_ref[...], g_ref[...]).astype(BF)
    w = BRANCH_W
    q_ref[...] = (_dg(xn, wq_ref[...], _NN) * QK_SCALE + qc_ref[...]).astype(BF)
    kt_ref[0, 0] = _dg(wkv_t_ref[0:w, :], xn, _NT)
    vt_ref[0, 0] = _dg(wkv_t_ref[w:2 * w, :], xn, _NT)
    f_ref[0] = _dg(wf_t_ref[...], xn, _NT)


def _row_specs(b, t, tm, layer):
    nt = t // tm
    row = lambda width: pl.BlockSpec((tm, width), lambda bi, ti: (bi * nt + ti, 0))
    col = lambda height: pl.BlockSpec((1, height, tm), lambda bi, ti: (bi, 0, ti))
    state = lambda height: pl.BlockSpec((1, 1, height, tm), lambda bi, ti: (layer, bi, 0, ti))
    return nt, row, col, state


_ANY = pl.BlockSpec(memory_space=pl.ANY)


def _fox_q_layout(wq):
    d = wq.shape[0]
    w3 = wq.reshape(d, FOX_HEADS, HEAD_DIM)
    wq_slab = jnp.concatenate([w3, jnp.zeros_like(w3)], axis=2).reshape(d, FOX_HEADS * LANES)
    slab = np.zeros((FOX_HEADS, LANES), np.float32)
    slab[:, HEAD_DIM:HEAD_DIM + D_PIECES] = -1.0
    return wq_slab, jnp.asarray(slab.reshape(1, FOX_HEADS * LANES))


def _fox_inproj(x, g, wq_slab, q_const, wkv_t, wf_t, kbuf, vbuf, layer, b, tm):
    n, d = x.shape
    t = n // b
    w = BRANCH_W
    wq_w = wq_slab.shape[1]
    nt, row, col, state = _row_specs(b, t, tm, layer)
    bufs, (kshape, vshape), alias = _state_operands(kbuf, vbuf)
    return pl.pallas_call(
        _fox_inproj_kernel,
        grid=(b, nt),
        in_specs=[row(d), _const_spec((1, d)), _const_spec((d, wq_w)), _const_spec((1, wq_w)),
                  _const_spec((2 * w, d)), _const_spec((FOX_HEADS, d)), _ANY, _ANY],
        out_specs=[row(wq_w), state(w), state(w), col(FOX_HEADS)],
        out_shape=[jax.ShapeDtypeStruct((n, wq_w), BF), jax.ShapeDtypeStruct(kshape, F32),
                   jax.ShapeDtypeStruct(vshape, F32), jax.ShapeDtypeStruct((b, FOX_HEADS, t), F32)],
        input_output_aliases={6: 1, 7: 2} if alias else {},
        compiler_params=_cparams("parallel", "parallel"),
    )(x, g, wq_slab, q_const, wkv_t, wf_t, *bufs)


def _diff_inproj_kernel(x_ref, g_ref, wq_ref, wk_t_ref, wv_ref, cos_ref, sa_ref, sb_ref, cos_t_ref, sin_t_ref,
                        kbuf_ref, vbuf_ref, q_ref, kt_ref, v_ref):
    del kbuf_ref, vbuf_ref
    kt_ref = kt_ref.at[0]
    tm = x_ref.shape[0]
    xn = _rms(x_ref[...], g_ref[...]).astype(BF)
    w = BRANCH_W
    half = ROT_DIM // 2
    cos, sa, sb = cos_ref[...], sa_ref[...], sb_ref[...]
    yq = _dg(xn, wq_ref[...], _NN)
    for j in range(w // LANES):
        yj = yq[:, j * LANES:(j + 1) * LANES]
        rj = yj * cos + pltpu.roll(yj, LANES - half, 1) * sa + pltpu.roll(yj, half, 1) * sb
        q_ref[:, j * LANES:(j + 1) * LANES] = (rj * QK_SCALE).astype(BF)
    yk = _dg(wk_t_ref[...], xn, _NT)
    cos_t, sin_t = cos_t_ref[...], sin_t_ref[...]
    for gi in range(GROUPS):
        r0 = gi * HEAD_DIM
        y1, y2 = yk[r0:r0 + half, :], yk[r0 + half:r0 + ROT_DIM, :]
        kt_ref[0, r0:r0 + half, :] = y1 * cos_t - y2 * sin_t
        kt_ref[0, r0 + half:r0 + ROT_DIM, :] = y2 * cos_t + y1 * sin_t
        kt_ref[0, r0 + ROT_DIM:r0 + HEAD_DIM, :] = yk[r0 + ROT_DIM:r0 + HEAD_DIM, :]
    yv = _dg(xn, wv_ref[...], _NN)
    dv = 2 * HEAD_DIM
    for h in range(DIFF_HEADS):
        v_ref[0, pl.ds(h, tm, stride=DIFF_HEADS), :] = yv[:, h * dv:(h + 1) * dv]


def _diff_inproj(x, g, wq, wk_t, wv, tabs, kbuf, vbuf, layer, b, tm):
    n, d = x.shape
    t = n // b
    w = BRANCH_W
    nt, row, col, state = _row_specs(b, t, tm, layer)
    row_tab = pl.BlockSpec((tm, LANES), lambda bi, ti: (ti, 0))
    col_tab = pl.BlockSpec((ROT_DIM // 2, tm), lambda bi, ti: (0, ti))
    vspec = pl.BlockSpec((1, DIFF_HEADS * tm, 2 * HEAD_DIM), lambda bi, ti: (layer, bi * nt + ti, 0))
    bufs, (kshape, vshape), alias = _state_operands(kbuf, vbuf)
    return pl.pallas_call(
        _diff_inproj_kernel,
        grid=(b, nt),
        in_specs=[row(d), _const_spec((1, d)), _const_spec((d, w)), _const_spec((w, d)), _const_spec((d, w)),
                  row_tab, row_tab, row_tab, col_tab, col_tab, _ANY, _ANY],
        out_specs=[row(w), state(w), vspec],
        out_shape=[jax.ShapeDtypeStruct((n, w), BF), jax.ShapeDtypeStruct(kshape, F32),
                   jax.ShapeDtypeStruct(vshape, F32)],
        input_output_aliases={10: 1, 11: 2} if alias else {},
        compiler_params=_cparams("parallel", "parallel"),
    )(x, g, wq, wk_t, wv, *tabs, *bufs)


def _rotary_tables(pos):
    half = ROT_DIM // 2
    inv = ROPE_THETA ** (-np.arange(half, dtype=np.float32) / half)
    ang = pos.astype(F32)[:, None] * jnp.asarray(inv, F32)[None, :]
    cos, sin = jnp.cos(ang), jnp.sin(ang)
    t = pos.shape[0]
    one = jnp.ones((t, HEAD_DIM - ROT_DIM), F32)
    zero = jnp.zeros((t, HEAD_DIM - ROT_DIM), F32)
    zh = jnp.zeros((t, half), F32)
    cos_h = jnp.concatenate([cos, cos, one], axis=1)
    sa_h = jnp.concatenate([-sin, zh, zero], axis=1)
    sb_h = jnp.concatenate([zh, sin, zero], axis=1)
    dup = lambda a: jnp.concatenate([a, a], axis=1)
    return dup(cos_h), dup(sa_h), dup(sb_h), cos.T, sin.T


def _rwkv_prep_kernel(x_ref, g_ref, wc_ref, shift0_ref, mu_ref, w0_ref, w2_ref, a0_ref, a2_ref, g2_ref,
                      kks_ref, ka_ref, rk_ref, seg_ref,
                      r_ref, lw_ref, km_ref, v_ref, kk_ref, kka_ref, gg_ref, bonus_ref, shift_ref,
                      carry_ref):
    t = pl.program_id(1)
    tm = x_ref.shape[0]
    w = BRANCH_W
    xn = _rms(x_ref[...], g_ref[...]).astype(BF)
    c = _dg(xn, wc_ref[...], _NN)

    @pl.when(t == 0)
    def _():
        carry_ref[...] = shift0_ref[0]

    row = lax.broadcasted_iota(jnp.int32, (tm, 1), 0)
    prev = jnp.where(row == 0, carry_ref[...], pltpu.roll(c, 1, 0))
    last = c[tm - 1:tm, :]
    carry_ref[...] = last
    shift_ref[0] = last

    xs = c + mu_ref[...] * (prev - c)
    cr, ck, cv = xs[:, 0:w], xs[:, w:2 * w], xs[:, 2 * w:3 * w]
    o = 3 * w
    xw = xs[:, o:o + DECAY_LORA]
    xa = xs[:, o + DECAY_LORA:o + DECAY_LORA + AAA_LORA]
    xg = xs[:, o + DECAY_LORA + AAA_LORA:]
    z = w0_ref[...] + _dot(jnp.tanh(xw), w2_ref[...])
    w_raw = -_softplus(-z) - 0.5
    a = jax.nn.sigmoid(a0_ref[...] + _dot(xa, a2_ref[...]))
    seg = seg_ref[...]
    kk = ck * kks_ref[...]
    kk = kk / jnp.maximum(jnp.sqrt(_dot_sel_right(kk * kk, seg)), 1e-12)
    kmod = ck * (1.0 + (a - 1.0) * ka_ref[...])
    r_ref[...] = cr
    lw_ref[...] = -jnp.exp(w_raw)
    km_ref[...] = kmod
    v_ref[...] = cv
    kk_ref[...] = kk
    kka_ref[...] = kk * a
    gg_ref[...] = _dot(jax.nn.sigmoid(xg), g2_ref[...])
    bonus_ref[...] = _dot_sel_right(cr * kmod * rk_ref[...], seg) * cv


def _rwkv_prep(x, g, wc, shift0, vecs, mats, seg, b, tm):
    n, d = x.shape
    t = n // b
    w = BRANCH_W
    mu, w0, a0, kks, ka, rk = vecs
    w2, a2, g2 = mats
    nt, row, _, _ = _row_specs(b, t, tm, 0)
    per_b = pl.BlockSpec((1, 1, RWKV_IN), lambda bi, ti: (bi, 0, 0))
    outs = [jax.ShapeDtypeStruct((n, w), F32)] * 8 + [jax.ShapeDtypeStruct((b, 1, RWKV_IN), F32)]
    return pl.pallas_call(
        _rwkv_prep_kernel,
        grid=(b, nt),
        in_specs=[row(d), _const_spec((1, d)), _const_spec((d, RWKV_IN)), per_b,
                  _const_spec((1, RWKV_IN)), _const_spec((1, w)), _const_spec((DECAY_LORA, w)),
                  _const_spec((1, w)), _const_spec((AAA_LORA, w)), _const_spec((GATE_LORA, w)),
                  _const_spec((1, w)), _const_spec((1, w)), _const_spec((1, w)), _const_spec((w, w))],
        out_specs=[row(w)] * 8 + [per_b],
        out_shape=outs,
        scratch_shapes=[pltpu.VMEM((1, RWKV_IN), F32)],
        compiler_params=_cparams("parallel", "arbitrary"),
    )(x, g, wc, shift0, mu, w0, w2, a0, a2, g2, kks, ka, rk, seg)


def _logf_cumsum_kernel(z_ref, bias_ref, lf_ref, d_ref, *, past, new):
    z = z_ref[0]
    width = z.shape[1]
    col = lax.broadcasted_iota(jnp.int32, z.shape, 1)
    zz = z + bias_ref[...]
    log_sig = jnp.minimum(zz, 0.0) - jnp.log(1.0 + jnp.exp(-jnp.abs(zz)))
    lf = jnp.where(col < past, z, jnp.where(col < past + new, log_sig, 0.0))
    lf_ref[0] = lf
    x = lf
    s = 1
    while s < width:
        x = x + jnp.where(col >= s, pltpu.roll(x, s, 1), 0.0)
        s *= 2
    d_ref[0] = x


def _logf_cumsum(z, bias, past, new):
    b, h, width = z.shape
    blk = pl.BlockSpec((1, h, width), lambda i: (i, 0, 0))
    return pl.pallas_call(
        functools.partial(_logf_cumsum_kernel, past=past, new=new),
        grid=(b,),
        in_specs=[blk, _const_spec((h, 1))],
        out_specs=[blk, blk],
        out_shape=[jax.ShapeDtypeStruct(z.shape, F32)] * 2,
        compiler_params=_cparams("parallel"),
    )(z, bias)


def _softmax_update(s, m_ref, l_ref, idx):
    m_prev = m_ref[idx]
    m_new = jnp.maximum(m_prev, jnp.max(s, axis=-1, keepdims=True))
    alpha = jnp.exp2(m_prev - m_new)
    p = jnp.exp2(s - m_new)
    l_ref[idx] = alpha * l_ref[idx] + jnp.sum(p, axis=-1, keepdims=True)
    m_ref[idx] = m_new
    return p.astype(BF), alpha


def _init_softmax_state(m_ref, l_ref, acc_ref):
    m_ref[...] = jnp.full(m_ref.shape, NEG_BIG, F32)
    l_ref[...] = jnp.zeros(l_ref.shape, F32)
    acc_ref[...] = jnp.zeros(acc_ref.shape, F32)


def _row_max_update(s, m_ref, idx):
    reps = s.shape[1] // LANES
    m_prev = m_ref[idx]
    m_new = jnp.maximum(m_prev, jnp.max(s, axis=-1, keepdims=True))
    alpha = jnp.exp2(m_prev - m_new)
    p = jnp.exp2(s - jnp.concatenate([m_new] * reps, axis=1))
    m_ref[idx] = m_new
    return p.astype(BF), alpha


def _init_prompt_state(m_ref, acc_ref):
    m_ref[...] = jnp.full(m_ref.shape, NEG_BIG, F32)
    acc_ref[...] = jnp.zeros(acc_ref.shape, F32)


def _causal_tile_pairs(nq):
    pairs = [(i, j) for i in range(nq) for j in range(i + 1)]
    return (jnp.asarray([p[0] for p in pairs], jnp.int32), jnp.asarray([p[1] for p in pairs], jnp.int32))


def _causal_blocks(i, j, block):
    pl.when(j < i)(functools.partial(block, False))
    pl.when(j == i)(functools.partial(block, True))


def _fox_prompt_kernel(it_ref, jt_ref, q_ref, kt_ref, vt_ref, dk_ref, o_ref, m_ref, acc_ref):
    pair = pl.program_id(1)
    i, j = it_ref[pair], jt_ref[pair]
    tq, tk = q_ref.shape[0], kt_ref.shape[3]
    heads = [slice(h * HEAD_DIM, (h + 1) * HEAD_DIM) for h in range(FOX_HEADS)]

    @pl.when(j == 0)
    def _():
        _init_prompt_state(m_ref, acc_ref)

    def block(masked):
        kb = kt_ref[0, 0].astype(BF)
        vb = vt_ref[0, 0].astype(BF)
        ones = jnp.ones((HEAD_DIM, tk), BF)
        d_hi, d_mid, d_lo = _split3(dk_ref[0] * LOG2E)
        rows = 2 * SUBLANES
        row = lax.broadcasted_iota(jnp.int32, (rows, tk), 0)
        pad = jnp.zeros((LANES - HEAD_DIM - rows, tk), BF)
        k_ops = []
        for h, hs in enumerate(heads):
            pieces = [jnp.broadcast_to(a[h:h + 1, :].astype(F32), (rows, tk)) for a in (d_hi, d_mid, d_lo)]
            extra = jnp.where(row == 0, pieces[0], jnp.where(row == 1, pieces[1], jnp.where(row == 2, pieces[2], 0.0)))
            k_ops.append(jnp.concatenate([kb[hs, :], extra.astype(BF), pad], axis=0))
        scores = [_dg(q_ref[:, h * LANES:(h + 1) * LANES], k_ops[h], _NN) for h in range(FOX_HEADS)]
        if masked:
            visible = (lax.broadcasted_iota(jnp.int32, (tq, tk), 1) <= lax.broadcasted_iota(jnp.int32, (tq, tk), 0))
        for h, hs in enumerate(heads):
            s = jnp.where(visible, scores[h], NEG_BIG) if masked else scores[h]
            p, alpha = _row_max_update(s, m_ref, h)
            v_ones = jnp.concatenate([vb[hs, :], ones], axis=0)
            acc_ref[h] = alpha * acc_ref[h] + _dg(p, v_ones, _NT)

    _causal_blocks(i, j, block)

    @pl.when(j == i)
    def _():
        for h, hs in enumerate(heads):
            acc = acc_ref[h]
            o_ref[:, hs] = acc[:, 0:HEAD_DIM] / acc[:, HEAD_DIM:2 * HEAD_DIM]


def _fox_prompt(q, kt, vt, d_row, layer, b, tq):
    n, wq = q.shape
    w = BRANCH_W
    t = n // b
    nq = t // tq
    i_tab, j_tab = _causal_tile_pairs(nq)
    qspec = lambda width: pl.BlockSpec((tq, width), lambda bi, p, it, jt: (bi * nq + it[p], 0))
    kspec = pl.BlockSpec((1, 1, w, tq), lambda bi, p, it, jt: (layer, bi, 0, jt[p]))
    grid_spec = pltpu.PrefetchScalarGridSpec(
        num_scalar_prefetch=2,
        grid=(b, i_tab.shape[0]),
        in_specs=[qspec(wq), kspec, kspec,
                  pl.BlockSpec((1, FOX_HEADS, tq), lambda bi, p, it, jt: (bi, 0, jt[p]))],
        out_specs=qspec(w),
        scratch_shapes=[pltpu.VMEM((FOX_HEADS, tq, LANES), F32), pltpu.VMEM((FOX_HEADS, tq, 2 * HEAD_DIM), F32)],
    )
    return pl.pallas_call(
        _fox_prompt_kernel,
        grid_spec=grid_spec,
        out_shape=jax.ShapeDtypeStruct((n, w), F32),
        compiler_params=_cparams("parallel", "arbitrary"),
    )(i_tab, j_tab, q, kt, vt, d_row)


def _diff_lambda(lam_ref, lam_init):
    p = lam_ref[...]
    s1 = jnp.sum(p[0:1, :] * p[1:2, :], axis=-1, keepdims=True)
    s2 = jnp.sum(p[2:3, :] * p[3:4, :], axis=-1, keepdims=True)
    return jnp.exp(s1) - jnp.exp(s2) + lam_init


def _diff_combine(acc0, l0, acc1, l1, lam, subln, lam_init):
    o = acc0 / l0 - lam * (acc1 / l1)
    return _rms(o, subln) * (1.0 - lam_init)


def _diff_prompt_kernel(it_ref, jt_ref, q_ref, kt_ref, v_ref, lam_ref, subln_ref, o_ref, m_ref, acc_ref, *,
                        lam_init):
    pair = pl.program_id(1)
    i, j = it_ref[pair], jt_ref[pair]
    tq, tk = q_ref.shape[0], kt_ref.shape[3]
    dv = 2 * HEAD_DIM

    @pl.when(j == 0)
    def _():
        _init_prompt_state(m_ref, acc_ref)

    def block(masked):
        kb = kt_ref[0, 0].astype(BF)
        ones = jnp.ones((tk, dv), BF)
        groups = [slice(gi * HEAD_DIM, (gi + 1) * HEAD_DIM) for gi in range(GROUPS)]
        scores = [_dg(q_ref[:, gs], kb[gs, :], _NN) for gs in groups]
        if masked:
            visible = ((lax.broadcasted_iota(jnp.int32, (tq, tk), 1) // CHUNK)
                       <= (lax.broadcasted_iota(jnp.int32, (tq, tk), 0) // CHUNK))
        for h in range(DIFF_HEADS):
            vh = v_ref[0, pl.ds(h, tk, stride=DIFF_HEADS), :].astype(BF)
            v_ones = jnp.concatenate([vh, ones], axis=1)
            for gi in (2 * h, 2 * h + 1):
                s = jnp.where(visible, scores[gi], NEG_BIG) if masked else scores[gi]
                p, alpha = _row_max_update(s, m_ref, gi)
                acc_ref[gi] = jnp.concatenate([alpha, alpha], axis=1) * acc_ref[gi] + _dg(p, v_ones, _NN)

    _causal_blocks(i, j, block)

    @pl.when(j == i)
    def _():
        lam = _diff_lambda(lam_ref, lam_init)
        for h in range(DIFF_HEADS):
            a0, a1 = acc_ref[2 * h], acc_ref[2 * h + 1]
            o_ref[:, h * dv:(h + 1) * dv] = _diff_combine(a0[:, 0:dv], a0[:, dv:], a1[:, 0:dv], a1[:, dv:], lam,
                                                          subln_ref[...], lam_init)


def _diff_prompt(q, kt, v, lam_p, subln, lam_init, layer, b, tq):
    n, w = q.shape
    t = n // b
    nq = t // tq
    i_tab, j_tab = _causal_tile_pairs(nq)
    qspec = pl.BlockSpec((tq, w), lambda bi, p, it, jt: (bi * nq + it[p], 0))
    ktspec = pl.BlockSpec((1, 1, w, tq), lambda bi, p, it, jt: (layer, bi, 0, jt[p]))
    vspec = pl.BlockSpec((1, DIFF_HEADS * tq, 2 * HEAD_DIM), lambda bi, p, it, jt: (layer, bi * nq + jt[p], 0))
    grid_spec = pltpu.PrefetchScalarGridSpec(
        num_scalar_prefetch=2,
        grid=(b, i_tab.shape[0]),
        in_specs=[qspec, ktspec, vspec, _const_spec((4, HEAD_DIM)), _const_spec((1, 2 * HEAD_DIM))],
        out_specs=qspec,
        scratch_shapes=[pltpu.VMEM((GROUPS, tq, LANES), F32), pltpu.VMEM((GROUPS, tq, 4 * HEAD_DIM), F32)],
    )
    return pl.pallas_call(
        functools.partial(_diff_prompt_kernel, lam_init=lam_init),
        grid_spec=grid_spec,
        out_shape=jax.ShapeDtypeStruct((n, w), F32),
        compiler_params=_cparams("parallel", "arbitrary"),
    )(i_tab, j_tab, q, kt, v, lam_p, subln)


def _sample_attn_kernel(*refs, fox, t_new, lam_init):
    if fox:
        q_ref, kc_ref, vc_ref, kn_ref, vn_ref, dq_ref, dkc_ref, dkn_ref, o_ref, m_ref, l_ref, acc_ref = refs
    else:
        q_ref, kc_ref, vc_ref, kn_ref, vn_ref, lam_ref, subln_ref, o_ref, m_ref, l_ref, acc_ref = refs
    j = pl.program_id(1)
    dv = 2 * HEAD_DIM
    pair = 2 * t_new

    @pl.when(j == 0)
    def _():
        _init_softmax_state(m_ref, l_ref, acc_ref)

    def expand(dk):
        return jnp.concatenate([jnp.broadcast_to(dk[gi:gi + 1, :], (t_new, dk.shape[1])) for gi in range(GROUPS)],
                               axis=0)

    def scores(kt, dk, causal):
        s = _dg(q_ref[0], kt.astype(BF), _NN)
        if fox:
            s = s + dq_ref[0] * LOG2E - expand(dk * LOG2E)
        if causal:
            qi = lax.broadcasted_iota(jnp.int32, s.shape, 0) % t_new
            kj = lax.broadcasted_iota(jnp.int32, s.shape, 1)
            s = jnp.where(kj <= qi, s, NEG_BIG)
        return _softmax_update(s, m_ref, l_ref, 0)

    def accumulate_diff(p, alpha, value_of_head):
        for h in range(DIFF_HEADS):
            rs = slice(h * pair, (h + 1) * pair)
            acc_ref[0, rs, :] = alpha[rs] * acc_ref[0, rs, :] + _dg(p[rs], value_of_head(h).astype(BF), _NN)

    tk = kc_ref.shape[3]
    if fox:
        p, alpha = scores(kc_ref[0, 0], dkc_ref[0], False)
        acc_ref[0] = alpha * acc_ref[0] + _dg(p, vc_ref[0, 0].astype(BF), _NT)
    else:
        p, alpha = scores(kc_ref[0, 0], None, False)
        accumulate_diff(p, alpha, lambda h: vc_ref[0, 0, pl.ds(h, tk, stride=DIFF_HEADS), :])

    @pl.when(j == pl.num_programs(1) - 1)
    def _():
        if fox:
            p, alpha = scores(kn_ref[0, 0], dkn_ref[0][:, 0:t_new], True)
            acc = alpha * acc_ref[0] + _dg(p, vn_ref[0, 0].astype(BF), _NT)
            l = l_ref[0]
            for gi in range(GROUPS):
                rs = slice(gi * t_new, (gi + 1) * t_new)
                cs = slice(gi * HEAD_DIM, (gi + 1) * HEAD_DIM)
                o_ref[:, cs] = acc[rs, cs] / l[rs]
        else:
            p, alpha = scores(kn_ref[0, 0], None, False)
            accumulate_diff(p, alpha, lambda h: vn_ref[0, pl.ds(h, t_new, stride=DIFF_HEADS), :])
            acc = acc_ref[0]
            l = l_ref[0]
            lam = _diff_lambda(lam_ref, lam_init)
            for h in range(DIFF_HEADS):
                r0 = slice(h * pair, h * pair + t_new)
                r1 = slice(h * pair + t_new, (h + 1) * pair)
                o_ref[:, h * dv:(h + 1) * dv] = _diff_combine(acc[r0], l[r0], acc[r1], l[r1], lam, subln_ref[...],
                                                              lam_init)


def _sample_attn(qbd, kc, vc, kn, vn, extra, *, layer, fox, b, t_new, past, tk, lam_init=0.0):
    w = BRANCH_W
    rows = GROUPS * t_new
    nk = past // tk
    qspec = pl.BlockSpec((1, rows, w), lambda bi, j: (bi, 0, 0))
    kcspec = pl.BlockSpec((1, 1, w, tk), lambda bi, j: (layer, bi, 0, j))
    knspec = pl.BlockSpec((1, 1, w, t_new), lambda bi, j: (layer, bi, 0, 0))
    ospec = pl.BlockSpec((t_new, w), lambda bi, j: (bi, 0))
    if fox:
        dq, d_row = extra
        vcspec, vnspec, acc_w = kcspec, knspec, w
        especs = [pl.BlockSpec((1, rows, 1), lambda bi, j: (bi, 0, 0)),
                  pl.BlockSpec((1, GROUPS, tk), lambda bi, j: (bi, 0, j)),
                  pl.BlockSpec((1, GROUPS, LANES), lambda bi, j: (bi, 0, past // LANES))]
        eargs = [dq, d_row, d_row]
    else:
        vcspec = pl.BlockSpec((1, 1, DIFF_HEADS * tk, 2 * HEAD_DIM), lambda bi, j: (layer, bi, j, 0))
        vnspec = pl.BlockSpec((1, DIFF_HEADS * t_new, 2 * HEAD_DIM), lambda bi, j: (layer, bi, 0))
        acc_w = 2 * HEAD_DIM
        especs = [_const_spec((4, HEAD_DIM)), _const_spec((1, 2 * HEAD_DIM))]
        eargs = list(extra)
    return pl.pallas_call(
        functools.partial(_sample_attn_kernel, fox=fox, t_new=t_new, lam_init=lam_init),
        grid=(b, nk),
        in_specs=[qspec, kcspec, vcspec, knspec, vnspec] + especs,
        out_specs=ospec,
        out_shape=jax.ShapeDtypeStruct((b * t_new, w), F32),
        scratch_shapes=[pltpu.VMEM((1, rows, 1), F32), pltpu.VMEM((1, rows, 1), F32),
                        pltpu.VMEM((1, rows, acc_w), F32)],
        compiler_params=_cparams("parallel", "arbitrary"),
    )(qbd, kc, vc, kn, vn, *eargs)


def _block_diag_queries(q, b, t_new):
    q4 = q.reshape(b, t_new, GROUPS, HEAD_DIM)
    eye = jnp.eye(GROUPS, dtype=q.dtype)
    qbd = q4.transpose(0, 2, 1, 3)[:, :, :, None, :] * eye[None, :, None, :, None]
    return qbd.reshape(b, GROUPS * t_new, GROUPS * HEAD_DIM)


def _rwkv_scan_kernel(r_ref, lw_ref, k_ref, v_ref, kk_ref, kka_ref, gg_ref, bonus_ref, gain_ref, bias_ref,
                      s0_ref, o_ref, sfin_ref, s_ref):
    c = pl.program_id(1)
    nb, ch = r_ref.shape[0], r_ref.shape[1]
    n = HEAD_DIM

    @pl.when(c == 0)
    def _():
        s_ref[...] = s0_ref[...]

    ri = lax.broadcasted_iota(jnp.int32, (ch, ch), 0)
    ci = lax.broadcasted_iota(jnp.int32, (ch, ch), 1)
    tri = (ci <= ri).astype(BF)
    ri2 = lax.broadcasted_iota(jnp.int32, (ch, 2 * ch), 0)
    ci2 = lax.broadcasted_iota(jnp.int32, (ch, 2 * ch), 1)
    ci2m = jnp.where(ci2 >= ch, ci2 - ch, ci2)
    strict_right = (ci2 >= ch) & (ci2m < ri2)
    strict_left = ci < ri
    is_x = lax.broadcasted_iota(jnp.int32, (ch, ch + n), 1) >= ch
    incl_both = ci2m <= ri2
    steps = max(1, int(math.ceil(math.log2(ch))))
    heads = [slice(h * n, (h + 1) * n) for h in range(RWKV_HEADS)]

    prep = []
    for bi in range(nb):
        lw = lw_ref[bi]
        cum = _dot_sel(tri, lw)
        cum_end = cum[ch - 1:ch, :]
        kk, kka, kmod, v_all = kk_ref[bi], kka_ref[bi], k_ref[bi], v_ref[bi]
        e_neg = jnp.exp(-cum)
        e_tail = jnp.exp(cum_end - cum)
        prep.append(dict(
            ar=_split2(jnp.concatenate([-kk * jnp.exp(cum - lw), r_ref[bi] * jnp.exp(cum)], axis=0)),
            bk=_split2(jnp.concatenate([kka * e_neg, kmod * e_neg], axis=0)),
            bk_tail=_split2(jnp.concatenate([kka * e_tail, kmod * e_tail], axis=0)),
            vv=_split2(jnp.concatenate([v_all, v_all], axis=0)),
            v=v_all, decay_end=jnp.exp(cum_end)))
    units = [(bi, h, heads[h]) for bi in range(nb) for h in range(RWKV_HEADS)]
    s_old = [s_ref[bi, h] for bi, h, _ in units]

    def dot3s(a2, b2, dims):
        (ah, al), (bh, bl) = a2, b2
        m = ah.shape[0]
        r = _dg(jnp.concatenate([ah, al], axis=0), bh, dims)
        return r[0:m] + r[m:] + _dg(ah, bl, dims)

    def dot3_tn(a2, b2):
        (ah, al), (bh, bl) = a2, b2
        return _dg(ah, bh, _TN) + (_dg(ah, bl, _TN) + _dg(al, bh, _TN))

    def nil_times(zh):
        a2, b2 = _split2(zh[:, 0:ch]), _split2(zh)
        wz = zh.shape[1]
        if wz % LANES:
            return dot3s(a2, b2, _NN)
        r = _dg(jnp.concatenate(a2, axis=0), jnp.concatenate(b2, axis=1), _NN)
        return (r[0:ch, 0:wz] + r[ch:, 0:wz]) + (r[0:ch, wz:] + r[ch:, wz:])

    cols = lambda a2, hs: (a2[0][:, hs], a2[1][:, hs])
    ga = []
    for u, (bi, h, hs) in enumerate(units):
        sh, sl = _split2(s_old[u])
        bk = prep[bi]["bk"]
        rhs = (jnp.concatenate([bk[0][:, hs], sh], axis=0), jnp.concatenate([bk[1][:, hs], sl], axis=0))
        ga.append(dot3s(cols(prep[bi]["ar"], hs), rhs, _NT))
    z = []
    for u, (bi, h, hs) in enumerate(units):
        g_top = ga[u][0:ch, 0:2 * ch]
        x0 = ga[u][0:ch, 2 * ch:] + dot3s(_split2(jnp.where(strict_right, g_top, 0.0)), cols(prep[bi]["vv"], hs), _NN)
        z.append(jnp.concatenate([jnp.where(strict_left, g_top[:, 0:ch], 0.0), x0], axis=1))
    for _ in range(steps):
        z = [nil_times(zh) + jnp.where(is_x, zh, 0.0) for zh in z]
    for u, (bi, h, hs) in enumerate(units):
        x = z[u][:, ch:]
        xv = _split2(jnp.concatenate([x, prep[bi]["v"][:, hs]], axis=0))
        g_bot = jnp.where(incl_both, ga[u][ch:, 0:2 * ch], 0.0)
        y = ga[u][ch:, 2 * ch:] + _dg(g_bot.astype(BF), xv[0], _NN)
        s_ref[bi, h] = s_old[u] * prep[bi]["decay_end"][:, hs] + dot3_tn(xv, cols(prep[bi]["bk_tail"], hs))

        mu = jnp.mean(y, axis=-1, keepdims=True)
        yc = y - mu
        var = jnp.mean(yc * yc, axis=-1, keepdims=True)
        yn = yc * lax.rsqrt(var + RWKV_GN_EPS) * gain_ref[:, hs] + bias_ref[:, hs]
        o_ref[bi, :, hs] = (yn + bonus_ref[bi, :, hs]) * gg_ref[bi, :, hs]

    @pl.when(c == pl.num_programs(1) - 1)
    def _():
        sfin_ref[...] = s_ref[...]


def _rwkv_scan(r, lw, km, v, kk, kka, gg, bonus, gain, bias, s0, b, ch, nb):
    n, w = r.shape
    t = n // b
    rows = [a.reshape(b, t, w) for a in (r, lw, km, v, kk, kka, gg, bonus)]
    row = pl.BlockSpec((nb, ch, w), lambda bi, ci: (bi, ci, 0))
    st = pl.BlockSpec((nb, RWKV_HEADS, HEAD_DIM, HEAD_DIM), lambda bi, ci: (bi, 0, 0, 0))
    out, s_fin = pl.pallas_call(
        _rwkv_scan_kernel,
        grid=(b // nb, t // ch),
        in_specs=[row] * 8 + [_const_spec((1, w)), _const_spec((1, w)), st],
        out_specs=[row, st],
        out_shape=[jax.ShapeDtypeStruct((b, t, w), F32), jax.ShapeDtypeStruct(s0.shape, F32)],
        scratch_shapes=[pltpu.VMEM((nb, RWKV_HEADS, HEAD_DIM, HEAD_DIM), F32)],
        compiler_params=_cparams("parallel", "arbitrary"),
    )(*rows, gain, bias, s0)
    return out.reshape(n, w), s_fin


def _merge_kernel(x_ref, oa_ref, ob_ref, oc_ref, gpre_ref, wg_ref, wb_ref, wo_ref, gpost_ref, y_ref):
    x = x_ref[...]
    d = x.shape[1]
    xn = _rms(x, gpre_ref[...]).astype(BF)
    acc = jnp.zeros(x.shape, F32)
    for i, o_ref in enumerate((oa_ref, ob_ref, oc_ref)):
        gate = jax.nn.sigmoid(_dg(xn, wg_ref[:, i * d:(i + 1) * d], _NN))
        acc = acc + gate * _dot(o_ref[...], wb_ref[i])
    y = _dot(acc, wo_ref[...])
    y_ref[...] = x + _rms(y, gpost_ref[...])


def _merge(x, oa, ob, oc, gpre, wg, wb, wo, gpost, tm):
    n, d = x.shape
    w = BRANCH_W
    row = lambda width: pl.BlockSpec((tm, width), lambda i: (i, 0))
    return pl.pallas_call(
        _merge_kernel,
        grid=(n // tm,),
        in_specs=[row(d), row(w), row(w), row(w), _const_spec((1, d)), _const_spec((d, 3 * d)),
                  _const_spec((3, w, d)), _const_spec((d, d)), _const_spec((1, d))],
        out_specs=row(d),
        out_shape=jax.ShapeDtypeStruct((n, d), F32),
        compiler_params=_cparams("parallel"),
    )(x, oa, ob, oc, gpre, wg, wb, wo, gpost)


def _memkv_kernel(m_ref, g_ref, w_ref, k_ref, v_ref):
    xn = _rms(m_ref[...], g_ref[...]).astype(BF)
    tm = m_ref.shape[0]
    hd = XATTN_HEAD_DIM
    w = XATTN_HEADS * hd
    yk = _dg(xn, w_ref[:, 0:w], _NN)
    yv = _dg(xn, w_ref[:, w:2 * w], _NN)
    for h in range(XATTN_HEADS):
        k_ref[pl.ds(h, tm, stride=XATTN_HEADS), :] = yk[:, h * hd:(h + 1) * hd]
        v_ref[pl.ds(h, tm, stride=XATTN_HEADS), :] = yv[:, h * hd:(h + 1) * hd]


def _memkv(mem, g, w_xkv, tm):
    n, d = mem.shape
    w = w_xkv.shape[1] // 2
    out = pl.BlockSpec((XATTN_HEADS * tm, XATTN_HEAD_DIM), lambda i: (i, 0))
    return pl.pallas_call(
        _memkv_kernel,
        grid=(n // tm,),
        in_specs=[pl.BlockSpec((tm, d), lambda i: (i, 0)), _const_spec((1, d)), _const_spec((d, 2 * w))],
        out_specs=[out, out],
        out_shape=[jax.ShapeDtypeStruct((XATTN_HEADS * n, XATTN_HEAD_DIM), F32)] * 2,
        compiler_params=_cparams("parallel"),
    )(mem, g, w_xkv)


def _xattn_kernel(x_ref, gpre_ref, wq_ref, mk_ref, mv_ref, wo_ref, gpost_ref, y_ref):
    x = x_ref[...]
    xn = _rms(x, gpre_ref[...])
    q = _dot(xn, wq_ref[...])
    hd = XATTN_HEAD_DIM
    n_mem = mk_ref.shape[2] // XATTN_HEADS
    outs = []
    for h in range(XATTN_HEADS):
        hs = slice(h * hd, (h + 1) * hd)
        mem_rows = pl.ds(h, n_mem, stride=XATTN_HEADS)
        s = _dot_nt(q[:, hs], mk_ref[0, 0, mem_rows, :]) * (hd ** -0.5)
        p = jnp.exp(s - jnp.max(s, axis=-1, keepdims=True))
        p = p / jnp.sum(p, axis=-1, keepdims=True)
        outs.append(_dot(p, mv_ref[0, 0, mem_rows, :]))
    o = jnp.concatenate(outs, axis=-1)
    y_ref[...] = x + _rms(_dot(o, wo_ref[...]), gpost_ref[...])


def _xattn(x, gpre, wq, mk, mv, wo, gpost, layer, b, tm):
    n, d = x.shape
    t = n // b
    nt = t // tm
    w = wq.shape[1]
    row = pl.BlockSpec((tm, d), lambda bi, ti: (bi * nt + ti, 0))
    mem = pl.BlockSpec((1, 1) + mk.shape[2:], lambda bi, ti: (layer, bi, 0, 0))
    return pl.pallas_call(
        _xattn_kernel,
        grid=(b, nt),
        in_specs=[row, _const_spec((1, d)), _const_spec((d, w)), mem, mem, _const_spec((w, d)),
                  _const_spec((1, d))],
        out_specs=row,
        out_shape=jax.ShapeDtypeStruct((n, d), F32),
        compiler_params=_cparams("parallel", "parallel"),
    )(x, gpre, wq, mk, mv, wo, gpost)


def _ffn_kernel(x_ref, gpre_ref, wu_ref, wv_ref, wo_ref, gpost_ref, y_ref, xn_ref, acc_ref):
    j = pl.program_id(1)

    @pl.when(j == 0)
    def _():
        xn_ref[...] = _rms(x_ref[...], gpre_ref[...]).astype(BF)
        acc_ref[...] = jnp.zeros(acc_ref.shape, F32)

    xn = xn_ref[...]
    u = _dg(xn, wu_ref[...], _NN)
    v = _dg(xn, wv_ref[...], _NN)
    acc_ref[...] += _dot(u * jax.nn.sigmoid(u) * v, wo_ref[...])

    @pl.when(j == pl.num_programs(1) - 1)
    def _():
        y_ref[...] = x_ref[...] + _rms(acc_ref[...], gpost_ref[...])


def _ffn(x, gpre, w_in, w_out, gpost, tm, th):
    n, d = x.shape
    hidden = w_out.shape[0]
    nh = hidden // th
    row = pl.BlockSpec((tm, d), lambda i, j: (i, 0))
    return pl.pallas_call(
        _ffn_kernel,
        grid=(n // tm, nh),
        in_specs=[row, _const_spec((1, d)),
                  pl.BlockSpec((d, th), lambda i, j: (0, j)),
                  pl.BlockSpec((d, th), lambda i, j: (0, nh + j)),
                  pl.BlockSpec((th, d), lambda i, j: (j, 0)),
                  _const_spec((1, d))],
        out_specs=row,
        out_shape=jax.ShapeDtypeStruct((n, d), F32),
        scratch_shapes=[pltpu.VMEM((tm, d), BF), pltpu.VMEM((tm, d), F32)],
        compiler_params=_cparams("parallel", "arbitrary"),
    )(x, gpre, w_in, w_in, w_out, gpost)


def _tile(n, pref):
    t = min(n, pref)
    assert n % t == 0, (n, t)
    return t


def _layer_weights(l, wts):
    p = {k: v[l] for k, v in wts.items()}
    w = BRANCH_W
    w_in = p["w_in"].astype(BF)
    w_in_t = p["w_in"].T.astype(BF)
    gate_w = w_in.shape[1] - (6 * w + FOX_HEADS + RWKV_IN)
    o_fox, o_f, o_diff = 0, 3 * w, 3 * w + FOX_HEADS
    o_rwkv = o_diff + 3 * w
    o_gate = o_rwkv + RWKV_IN
    row = lambda a: a.reshape(1, -1)
    seg = np.kron(np.eye(RWKV_HEADS, dtype=np.float32), np.ones((HEAD_DIM, HEAD_DIM), np.float32))
    return dict(
        g_mix_pre=row(p["norm_mix_pre"]), g_mix_post=row(p["norm_mix_post"]),
        w_fox_q=_fox_q_layout(w_in[:, o_fox:o_fox + w]), w_fox_kv_t=w_in_t[o_fox + w:o_fox + 3 * w],
        w_fox_f_t=w_in_t[o_f:o_f + FOX_HEADS],
        w_diff_q=w_in[:, o_diff:o_diff + w], w_diff_k_t=w_in_t[o_diff + w:o_diff + 2 * w],
        w_diff_v=w_in[:, o_diff + 2 * w:o_diff + 3 * w],
        w_rwkv=w_in[:, o_rwkv:o_rwkv + RWKV_IN], w_gate=w_in[:, o_gate:o_gate + gate_w],
        fox_bias=p["fox_forget_bias"].reshape(FOX_HEADS, 1),
        diff_lambda=p["diff_lambda"], diff_subln=row(p["diff_subln"]),
        rwkv_vecs=tuple(row(p[k]) for k in ("rwkv_mu", "rwkv_w0", "rwkv_a0", "rwkv_kk_scale", "rwkv_ka",
                                            "rwkv_rk")),
        rwkv_mats=tuple(p[k].astype(BF) for k in ("rwkv_w2", "rwkv_a2", "rwkv_g2")),
        rwkv_seg=jnp.asarray(seg, BF),
        rwkv_gain=row(p["rwkv_ln_gain"]), rwkv_bias=row(p["rwkv_ln_bias"]),
        w_branch=p["w_branch"].astype(BF), w_out=p["w_out"].astype(BF),
        g_x_pre=row(p["norm_x_pre"]), g_x_post=row(p["norm_x_post"]), g_mem=row(p["norm_mem"]),
        w_xq=p["w_xq"].astype(BF), w_xkv=p["w_xkv"].astype(BF), w_xo=p["w_xo"].astype(BF),
        g_ffn_pre=row(p["norm_ffn_pre"]), g_ffn_post=row(p["norm_ffn_post"]),
        w_ffn_in=p["w_ffn_in"].astype(BF), w_ffn_out=p["w_ffn_out"].astype(BF),
    )


def _new_state_buffers(depth, b, t):
    kt = (depth, b, BRANCH_W, t)
    return dict(kf=kt, vf=kt, kd=kt, vd=(depth, b * t * DIFF_HEADS, 2 * HEAD_DIM))


def _state_operands(kbuf, vbuf):
    if isinstance(kbuf, tuple):
        placeholder = jnp.zeros((SUBLANES, LANES), F32)
        return (placeholder, placeholder), (kbuf, vbuf), False
    return (kbuf, vbuf), (kbuf.shape, vbuf.shape), True


def _mixer(x, lw, l, b, t, past, tabs, bufs):
    n = x.shape[0]
    lam_init = 0.8 - 0.6 * math.exp(-0.3 * l)
    tm = _tile(t, 512)
    qf, kf_t, vf_t, f_bt = _fox_inproj(x, lw["g_mix_pre"], *lw["w_fox_q"], lw["w_fox_kv_t"], lw["w_fox_f_t"],
                                       bufs["kf"], bufs["vf"], l, b, tm)
    qd, kd_t, vd = _diff_inproj(x, lw["g_mix_pre"], lw["w_diff_q"], lw["w_diff_k_t"], lw["w_diff_v"], tabs,
                                bufs["kd"], bufs["vd"], l, b, tm)
    bufs = dict(kf=kf_t, vf=vf_t, kd=kd_t, vd=vd)

    if past is None:
        p_len = 0
        z = f_bt
        shift0 = jnp.zeros((b, 1, RWKV_IN), F32)
        s0 = jnp.zeros((b, RWKV_HEADS, HEAD_DIM, HEAD_DIM), F32)
    else:
        fox_kt, fox_vt, fox_lf_t, diff_kt, diff_v, s0, shift0 = past
        p_len = fox_kt.shape[3]
        pad = (-(p_len + t)) % LANES
        z = jnp.concatenate([fox_lf_t[l], f_bt, jnp.zeros((b, FOX_HEADS, pad), F32)], axis=2)
        s0 = s0[l]
        shift0 = shift0[l].reshape(b, 1, RWKV_IN)
    lf, d_row = _logf_cumsum(z, lw["fox_bias"], p_len, t)
    log_f = lf[:, :, p_len:p_len + t].transpose(0, 2, 1)

    if past is None:
        tq = _tile(t, 512)
        out_a = _fox_prompt(qf, kf_t, vf_t, d_row, l, b, tq)
        out_b = _diff_prompt(qd, kd_t, vd, lw["diff_lambda"], lw["diff_subln"], lam_init, l, b, tq)
    else:
        assert p_len % CHUNK == 0 and t <= CHUNK and p_len % LANES == 0
        tk = _tile(p_len, 2048)
        dq = d_row[:, :, p_len:p_len + t].reshape(b, GROUPS * t, 1)
        qf_feat = qf.reshape(n, FOX_HEADS, LANES)[:, :, 0:HEAD_DIM].reshape(n, BRANCH_W)
        out_a = _sample_attn(_block_diag_queries(qf_feat, b, t), fox_kt, fox_vt, kf_t, vf_t, (dq, d_row),
                             layer=l, fox=True, b=b, t_new=t, past=p_len, tk=tk)
        out_b = _sample_attn(_block_diag_queries(qd, b, t), diff_kt, diff_v, kd_t, vd,
                             (lw["diff_lambda"], lw["diff_subln"]),
                             layer=l, fox=False, b=b, t_new=t, past=p_len, tk=tk, lam_init=lam_init)

    r, lgw, km, v, kk, kka, gg, bonus, shift = _rwkv_prep(
        x, lw["g_mix_pre"], lw["w_rwkv"], shift0, lw["rwkv_vecs"], lw["rwkv_mats"], lw["rwkv_seg"], b, tm)
    ch = _tile(t, CHUNK)
    out_c, s_new = _rwkv_scan(r, lgw, km, v, kk, kka, gg, bonus, lw["rwkv_gain"], lw["rwkv_bias"], s0, b, ch,
                              _tile(b, 4))

    y = _merge(x, out_a, out_b, out_c, lw["g_mix_pre"], lw["w_gate"], lw["w_branch"], lw["w_out"],
               lw["g_mix_post"], _tile(n, 512))
    state = (log_f, s_new, shift.reshape(b, RWKV_IN))
    return y, state, bufs


def _layer(x, lw, l, b, t, past, mem_k, mem_v, mem_layer, tabs, bufs):
    n = x.shape[0]
    x, state, bufs = _mixer(x, lw, l, b, t, past, tabs, bufs)
    x = _xattn(x, lw["g_x_pre"], lw["w_xq"], mem_k, mem_v, lw["w_xo"], lw["g_x_post"], mem_layer, b,
               _tile(t, 512))
    hidden = lw["w_ffn_out"].shape[0]
    th = hidden // 2 if (hidden // 2) % LANES == 0 else 2 * LANES
    x = _ffn(x, lw["g_ffn_pre"], lw["w_ffn_in"], lw["w_ffn_out"], lw["g_ffn_post"], _tile(n, 512), th)
    return x, state, bufs


def _assemble_states(states, bufs, b, t):
    log_f, s_new, shift = (jnp.stack(e) for e in zip(*states))
    depth = bufs["kf"].shape[0]
    tok_major = lambda a: a.reshape(depth, b, FOX_HEADS, HEAD_DIM, t).transpose(0, 1, 4, 2, 3)
    fox_k, fox_v = tok_major(bufs["kf"]), tok_major(bufs["vf"])
    diff_k = tok_major(bufs["kd"]).reshape(depth, b, t, DIFF_HEADS, 2, HEAD_DIM)
    diff_v = bufs["vd"].reshape(depth, b, t, DIFF_HEADS, 2 * HEAD_DIM)
    return fox_k, fox_v, log_f, diff_k, diff_v, s_new, shift


def kernel(x_prompt, x_sample, mem_prompt, cache_fox_k, cache_fox_v, cache_fox_logf, cache_diff_k, cache_diff_v, state_rwkv, state_rwkv_shift, cache_mem_k, cache_mem_v, norm_mix_pre, norm_mix_post, w_in, fox_forget_bias, diff_lambda, diff_subln, rwkv_mu, rwkv_w0, rwkv_w2, rwkv_a0, rwkv_a2, rwkv_g2, rwkv_kk_scale, rwkv_ka, rwkv_rk, rwkv_ln_gain, rwkv_ln_bias, w_branch, w_out, norm_x_pre, norm_x_post, norm_mem, w_xq, w_xkv, w_xo, norm_ffn_pre, norm_ffn_post, w_ffn_in, w_ffn_out):
    wts = dict(norm_mix_pre=norm_mix_pre, norm_mix_post=norm_mix_post, w_in=w_in, fox_forget_bias=fox_forget_bias,
               diff_lambda=diff_lambda, diff_subln=diff_subln, rwkv_mu=rwkv_mu, rwkv_w0=rwkv_w0, rwkv_w2=rwkv_w2,
               rwkv_a0=rwkv_a0, rwkv_a2=rwkv_a2, rwkv_g2=rwkv_g2, rwkv_kk_scale=rwkv_kk_scale, rwkv_ka=rwkv_ka,
               rwkv_rk=rwkv_rk, rwkv_ln_gain=rwkv_ln_gain, rwkv_ln_bias=rwkv_ln_bias, w_branch=w_branch,
               w_out=w_out, norm_x_pre=norm_x_pre, norm_x_post=norm_x_post, norm_mem=norm_mem, w_xq=w_xq,
               w_xkv=w_xkv, w_xo=w_xo, norm_ffn_pre=norm_ffn_pre, norm_ffn_post=norm_ffn_post,
               w_ffn_in=w_ffn_in, w_ffn_out=w_ffn_out)
    depth = w_in.shape[0]
    bp, tp, d = x_prompt.shape
    bs, ts, _ = x_sample.shape
    p_len = cache_fox_k.shape[2]
    n_mem = mem_prompt.shape[1]
    w = BRANCH_W

    tabs_p = _rotary_tables(jnp.arange(tp, dtype=jnp.int32))
    tabs_s = _rotary_tables(p_len + jnp.arange(ts, dtype=jnp.int32))

    past_s = (cache_fox_k.transpose(0, 1, 3, 4, 2).reshape(depth, bs, w, p_len),
              cache_fox_v.transpose(0, 1, 3, 4, 2).reshape(depth, bs, w, p_len),
              cache_fox_logf.transpose(0, 1, 3, 2),
              cache_diff_k.transpose(0, 1, 3, 4, 5, 2).reshape(depth, bs, w, p_len),
              cache_diff_v.reshape(depth, bs, p_len * DIFF_HEADS, 2 * HEAD_DIM),
              state_rwkv, state_rwkv_shift)
    mem_rows = (n_mem * XATTN_HEADS, XATTN_HEAD_DIM)
    mem_k_s = cache_mem_k.reshape((depth, bs) + mem_rows)
    mem_v_s = cache_mem_v.reshape((depth, bs) + mem_rows)

    xp = x_prompt.reshape(bp * tp, d)
    xs = x_sample.reshape(bs * ts, d)
    mem = mem_prompt.reshape(bp * n_mem, d)
    p_new, s_new, p_mk, p_mv = [], [], [], []
    bufs_p = _new_state_buffers(depth, bp, tp)
    bufs_s = _new_state_buffers(depth, bs, ts)
    for l in range(depth):
        lw = _layer_weights(l, wts)
        mk, mv = _memkv(mem, lw["g_mem"], lw["w_xkv"], _tile(bp * n_mem, 256))
        xp, st_p, bufs_p = _layer(xp, lw, l, bp, tp, None, mk.reshape((1, bp) + mem_rows),
                                  mv.reshape((1, bp) + mem_rows), 0, tabs_p, bufs_p)
        xs, st_s, bufs_s = _layer(xs, lw, l, bs, ts, past_s, mem_k_s, mem_v_s, l, tabs_s, bufs_s)
        p_new.append(st_p)
        s_new.append(st_s)
        p_mk.append(mk.reshape(bp, n_mem, XATTN_HEADS, XATTN_HEAD_DIM))
        p_mv.append(mv.reshape(bp, n_mem, XATTN_HEADS, XATTN_HEAD_DIM))
    p_out = _assemble_states(p_new, bufs_p, bp, tp)
    s_out = _assemble_states(s_new, bufs_s, bs, ts)
    return (xp.reshape(bp, tp, d), xs.reshape(bs, ts, d)) + p_out + (jnp.stack(p_mk), jnp.stack(p_mv)) + s_out
```

```python
import functools
import math

import numpy as np
import jax
import jax.numpy as jnp
from jax import lax
from jax.experimental import pallas as pl
from jax.experimental.pallas import tpu as pltpu

F32 = jnp.float32
BF = jnp.bfloat16

CHUNK = 64
HEAD_DIM = 64
FOX_HEADS = 8
DIFF_HEADS = 4
RWKV_HEADS = 8
GROUPS = 8
BRANCH_W = 512
DECAY_LORA = 64
AAA_LORA = 64
GATE_LORA = 128
RWKV_IN = 3 * BRANCH_W + DECAY_LORA + AAA_LORA + GATE_LORA
ROT_DIM = HEAD_DIM // 4
ROPE_THETA = 500000.0
XATTN_HEADS = 4
XATTN_HEAD_DIM = 128
RMS_EPS = 1e-6
RWKV_GN_EPS = 64e-5
ATTN_SCALE = HEAD_DIM ** -0.5
LOG2E = math.log2(math.e)
QK_SCALE = ATTN_SCALE * LOG2E
D_PIECES = 3
NEG_BIG = -1e30

LANES = 128
SUBLANES = 8
VMEM_LIMIT_BYTES = 56 * 1024 * 1024


def _cparams(*sem):
    return pltpu.CompilerParams(dimension_semantics=sem, vmem_limit_bytes=VMEM_LIMIT_BYTES)


def _const_spec(shape):
    nd = len(shape)
    return pl.BlockSpec(shape, lambda *_: (0,) * nd)


def _rms(x, g):
    return x * lax.rsqrt(jnp.mean(x * x, axis=-1, keepdims=True) + RMS_EPS) * g


_NN = ((1,), (0,))
_NT = ((1,), (1,))
_TN = ((0,), (0,))


def _dg(a, b, dims):
    return lax.dot_general(a, b, (dims, ((), ())), preferred_element_type=F32)


def _dot(a, b):
    return _dg(a.astype(BF), b.astype(BF), _NN)


def _dot_nt(a, b):
    return _dg(a.astype(BF), b.astype(BF), _NT)


def _split2(x):
    hi = x.astype(BF)
    lo = (x - hi.astype(F32)).astype(BF)
    return hi, lo


def _split3(x):
    hi = x.astype(BF)
    r1 = x - hi.astype(F32)
    mid = r1.astype(BF)
    lo = (r1 - mid.astype(F32)).astype(BF)
    return hi, mid, lo


def _dot_sel(sel, x):
    hi, mid, lo = _split3(x)
    return _dg(sel, hi, _NN) + (_dg(sel, mid, _NN) + _dg(sel, lo, _NN))


def _dot_sel_right(x, sel):
    hi, lo = _split2(x)
    return _dg(hi, sel, _NN) + _dg(lo, sel, _NN)


def _softplus(x):
    return jnp.maximum(x, 0.0) + jnp.log(1.0 + jnp.exp(-jnp.abs(x)))


def _fox_inproj_kernel(x_ref, g_ref, wq_ref, qc_ref, wkv_t_ref, wf_t_ref, kbuf_ref, vbuf_ref,
                       q_ref, kt_ref, vt_ref, f_ref):
    del kbuf_ref, vbuf_ref
    xn = _rms(x_ref[...], g_ref[...]).astype(BF)
    w = BRANCH_W
    q_ref[...] = (_dg(xn, wq_ref[...], _NN) * QK_SCALE + qc_ref[...]).astype(BF)
    kt_ref[0, 0] = _dg(wkv_t_ref[0:w, :], xn, _NT)
    vt_ref[0, 0] = _dg(wkv_t_ref[w:2 * w, :], xn, _NT)
    f_ref[0] = _dg(wf_t_ref[...], xn, _NT)


def _row_specs(b, t, tm, layer):
    nt = t // tm
    row = lambda width: pl.BlockSpec((tm, width), lambda bi, ti: (bi * nt + ti, 0))
    col = lambda height: pl.BlockSpec((1, height, tm), lambda bi, ti: (bi, 0, ti))
    state = lambda height: pl.BlockSpec((1, 1, height, tm), lambda bi, ti: (layer, bi, 0, ti))
    return nt, row, col, state


_ANY = pl.BlockSpec(memory_space=pl.ANY)


def _fox_q_layout(wq):
    d = wq.shape[0]
    w3 = wq.reshape(d, FOX_HEADS, HEAD_DIM)
    wq_slab = jnp.concatenate([w3, jnp.zeros_like(w3)], axis=2).reshape(d, FOX_HEADS * LANES)
    slab = np.zeros((FOX_HEADS, LANES), np.float32)
    slab[:, HEAD_DIM:HEAD_DIM + D_PIECES] = -1.0
    return wq_slab, jnp.asarray(slab.reshape(1, FOX_HEADS * LANES))


def _fox_inproj(x, g, wq_slab, q_const, wkv_t, wf_t, kbuf, vbuf, layer, b, tm):
    n, d = x.shape
    t = n // b
    w = BRANCH_W
    wq_w = wq_slab.shape[1]
    nt, row, col, state = _row_specs(b, t, tm, layer)
    bufs, (kshape, vshape), alias = _state_operands(kbuf, vbuf)
    return pl.pallas_call(
        _fox_inproj_kernel,
        grid=(b, nt),
        in_specs=[row(d), _const_spec((1, d)), _const_spec((d, wq_w)), _const_spec((1, wq_w)),
                  _const_spec((2 * w, d)), _const_spec((FOX_HEADS, d)), _ANY, _ANY],
        out_specs=[row(wq_w), state(w), state(w), col(FOX_HEADS)],
        out_shape=[jax.ShapeDtypeStruct((n, wq_w), BF), jax.ShapeDtypeStruct(kshape, F32),
                   jax.ShapeDtypeStruct(vshape, F32), jax.ShapeDtypeStruct((b, FOX_HEADS, t), F32)],
        input_output_aliases={6: 1, 7: 2} if alias else {},
        compiler_params=_cparams("parallel", "parallel"),
    )(x, g, wq_slab, q_const, wkv_t, wf_t, *bufs)


def _diff_inproj_kernel(x_ref, g_ref, wq_ref, wk_t_ref, wv_ref, cos_ref, sa_ref, sb_ref, cos_t_ref, sin_t_ref,
                        kbuf_ref, vbuf_ref, q_ref, kt_ref, v_ref):
    del kbuf_ref, vbuf_ref
    kt_ref = kt_ref.at[0]
    tm = x_ref.shape[0]
    xn = _rms(x_ref[...], g_ref[...]).astype(BF)
    w = BRANCH_W
    half = ROT_DIM // 2
    cos, sa, sb = cos_ref[...], sa_ref[...], sb_ref[...]
    yq = _dg(xn, wq_ref[...], _NN)
    for j in range(w // LANES):
        yj = yq[:, j * LANES:(j + 1) * LANES]
        rj = yj * cos + pltpu.roll(yj, LANES - half, 1) * sa + pltpu.roll(yj, half, 1) * sb
        q_ref[:, j * LANES:(j + 1) * LANES] = (rj * QK_SCALE).astype(BF)
    yk = _dg(wk_t_ref[...], xn, _NT)
    cos_t, sin_t = cos_t_ref[...], sin_t_ref[...]
    for gi in range(GROUPS):
        r0 = gi * HEAD_DIM
        y1, y2 = yk[r0:r0 + half, :], yk[r0 + half:r0 + ROT_DIM, :]
        kt_ref[0, r0:r0 + half, :] = y1 * cos_t - y2 * sin_t
        kt_ref[0, r0 + half:r0 + ROT_DIM, :] = y2 * cos_t + y1 * sin_t
        kt_ref[0, r0 + ROT_DIM:r0 + HEAD_DIM, :] = yk[r0 + ROT_DIM:r0 + HEAD_DIM, :]
    yv = _dg(xn, wv_ref[...], _NN)
    dv = 2 * HEAD_DIM
    for h in range(DIFF_HEADS):
        v_ref[0, pl.ds(h, tm, stride=DIFF_HEADS), :] = yv[:, h * dv:(h + 1) * dv]


def _diff_inproj(x, g, wq, wk_t, wv, tabs, kbuf, vbuf, layer, b, tm):
    n, d = x.shape
    t = n // b
    w = BRANCH_W
    nt, row, col, state = _row_specs(b, t, tm, layer)
    row_tab = pl.BlockSpec((tm, LANES), lambda bi, ti: (ti, 0))
    col_tab = pl.BlockSpec((ROT_DIM // 2, tm), lambda bi, ti: (0, ti))
    vspec = pl.BlockSpec((1, DIFF_HEADS * tm, 2 * HEAD_DIM), lambda bi, ti: (layer, bi * nt + ti, 0))
    bufs, (kshape, vshape), alias = _state_operands(kbuf, vbuf)
    return pl.pallas_call(
        _diff_inproj_kernel,
        grid=(b, nt),
        in_specs=[row(d), _const_spec((1, d)), _const_spec((d, w)), _const_spec((w, d)), _const_spec((d, w)),
                  row_tab, row_tab, row_tab, col_tab, col_tab, _ANY, _ANY],
        out_specs=[row(w), state(w), vspec],
        out_shape=[jax.ShapeDtypeStruct((n, w), BF), jax.ShapeDtypeStruct(kshape, F32),
                   jax.ShapeDtypeStruct(vshape, F32)],
        input_output_aliases={10: 1, 11: 2} if alias else {},
        compiler_params=_cparams("parallel", "parallel"),
    )(x, g, wq, wk_t, wv, *tabs, *bufs)


def _rotary_tables(pos):
    half = ROT_DIM // 2
    inv = ROPE_THETA ** (-np.arange(half, dtype=np.float32) / half)
    ang = pos.astype(F32)[:, None] * jnp.asarray(inv, F32)[None, :]
    cos, sin = jnp.cos(ang), jnp.sin(ang)
    t = pos.shape[0]
    one = jnp.ones((t, HEAD_DIM - ROT_DIM), F32)
    zero = jnp.zeros((t, HEAD_DIM - ROT_DIM), F32)
    zh = jnp.zeros((t, half), F32)
    cos_h = jnp.concatenate([cos, cos, one], axis=1)
    sa_h = jnp.concatenate([-sin, zh, zero], axis=1)
    sb_h = jnp.concatenate([zh, sin, zero], axis=1)
    dup = lambda a: jnp.concatenate([a, a], axis=1)
    return dup(cos_h), dup(sa_h), dup(sb_h), cos.T, sin.T


def _rwkv_prep_kernel(x_ref, g_ref, wc_ref, shift0_ref, mu_ref, w0_ref, w2_ref, a0_ref, a2_ref, g2_ref,
                      kks_ref, ka_ref, rk_ref, seg_ref,
                      r_ref, lw_ref, km_ref, v_ref, kk_ref, kka_ref, gg_ref, bonus_ref, shift_ref,
                      carry_ref):
    t = pl.program_id(1)
    tm = x_ref.shape[0]
    w = BRANCH_W
    xn = _rms(x_ref[...], g_ref[...]).astype(BF)
    c = _dg(xn, wc_ref[...], _NN)

    @pl.when(t == 0)
    def _():
        carry_ref[...] = shift0_ref[0]

    row = lax.broadcasted_iota(jnp.int32, (tm, 1), 0)
    prev = jnp.where(row == 0, carry_ref[...], pltpu.roll(c, 1, 0))
    last = c[tm - 1:tm, :]
    carry_ref[...] = last
    shift_ref[0] = last

    xs = c + mu_ref[...] * (prev - c)
    cr, ck, cv = xs[:, 0:w], xs[:, w:2 * w], xs[:, 2 * w:3 * w]
    o = 3 * w
    xw = xs[:, o:o + DECAY_LORA]
    xa = xs[:, o + DECAY_LORA:o + DECAY_LORA + AAA_LORA]
    xg = xs[:, o + DECAY_LORA + AAA_LORA:]
    z = w0_ref[...] + _dot(jnp.tanh(xw), w2_ref[...])
    w_raw = -_softplus(-z) - 0.5
    a = jax.nn.sigmoid(a0_ref[...] + _dot(xa, a2_ref[...]))
    seg = seg_ref[...]
    kk = ck * kks_ref[...]
    kk = kk / jnp.maximum(jnp.sqrt(_dot_sel_right(kk * kk, seg)), 1e-12)
    kmod = ck * (1.0 + (a - 1.0) * ka_ref[...])
    r_ref[...] = cr
    lw_ref[...] = -jnp.exp(w_raw)
    km_ref[...] = kmod
    v_ref[...] = cv
    kk_ref[...] = kk
    kka_ref[...] = kk * a
    gg_ref[...] = _dot(jax.nn.sigmoid(xg), g2_ref[...])
    bonus_ref[...] = _dot_sel_right(cr * kmod * rk_ref[...], seg) * cv


def _rwkv_prep(x, g, wc, shift0, vecs, mats, seg, b, tm):
    n, d = x.shape
    t = n // b
    w = BRANCH_W
    mu, w0, a0, kks, ka, rk = vecs
    w2, a2, g2 = mats
    nt, row, _, _ = _row_specs(b, t, tm, 0)
    per_b = pl.BlockSpec((1, 1, RWKV_IN), lambda bi, ti: (bi, 0, 0))
    outs = [jax.ShapeDtypeStruct((n, w), F32)] * 8 + [jax.ShapeDtypeStruct((b, 1, RWKV_IN), F32)]
    return pl.pallas_call(
        _rwkv_prep_kernel,
        grid=(b, nt),
        in_specs=[row(d), _const_spec((1, d)), _const_spec((d, RWKV_IN)), per_b,
                  _const_spec((1, RWKV_IN)), _const_spec((1, w)), _const_spec((DECAY_LORA, w)),
                  _const_spec((1, w)), _const_spec((AAA_LORA, w)), _const_spec((GATE_LORA, w)),
                  _const_spec((1, w)), _const_spec((1, w)), _const_spec((1, w)), _const_spec((w, w))],
        out_specs=[row(w)] * 8 + [per_b],
        out_shape=outs,
        scratch_shapes=[pltpu.VMEM((1, RWKV_IN), F32)],
        compiler_params=_cparams("parallel", "arbitrary"),
    )(x, g, wc, shift0, mu, w0, w2, a0, a2, g2, kks, ka, rk, seg)


def _logf_cumsum_kernel(z_ref, bias_ref, lf_ref, d_ref, *, past, new):
    z = z_ref[0]
    width = z.shape[1]
    col = lax.broadcasted_iota(jnp.int32, z.shape, 1)
    zz = z + bias_ref[...]
    log_sig = jnp.minimum(zz, 0.0) - jnp.log(1.0 + jnp.exp(-jnp.abs(zz)))
    lf = jnp.where(col < past, z, jnp.where(col < past + new, log_sig, 0.0))
    lf_ref[0] = lf
    x = lf
    s = 1
    while s < width:
        x = x + jnp.where(col >= s, pltpu.roll(x, s, 1), 0.0)
        s *= 2
    d_ref[0] = x


def _logf_cumsum(z, bias, past, new):
    b, h, width = z.shape
    blk = pl.BlockSpec((1, h, width), lambda i: (i, 0, 0))
    return pl.pallas_call(
        functools.partial(_logf_cumsum_kernel, past=past, new=new),
        grid=(b,),
        in_specs=[blk, _const_spec((h, 1))],
        out_specs=[blk, blk],
        out_shape=[jax.ShapeDtypeStruct(z.shape, F32)] * 2,
        compiler_params=_cparams("parallel"),
    )(z, bias)


def _softmax_update(s, m_ref, l_ref, idx):
    m_prev = m_ref[idx]
    m_new = jnp.maximum(m_prev, jnp.max(s, axis=-1, keepdims=True))
    alpha = jnp.exp2(m_prev - m_new)
    p = jnp.exp2(s - m_new)
    l_ref[idx] = alpha * l_ref[idx] + jnp.sum(p, axis=-1, keepdims=True)
    m_ref[idx] = m_new
    return p.astype(BF), alpha


def _init_softmax_state(m_ref, l_ref, acc_ref):
    m_ref[...] = jnp.full(m_ref.shape, NEG_BIG, F32)
    l_ref[...] = jnp.zeros(l_ref.shape, F32)
    acc_ref[...] = jnp.zeros(acc_ref.shape, F32)


def _row_max_update(s, m_ref, idx):
    reps = s.shape[1] // LANES
    m_prev = m_ref[idx]
    m_new = jnp.maximum(m_prev, jnp.max(s, axis=-1, keepdims=True))
    alpha = jnp.exp2(m_prev - m_new)
    p = jnp.exp2(s - jnp.concatenate([m_new] * reps, axis=1))
    m_ref[idx] = m_new
    return p.astype(BF), alpha


def _init_prompt_state(m_ref, acc_ref):
    m_ref[...] = jnp.full(m_ref.shape, NEG_BIG, F32)
    acc_ref[...] = jnp.zeros(acc_ref.shape, F32)


def _causal_tile_pairs(nq):
    pairs = [(i, j) for i in range(nq) for j in range(i + 1)]
    return (jnp.asarray([p[0] for p in pairs], jnp.int32), jnp.asarray([p[1] for p in pairs], jnp.int32))


def _causal_blocks(i, j, block):
    pl.when(j < i)(functools.partial(block, False))
    pl.when(j == i)(functools.partial(block, True))


def _fox_prompt_kernel(it_ref, jt_ref, q_ref, kt_ref, vt_ref, dk_ref, o_ref, m_ref, acc_ref):
    pair = pl.program_id(1)
    i, j = it_ref[pair], jt_ref[pair]
    tq, tk = q_ref.shape[0], kt_ref.shape[3]
    heads = [slice(h * HEAD_DIM, (h + 1) * HEAD_DIM) for h in range(FOX_HEADS)]

    @pl.when(j == 0)
    def _():
        _init_prompt_state(m_ref, acc_ref)

    def block(masked):
        kb = kt_ref[0, 0].astype(BF)
        vb = vt_ref[0, 0].astype(BF)
        ones = jnp.ones((HEAD_DIM, tk), BF)
        d_hi, d_mid, d_lo = _split3(dk_ref[0] * LOG2E)
        rows = 2 * SUBLANES
        row = lax.broadcasted_iota(jnp.int32, (rows, tk), 0)
        pad = jnp.zeros((LANES - HEAD_DIM - rows, tk), BF)
        k_ops = []
        for h, hs in enumerate(heads):
            pieces = [jnp.broadcast_to(a[h:h + 1, :].astype(F32), (rows, tk)) for a in (d_hi, d_mid, d_lo)]
            extra = jnp.where(row == 0, pieces[0], jnp.where(row == 1, pieces[1], jnp.where(row == 2, pieces[2], 0.0)))
            k_ops.append(jnp.concatenate([kb[hs, :], extra.astype(BF), pad], axis=0))
        scores = [_dg(q_ref[:, h * LANES:(h + 1) * LANES], k_ops[h], _NN) for h in range(FOX_HEADS)]
        if masked:
            visible = (lax.broadcasted_iota(jnp.int32, (tq, tk), 1) <= lax.broadcasted_iota(jnp.int32, (tq, tk), 0))
        for h, hs in enumerate(heads):
            s = jnp.where(visible, scores[h], NEG_BIG) if masked else scores[h]
            p, alpha = _row_max_update(s, m_ref, h)
            v_ones = jnp.concatenate([vb[hs, :], ones], axis=0)
            acc_ref[h] = alpha * acc_ref[h] + _dg(p, v_ones, _NT)

    _causal_blocks(i, j, block)

    @pl.when(j == i)
    def _():
        for h, hs in enumerate(heads):
            acc = acc_ref[h]
            o_ref[:, hs] = acc[:, 0:HEAD_DIM] / acc[:, HEAD_DIM:2 * HEAD_DIM]


def _fox_prompt(q, kt, vt, d_row, layer, b, tq):
    n, wq = q.shape
    w = BRANCH_W
    t = n // b
    nq = t // tq
    i_tab, j_tab = _causal_tile_pairs(nq)
    qspec = lambda width: pl.BlockSpec((tq, width), lambda bi, p, it, jt: (bi * nq + it[p], 0))
    kspec = pl.BlockSpec((1, 1, w, tq), lambda bi, p, it, jt: (layer, bi, 0, jt[p]))
    grid_spec = pltpu.PrefetchScalarGridSpec(
        num_scalar_prefetch=2,
        grid=(b, i_tab.shape[0]),
        in_specs=[qspec(wq), kspec, kspec,
                  pl.BlockSpec((1, FOX_HEADS, tq), lambda bi, p, it, jt: (bi, 0, jt[p]))],
        out_specs=qspec(w),
        scratch_shapes=[pltpu.VMEM((FOX_HEADS, tq, LANES), F32), pltpu.VMEM((FOX_HEADS, tq, 2 * HEAD_DIM), F32)],
    )
    return pl.pallas_call(
        _fox_prompt_kernel,
        grid_spec=grid_spec,
        out_shape=jax.ShapeDtypeStruct((n, w), F32),
        compiler_params=_cparams("parallel", "arbitrary"),
    )(i_tab, j_tab, q, kt, vt, d_row)


def _diff_lambda(lam_ref, lam_init):
    p = lam_ref[...]
    s1 = jnp.sum(p[0:1, :] * p[1:2, :], axis=-1, keepdims=True)
    s2 = jnp.sum(p[2:3, :] * p[3:4, :], axis=-1, keepdims=True)
    return jnp.exp(s1) - jnp.exp(s2) + lam_init


def _diff_combine(acc0, l0, acc1, l1, lam, subln, lam_init):
    o = acc0 / l0 - lam * (acc1 / l1)
    return _rms(o, subln) * (1.0 - lam_init)


def _diff_prompt_kernel(it_ref, jt_ref, q_ref, kt_ref, v_ref, lam_ref, subln_ref, o_ref, m_ref, acc_ref, *,
                        lam_init):
    pair = pl.program_id(1)
    i, j = it_ref[pair], jt_ref[pair]
    tq, tk = q_ref.shape[0], kt_ref.shape[3]
    dv = 2 * HEAD_DIM

    @pl.when(j == 0)
    def _():
        _init_prompt_state(m_ref, acc_ref)

    def block(masked):
        kb = kt_ref[0, 0].astype(BF)
        ones = jnp.ones((tk, dv), BF)
        groups = [slice(gi * HEAD_DIM, (gi + 1) * HEAD_DIM) for gi in range(GROUPS)]
        scores = [_dg(q_ref[:, gs], kb[gs, :], _NN) for gs in groups]
        if masked:
            visible = ((lax.broadcasted_iota(jnp.int32, (tq, tk), 1) // CHUNK)
                       <= (lax.broadcasted_iota(jnp.int32, (tq, tk), 0) // CHUNK))
        for h in range(DIFF_HEADS):
            vh = v_ref[0, pl.ds(h, tk, stride=DIFF_HEADS), :].astype(BF)
            v_ones = jnp.concatenate([vh, ones], axis=1)
            for gi in (2 * h, 2 * h + 1):
                s = jnp.where(visible, scores[gi], NEG_BIG) if masked else scores[gi]
                p, alpha = _row_max_update(s, m_ref, gi)
                acc_ref[gi] = jnp.concatenate([alpha, alpha], axis=1) * acc_ref[gi] + _dg(p, v_ones, _NN)

    _causal_blocks(i, j, block)

    @pl.when(j == i)
    def _():
        lam = _diff_lambda(lam_ref, lam_init)
        for h in range(DIFF_HEADS):
            a0, a1 = acc_ref[2 * h], acc_ref[2 * h + 1]
            o_ref[:, h * dv:(h + 1) * dv] = _diff_combine(a0[:, 0:dv], a0[:, dv:], a1[:, 0:dv], a1[:, dv:], lam,
                                                          subln_ref[...], lam_init)


def _diff_prompt(q, kt, v, lam_p, subln, lam_init, layer, b, tq):
    n, w = q.shape
    t = n // b
    nq = t // tq
    i_tab, j_tab = _causal_tile_pairs(nq)
    qspec = pl.BlockSpec((tq, w), lambda bi, p, it, jt: (bi * nq + it[p], 0))
    ktspec = pl.BlockSpec((1, 1, w, tq), lambda bi, p, it, jt: (layer, bi, 0, jt[p]))
    vspec = pl.BlockSpec((1, DIFF_HEADS * tq, 2 * HEAD_DIM), lambda bi, p, it, jt: (layer, bi * nq + jt[p], 0))
    grid_spec = pltpu.PrefetchScalarGridSpec(
        num_scalar_prefetch=2,
        grid=(b, i_tab.shape[0]),
        in_specs=[qspec, ktspec, vspec, _const_spec((4, HEAD_DIM)), _const_spec((1, 2 * HEAD_DIM))],
        out_specs=qspec,
        scratch_shapes=[pltpu.VMEM((GROUPS, tq, LANES), F32), pltpu.VMEM((GROUPS, tq, 4 * HEAD_DIM), F32)],
    )
    return pl.pallas_call(
        functools.partial(_diff_prompt_kernel, lam_init=lam_init),
        grid_spec=grid_spec,
        out_shape=jax.ShapeDtypeStruct((n, w), F32),
        compiler_params=_cparams("parallel", "arbitrary"),
    )(i_tab, j_tab, q, kt, v, lam_p, subln)


def _sample_attn_kernel(*refs, fox, t_new, lam_init):
    if fox:
        q_ref, kc_ref, vc_ref, kn_ref, vn_ref, dq_ref, dkc_ref, dkn_ref, o_ref, m_ref, l_ref, acc_ref = refs
    else:
        q_ref, kc_ref, vc_ref, kn_ref, vn_ref, lam_ref, subln_ref, o_ref, m_ref, l_ref, acc_ref = refs
    j = pl.program_id(1)
    dv = 2 * HEAD_DIM
    pair = 2 * t_new

    @pl.when(j == 0)
    def _():
        _init_softmax_state(m_ref, l_ref, acc_ref)

    def expand(dk):
        return jnp.concatenate([jnp.broadcast_to(dk[gi:gi + 1, :], (t_new, dk.shape[1])) for gi in range(GROUPS)],
                               axis=0)

    def scores(kt, dk, causal):
        s = _dg(q_ref[0], kt.astype(BF), _NN)
        if fox:
            s = s + dq_ref[0] * LOG2E - expand(dk * LOG2E)
        if causal:
            qi = lax.broadcasted_iota(jnp.int32, s.shape, 0) % t_new
            kj = lax.broadcasted_iota(jnp.int32, s.shape, 1)
            s = jnp.where(kj <= qi, s, NEG_BIG)
        return _softmax_update(s, m_ref, l_ref, 0)

    def accumulate_diff(p, alpha, value_of_head):
        for h in range(DIFF_HEADS):
            rs = slice(h * pair, (h + 1) * pair)
            acc_ref[0, rs, :] = alpha[rs] * acc_ref[0, rs, :] + _dg(p[rs], value_of_head(h).astype(BF), _NN)

    tk = kc_ref.shape[3]
    if fox:
        p, alpha = scores(kc_ref[0, 0], dkc_ref[0], False)
        acc_ref[0] = alpha * acc_ref[0] + _dg(p, vc_ref[0, 0].astype(BF), _NT)
    else:
        p, alpha = scores(kc_ref[0, 0], None, False)
        accumulate_diff(p, alpha, lambda h: vc_ref[0, 0, pl.ds(h, tk, stride=DIFF_HEADS), :])

    @pl.when(j == pl.num_programs(1) - 1)
    def _():
        if fox:
            p, alpha = scores(kn_ref[0, 0], dkn_ref[0][:, 0:t_new], True)
            acc = alpha * acc_ref[0] + _dg(p, vn_ref[0, 0].astype(BF), _NT)
            l = l_ref[0]
            for gi in range(GROUPS):
                rs = slice(gi * t_new, (gi + 1) * t_new)
                cs = slice(gi * HEAD_DIM, (gi + 1) * HEAD_DIM)
                o_ref[:, cs] = acc[rs, cs] / l[rs]
        else:
            p, alpha = scores(kn_ref[0, 0], None, False)
            accumulate_diff(p, alpha, lambda h: vn_ref[0, pl.ds(h, t_new, stride=DIFF_HEADS), :])
            acc = acc_ref[0]
            l = l_ref[0]
            lam = _diff_lambda(lam_ref, lam_init)
            for h in range(DIFF_HEADS):
                r0 = slice(h * pair, h * pair + t_new)
                r1 = slice(h * pair + t_new, (h + 1) * pair)
                o_ref[:, h * dv:(h + 1) * dv] = _diff_combine(acc[r0], l[r0], acc[r1], l[r1], lam, subln_ref[...],
                                                              lam_init)


def _sample_attn(qbd, kc, vc, kn, vn, extra, *, layer, fox, b, t_new, past, tk, lam_init=0.0):
    w = BRANCH_W
    rows = GROUPS * t_new
    nk = past // tk
    qspec = pl.BlockSpec((1, rows, w), lambda bi, j: (bi, 0, 0))
    kcspec = pl.BlockSpec((1, 1, w, tk), lambda bi, j: (layer, bi, 0, j))
    knspec = pl.BlockSpec((1, 1, w, t_new), lambda bi, j: (layer, bi, 0, 0))
    ospec = pl.BlockSpec((t_new, w), lambda bi, j: (bi, 0))
    if fox:
        dq, d_row = extra
        vcspec, vnspec, acc_w = kcspec, knspec, w
        especs = [pl.BlockSpec((1, rows, 1), lambda bi, j: (bi, 0, 0)),
                  pl.BlockSpec((1, GROUPS, tk), lambda bi, j: (bi, 0, j)),
                  pl.BlockSpec((1, GROUPS, LANES), lambda bi, j: (bi, 0, past // LANES))]
        eargs = [dq, d_row, d_row]
    else:
        vcspec = pl.BlockSpec((1, 1, DIFF_HEADS * tk, 2 * HEAD_DIM), lambda bi, j: (layer, bi, j, 0))
        vnspec = pl.BlockSpec((1, DIFF_HEADS * t_new, 2 * HEAD_DIM), lambda bi, j: (layer, bi, 0))
        acc_w = 2 * HEAD_DIM
        especs = [_const_spec((4, HEAD_DIM)), _const_spec((1, 2 * HEAD_DIM))]
        eargs = list(extra)
    return pl.pallas_call(
        functools.partial(_sample_attn_kernel, fox=fox, t_new=t_new, lam_init=lam_init),
        grid=(b, nk),
        in_specs=[qspec, kcspec, vcspec, knspec, vnspec] + especs,
        out_specs=ospec,
        out_shape=jax.ShapeDtypeStruct((b * t_new, w), F32),
        scratch_shapes=[pltpu.VMEM((1, rows, 1), F32), pltpu.VMEM((1, rows, 1), F32),
                        pltpu.VMEM((1, rows, acc_w), F32)],
        compiler_params=_cparams("parallel", "arbitrary"),
    )(qbd, kc, vc, kn, vn, *eargs)


def _block_diag_queries(q, b, t_new):
    rows, w = GROUPS * t_new, GROUPS * HEAD_DIM
    tiled = jnp.tile(q.reshape(b, t_new, w), (1, GROUPS, 1))
    row_group = lax.broadcasted_iota(jnp.int32, (rows, w), 0) // t_new
    col_group = lax.broadcasted_iota(jnp.int32, (rows, w), 1) // HEAD_DIM
    return jnp.where((row_group == col_group)[None], tiled, jnp.zeros_like(tiled))


def _rwkv_scan_kernel(r_ref, lw_ref, k_ref, v_ref, kk_ref, kka_ref, gg_ref, bonus_ref, gain_ref, bias_ref,
                      s0_ref, o_ref, sfin_ref, s_ref):
    c = pl.program_id(1)
    nb, ch = r_ref.shape[0], r_ref.shape[1]
    n = HEAD_DIM

    @pl.when(c == 0)
    def _():
        s_ref[...] = s0_ref[...]

    ri = lax.broadcasted_iota(jnp.int32, (ch, ch), 0)
    ci = lax.broadcasted_iota(jnp.int32, (ch, ch), 1)
    tri = (ci <= ri).astype(BF)
    ri2 = lax.broadcasted_iota(jnp.int32, (ch, 2 * ch), 0)
    ci2 = lax.broadcasted_iota(jnp.int32, (ch, 2 * ch), 1)
    ci2m = jnp.where(ci2 >= ch, ci2 - ch, ci2)
    strict_right = (ci2 >= ch) & (ci2m < ri2)
    strict_left = ci < ri
    is_x = lax.broadcasted_iota(jnp.int32, (ch, ch + n), 1) >= ch
    incl_both = ci2m <= ri2
    steps = max(1, int(math.ceil(math.log2(ch))))
    heads = [slice(h * n, (h + 1) * n) for h in range(RWKV_HEADS)]

    prep = []
    for bi in range(nb):
        lw = lw_ref[bi]
        cum = _dot_sel(tri, lw)
        cum_end = cum[ch - 1:ch, :]
        kk, kka, kmod, v_all = kk_ref[bi], kka_ref[bi], k_ref[bi], v_ref[bi]
        e_neg = jnp.exp(-cum)
        e_tail = jnp.exp(cum_end - cum)
        prep.append(dict(
            ar=_split2(jnp.concatenate([-kk * jnp.exp(cum - lw), r_ref[bi] * jnp.exp(cum)], axis=0)),
            bk=_split2(jnp.concatenate([kka * e_neg, kmod * e_neg], axis=0)),
            bk_tail=_split2(jnp.concatenate([kka * e_tail, kmod * e_tail], axis=0)),
            vv=_split2(jnp.concatenate([v_all, v_all], axis=0)),
            v=v_all, decay_end=jnp.exp(cum_end)))
    units = [(bi, h, heads[h]) for bi in range(nb) for h in range(RWKV_HEADS)]
    s_old = [s_ref[bi, h] for bi, h, _ in units]

    def dot3s(a2, b2, dims):
        (ah, al), (bh, bl) = a2, b2
        m = ah.shape[0]
        r = _dg(jnp.concatenate([ah, al], axis=0), bh, dims)
        return r[0:m] + r[m:] + _dg(ah, bl, dims)

    def dot3_tn(a2, b2):
        (ah, al), (bh, bl) = a2, b2
        return _dg(ah, bh, _TN) + (_dg(ah, bl, _TN) + _dg(al, bh, _TN))

    def nil_times(zh):
        a2, b2 = _split2(zh[:, 0:ch]), _split2(zh)
        wz = zh.shape[1]
        if wz % LANES:
            return dot3s(a2, b2, _NN)
        r = _dg(jnp.concatenate(a2, axis=0), jnp.concatenate(b2, axis=1), _NN)
        return (r[0:ch, 0:wz] + r[ch:, 0:wz]) + (r[0:ch, wz:] + r[ch:, wz:])

    cols = lambda a2, hs: (a2[0][:, hs], a2[1][:, hs])
    ga = []
    for u, (bi, h, hs) in enumerate(units):
        sh, sl = _split2(s_old[u])
        bk = prep[bi]["bk"]
        rhs = (jnp.concatenate([bk[0][:, hs], sh], axis=0), jnp.concatenate([bk[1][:, hs], sl], axis=0))
        ga.append(dot3s(cols(prep[bi]["ar"], hs), rhs, _NT))
    z = []
    for u, (bi, h, hs) in enumerate(units):
        g_top = ga[u][0:ch, 0:2 * ch]
        x0 = ga[u][0:ch, 2 * ch:] + dot3s(_split2(jnp.where(strict_right, g_top, 0.0)), cols(prep[bi]["vv"], hs), _NN)
        z.append(jnp.concatenate([jnp.where(strict_left, g_top[:, 0:ch], 0.0), x0], axis=1))
    for _ in range(steps):
        z = [nil_times(zh) + jnp.where(is_x, zh, 0.0) for zh in z]
    for u, (bi, h, hs) in enumerate(units):
        x = z[u][:, ch:]
        xv = _split2(jnp.concatenate([x, prep[bi]["v"][:, hs]], axis=0))
        g_bot = jnp.where(incl_both, ga[u][ch:, 0:2 * ch], 0.0)
        y = ga[u][ch:, 2 * ch:] + _dg(g_bot.astype(BF), xv[0], _NN)
        s_ref[bi, h] = s_old[u] * prep[bi]["decay_end"][:, hs] + dot3_tn(xv, cols(prep[bi]["bk_tail"], hs))

        mu = jnp.mean(y, axis=-1, keepdims=True)
        yc = y - mu
        var = jnp.mean(yc * yc, axis=-1, keepdims=True)
        yn = yc * lax.rsqrt(var + RWKV_GN_EPS) * gain_ref[:, hs] + bias_ref[:, hs]
        o_ref[bi, :, hs] = (yn + bonus_ref[bi, :, hs]) * gg_ref[bi, :, hs]

    @pl.when(c == pl.num_programs(1) - 1)
    def _():
        sfin_ref[...] = s_ref[...]


def _rwkv_scan(r, lw, km, v, kk, kka, gg, bonus, gain, bias, s0, b, ch, nb):
    n, w = r.shape
    t = n // b
    rows = [a.reshape(b, t, w) for a in (r, lw, km, v, kk, kka, gg, bonus)]
    row = pl.BlockSpec((nb, ch, w), lambda bi, ci: (bi, ci, 0))
    st = pl.BlockSpec((nb, RWKV_HEADS, HEAD_DIM, HEAD_DIM), lambda bi, ci: (bi, 0, 0, 0))
    out, s_fin = pl.pallas_call(
        _rwkv_scan_kernel,
        grid=(b // nb, t // ch),
        in_specs=[row] * 8 + [_const_spec((1, w)), _const_spec((1, w)), st],
        out_specs=[row, st],
        out_shape=[jax.ShapeDtypeStruct((b, t, w), F32), jax.ShapeDtypeStruct(s0.shape, F32)],
        scratch_shapes=[pltpu.VMEM((nb, RWKV_HEADS, HEAD_DIM, HEAD_DIM), F32)],
        compiler_params=_cparams("parallel", "arbitrary"),
    )(*rows, gain, bias, s0)
    return out.reshape(n, w), s_fin


def _merge_kernel(x_ref, oa_ref, ob_ref, oc_ref, gpre_ref, wg_ref, wb_ref, wo_ref, gpost_ref, y_ref):
    x = x_ref[...]
    d = x.shape[1]
    xn = _rms(x, gpre_ref[...]).astype(BF)
    acc = jnp.zeros(x.shape, F32)
    for i, o_ref in enumerate((oa_ref, ob_ref, oc_ref)):
        gate = jax.nn.sigmoid(_dg(xn, wg_ref[:, i * d:(i + 1) * d], _NN))
        acc = acc + gate * _dot(o_ref[...], wb_ref[i])
    y = _dot(acc, wo_ref[...])
    y_ref[...] = x + _rms(y, gpost_ref[...])


def _merge(x, oa, ob, oc, gpre, wg, wb, wo, gpost, tm):
    n, d = x.shape
    w = BRANCH_W
    row = lambda width: pl.BlockSpec((tm, width), lambda i: (i, 0))
    return pl.pallas_call(
        _merge_kernel,
        grid=(n // tm,),
        in_specs=[row(d), row(w), row(w), row(w), _const_spec((1, d)), _const_spec((d, 3 * d)),
                  _const_spec((3, w, d)), _const_spec((d, d)), _const_spec((1, d))],
        out_specs=row(d),
        out_shape=jax.ShapeDtypeStruct((n, d), F32),
        compiler_params=_cparams("parallel"),
    )(x, oa, ob, oc, gpre, wg, wb, wo, gpost)


def _memkv_kernel(m_ref, g_ref, w_ref, k_ref, v_ref):
    xn = _rms(m_ref[...], g_ref[...]).astype(BF)
    tm = m_ref.shape[0]
    hd = XATTN_HEAD_DIM
    w = XATTN_HEADS * hd
    yk = _dg(xn, w_ref[:, 0:w], _NN)
    yv = _dg(xn, w_ref[:, w:2 * w], _NN)
    for h in range(XATTN_HEADS):
        k_ref[pl.ds(h, tm, stride=XATTN_HEADS), :] = yk[:, h * hd:(h + 1) * hd]
        v_ref[pl.ds(h, tm, stride=XATTN_HEADS), :] = yv[:, h * hd:(h + 1) * hd]


def _memkv(mem, g, w_xkv, tm):
    n, d = mem.shape
    w = w_xkv.shape[1] // 2
    out = pl.BlockSpec((XATTN_HEADS * tm, XATTN_HEAD_DIM), lambda i: (i, 0))
    return pl.pallas_call(
        _memkv_kernel,
        grid=(n // tm,),
        in_specs=[pl.BlockSpec((tm, d), lambda i: (i, 0)), _const_spec((1, d)), _const_spec((d, 2 * w))],
        out_specs=[out, out],
        out_shape=[jax.ShapeDtypeStruct((XATTN_HEADS * n, XATTN_HEAD_DIM), F32)] * 2,
        compiler_params=_cparams("parallel"),
    )(mem, g, w_xkv)


def _xattn_kernel(x_ref, gpre_ref, wq_ref, mk_ref, mv_ref, wo_ref, gpost_ref, y_ref):
    x = x_ref[...]
    xn = _rms(x, gpre_ref[...])
    q = _dot(xn, wq_ref[...])
    hd = XATTN_HEAD_DIM
    n_mem = mk_ref.shape[2] // XATTN_HEADS
    outs = []
    for h in range(XATTN_HEADS):
        hs = slice(h * hd, (h + 1) * hd)
        mem_rows = pl.ds(h, n_mem, stride=XATTN_HEADS)
        s = _dot_nt(q[:, hs], mk_ref[0, 0, mem_rows, :]) * (hd ** -0.5)
        p = jnp.exp(s - jnp.max(s, axis=-1, keepdims=True))
        p = p / jnp.sum(p, axis=-1, keepdims=True)
        outs.append(_dot(p, mv_ref[0, 0, mem_rows, :]))
    o = jnp.concatenate(outs, axis=-1)
    y_ref[...] = x + _rms(_dot(o, wo_ref[...]), gpost_ref[...])


def _xattn(x, gpre, wq, mk, mv, wo, gpost, layer, b, tm):
    n, d = x.shape
    t = n // b
    nt = t // tm
    w = wq.shape[1]
    row = pl.BlockSpec((tm, d), lambda bi, ti: (bi * nt + ti, 0))
    mem = pl.BlockSpec((1, 1) + mk.shape[2:], lambda bi, ti: (layer, bi, 0, 0))
    return pl.pallas_call(
        _xattn_kernel,
        grid=(b, nt),
        in_specs=[row, _const_spec((1, d)), _const_spec((d, w)), mem, mem, _const_spec((w, d)),
                  _const_spec((1, d))],
        out_specs=row,
        out_shape=jax.ShapeDtypeStruct((n, d), F32),
        compiler_params=_cparams("parallel", "parallel"),
    )(x, gpre, wq, mk, mv, wo, gpost)


def _ffn_kernel(x_ref, gpre_ref, wu_ref, wv_ref, wo_ref, gpost_ref, y_ref, xn_ref, acc_ref):
    j = pl.program_id(1)

    @pl.when(j == 0)
    def _():
        xn_ref[...] = _rms(x_ref[...], gpre_ref[...]).astype(BF)
        acc_ref[...] = jnp.zeros(acc_ref.shape, F32)

    xn = xn_ref[...]
    u = _dg(xn, wu_ref[...], _NN)
    v = _dg(xn, wv_ref[...], _NN)
    acc_ref[...] += _dot(u * jax.nn.sigmoid(u) * v, wo_ref[...])

    @pl.when(j == pl.num_programs(1) - 1)
    def _():
        y_ref[...] = x_ref[...] + _rms(acc_ref[...], gpost_ref[...])


def _ffn(x, gpre, w_in, w_out, gpost, tm, th):
    n, d = x.shape
    hidden = w_out.shape[0]
    nh = hidden // th
    row = pl.BlockSpec((tm, d), lambda i, j: (i, 0))
    return pl.pallas_call(
        _ffn_kernel,
        grid=(n // tm, nh),
        in_specs=[row, _const_spec((1, d)),
                  pl.BlockSpec((d, th), lambda i, j: (0, j)),
                  pl.BlockSpec((d, th), lambda i, j: (0, nh + j)),
                  pl.BlockSpec((th, d), lambda i, j: (j, 0)),
                  _const_spec((1, d))],
        out_specs=row,
        out_shape=jax.ShapeDtypeStruct((n, d), F32),
        scratch_shapes=[pltpu.VMEM((tm, d), BF), pltpu.VMEM((tm, d), F32)],
        compiler_params=_cparams("parallel", "arbitrary"),
    )(x, gpre, w_in, w_in, w_out, gpost)


def _tile(n, pref):
    t = min(n, pref)
    assert n % t == 0, (n, t)
    return t


def _layer_weights(l, wts):
    p = {k: v[l] for k, v in wts.items()}
    w = BRANCH_W
    w_in = p["w_in"].astype(BF)
    w_in_t = p["w_in"].T.astype(BF)
    gate_w = w_in.shape[1] - (6 * w + FOX_HEADS + RWKV_IN)
    o_fox, o_f, o_diff = 0, 3 * w, 3 * w + FOX_HEADS
    o_rwkv = o_diff + 3 * w
    o_gate = o_rwkv + RWKV_IN
    row = lambda a: a.reshape(1, -1)
    seg = np.kron(np.eye(RWKV_HEADS, dtype=np.float32), np.ones((HEAD_DIM, HEAD_DIM), np.float32))
    return dict(
        g_mix_pre=row(p["norm_mix_pre"]), g_mix_post=row(p["norm_mix_post"]),
        w_fox_q=_fox_q_layout(w_in[:, o_fox:o_fox + w]), w_fox_kv_t=w_in_t[o_fox + w:o_fox + 3 * w],
        w_fox_f_t=w_in_t[o_f:o_f + FOX_HEADS],
        w_diff_q=w_in[:, o_diff:o_diff + w], w_diff_k_t=w_in_t[o_diff + w:o_diff + 2 * w],
        w_diff_v=w_in[:, o_diff + 2 * w:o_diff + 3 * w],
        w_rwkv=w_in[:, o_rwkv:o_rwkv + RWKV_IN], w_gate=w_in[:, o_gate:o_gate + gate_w],
        fox_bias=p["fox_forget_bias"].reshape(FOX_HEADS, 1),
        diff_lambda=p["diff_lambda"], diff_subln=row(p["diff_subln"]),
        rwkv_vecs=tuple(row(p[k]) for k in ("rwkv_mu", "rwkv_w0", "rwkv_a0", "rwkv_kk_scale", "rwkv_ka",
                                            "rwkv_rk")),
        rwkv_mats=tuple(p[k].astype(BF) for k in ("rwkv_w2", "rwkv_a2", "rwkv_g2")),
        rwkv_seg=jnp.asarray(seg, BF),
        rwkv_gain=row(p["rwkv_ln_gain"]), rwkv_bias=row(p["rwkv_ln_bias"]),
        w_branch=p["w_branch"].astype(BF), w_out=p["w_out"].astype(BF),
        g_x_pre=row(p["norm_x_pre"]), g_x_post=row(p["norm_x_post"]), g_mem=row(p["norm_mem"]),
        w_xq=p["w_xq"].astype(BF), w_xkv=p["w_xkv"].astype(BF), w_xo=p["w_xo"].astype(BF),
        g_ffn_pre=row(p["norm_ffn_pre"]), g_ffn_post=row(p["norm_ffn_post"]),
        w_ffn_in=p["w_ffn_in"].astype(BF), w_ffn_out=p["w_ffn_out"].astype(BF),
    )


def _new_state_buffers(depth, b, t):
    kt = (depth, b, BRANCH_W, t)
    return dict(kf=kt, vf=kt, kd=kt, vd=(depth, b * t * DIFF_HEADS, 2 * HEAD_DIM))


def _state_operands(kbuf, vbuf):
    if isinstance(kbuf, tuple):
        placeholder = jnp.zeros((SUBLANES, LANES), F32)
        return (placeholder, placeholder), (kbuf, vbuf), False
    return (kbuf, vbuf), (kbuf.shape, vbuf.shape), True


def _mixer(x, lw, l, b, t, past, tabs, bufs):
    n = x.shape[0]
    lam_init = 0.8 - 0.6 * math.exp(-0.3 * l)
    tm = _tile(t, 512)
    qf, kf_t, vf_t, f_bt = _fox_inproj(x, lw["g_mix_pre"], *lw["w_fox_q"], lw["w_fox_kv_t"], lw["w_fox_f_t"],
                                       bufs["kf"], bufs["vf"], l, b, tm)
    qd, kd_t, vd = _diff_inproj(x, lw["g_mix_pre"], lw["w_diff_q"], lw["w_diff_k_t"], lw["w_diff_v"], tabs,
                                bufs["kd"], bufs["vd"], l, b, tm)
    bufs = dict(kf=kf_t, vf=vf_t, kd=kd_t, vd=vd)

    if past is None:
        p_len = 0
        z = f_bt
        shift0 = jnp.zeros((b, 1, RWKV_IN), F32)
        s0 = jnp.zeros((b, RWKV_HEADS, HEAD_DIM, HEAD_DIM), F32)
    else:
        fox_kt, fox_vt, fox_lf_t, diff_kt, diff_v, s0, shift0 = past
        p_len = fox_kt.shape[3]
        pad = (-(p_len + t)) % LANES
        z = jnp.concatenate([fox_lf_t[l], f_bt, jnp.zeros((b, FOX_HEADS, pad), F32)], axis=2)
        s0 = s0[l]
        shift0 = shift0[l].reshape(b, 1, RWKV_IN)
    lf, d_row = _logf_cumsum(z, lw["fox_bias"], p_len, t)
    log_f = lf[:, :, p_len:p_len + t].transpose(0, 2, 1)

    if past is None:
        tq = _tile(t, 512)
        out_a = _fox_prompt(qf, kf_t, vf_t, d_row, l, b, tq)
        out_b = _diff_prompt(qd, kd_t, vd, lw["diff_lambda"], lw["diff_subln"], lam_init, l, b, tq)
    else:
        assert p_len % CHUNK == 0 and t <= CHUNK and p_len % LANES == 0
        tk = _tile(p_len, 2048)
        dq = d_row[:, :, p_len:p_len + t].reshape(b, GROUPS * t, 1)
        qf_feat = qf.reshape(n, FOX_HEADS, LANES)[:, :, 0:HEAD_DIM].reshape(n, BRANCH_W)
        out_a = _sample_attn(_block_diag_queries(qf_feat, b, t), fox_kt, fox_vt, kf_t, vf_t, (dq, d_row),
                             layer=l, fox=True, b=b, t_new=t, past=p_len, tk=tk)
        out_b = _sample_attn(_block_diag_queries(qd, b, t), diff_kt, diff_v, kd_t, vd,
                             (lw["diff_lambda"], lw["diff_subln"]),
                             layer=l, fox=False, b=b, t_new=t, past=p_len, tk=tk, lam_init=lam_init)

    r, lgw, km, v, kk, kka, gg, bonus, shift = _rwkv_prep(
        x, lw["g_mix_pre"], lw["w_rwkv"], shift0, lw["rwkv_vecs"], lw["rwkv_mats"], lw["rwkv_seg"], b, tm)
    ch = _tile(t, CHUNK)
    out_c, s_new = _rwkv_scan(r, lgw, km, v, kk, kka, gg, bonus, lw["rwkv_gain"], lw["rwkv_bias"], s0, b, ch,
                              _tile(b, 4))

    y = _merge(x, out_a, out_b, out_c, lw["g_mix_pre"], lw["w_gate"], lw["w_branch"], lw["w_out"],
               lw["g_mix_post"], _tile(n, 512))
    state = (log_f, s_new, shift.reshape(b, RWKV_IN))
    return y, state, bufs


def _layer(x, lw, l, b, t, past, mem_k, mem_v, mem_layer, tabs, bufs):
    n = x.shape[0]
    x, state, bufs = _mixer(x, lw, l, b, t, past, tabs, bufs)
    x = _xattn(x, lw["g_x_pre"], lw["w_xq"], mem_k, mem_v, lw["w_xo"], lw["g_x_post"], mem_layer, b,
               _tile(t, 512))
    hidden = lw["w_ffn_out"].shape[0]
    th = hidden // 2 if (hidden // 2) % LANES == 0 else 2 * LANES
    x = _ffn(x, lw["g_ffn_pre"], lw["w_ffn_in"], lw["w_ffn_out"], lw["g_ffn_post"], _tile(n, 512), th)
    return x, state, bufs


def _assemble_states(states, bufs, b, t):
    log_f, s_new, shift = (jnp.stack(e) for e in zip(*states))
    depth = bufs["kf"].shape[0]
    tok_major = lambda a: a.reshape(depth, b, FOX_HEADS, HEAD_DIM, t).transpose(0, 1, 4, 2, 3)
    fox_k, fox_v = tok_major(bufs["kf"]), tok_major(bufs["vf"])
    diff_k = tok_major(bufs["kd"]).reshape(depth, b, t, DIFF_HEADS, 2, HEAD_DIM)
    diff_v = bufs["vd"].reshape(depth, b, t, DIFF_HEADS, 2 * HEAD_DIM)
    return fox_k, fox_v, log_f, diff_k, diff_v, s_new, shift


def kernel(x_prompt, x_sample, mem_prompt, cache_fox_k, cache_fox_v, cache_fox_logf, cache_diff_k, cache_diff_v, state_rwkv, state_rwkv_shift, cache_mem_k, cache_mem_v, norm_mix_pre, norm_mix_post, w_in, fox_forget_bias, diff_lambda, diff_subln, rwkv_mu, rwkv_w0, rwkv_w2, rwkv_a0, rwkv_a2, rwkv_g2, rwkv_kk_scale, rwkv_ka, rwkv_rk, rwkv_ln_gain, rwkv_ln_bias, w_branch, w_out, norm_x_pre, norm_x_post, norm_mem, w_xq, w_xkv, w_xo, norm_ffn_pre, norm_ffn_post, w_ffn_in, w_ffn_out):
    wts = dict(norm_mix_pre=norm_mix_pre, norm_mix_post=norm_mix_post, w_in=w_in, fox_forget_bias=fox_forget_bias,
               diff_lambda=diff_lambda, diff_subln=diff_subln, rwkv_mu=rwkv_mu, rwkv_w0=rwkv_w0, rwkv_w2=rwkv_w2,
               rwkv_a0=rwkv_a0, rwkv_a2=rwkv_a2, rwkv_g2=rwkv_g2, rwkv_kk_scale=rwkv_kk_scale, rwkv_ka=rwkv_ka,
               rwkv_rk=rwkv_rk, rwkv_ln_gain=rwkv_ln_gain, rwkv_ln_bias=rwkv_ln_bias, w_branch=w_branch,
               w_out=w_out, norm_x_pre=norm_x_pre, norm_x_post=norm_x_post, norm_mem=norm_mem, w_xq=w_xq,
               w_xkv=w_xkv, w_xo=w_xo, norm_ffn_pre=norm_ffn_pre, norm_ffn_post=norm_ffn_post,
               w_ffn_in=w_ffn_in, w_ffn_out=w_ffn_out)
    depth = w_in.shape[0]
    bp, tp, d = x_prompt.shape
    bs, ts, _ = x_sample.shape
    p_len = cache_fox_k.shape[2]
    n_mem = mem_prompt.shape[1]
    w = BRANCH_W

    tabs_p = _rotary_tables(jnp.arange(tp, dtype=jnp.int32))
    tabs_s = _rotary_tables(p_len + jnp.arange(ts, dtype=jnp.int32))

    past_s = (cache_fox_k.transpose(0, 1, 3, 4, 2).reshape(depth, bs, w, p_len),
              cache_fox_v.transpose(0, 1, 3, 4, 2).reshape(depth, bs, w, p_len),
              cache_fox_logf.transpose(0, 1, 3, 2),
              cache_diff_k.transpose(0, 1, 3, 4, 5, 2).reshape(depth, bs, w, p_len),
              cache_diff_v.reshape(depth, bs, p_len * DIFF_HEADS, 2 * HEAD_DIM),
              state_rwkv, state_rwkv_shift)
    mem_rows = (n_mem * XATTN_HEADS, XATTN_HEAD_DIM)
    mem_k_s = cache_mem_k.reshape((depth, bs) + mem_rows)
    mem_v_s = cache_mem_v.reshape((depth, bs) + mem_rows)

    xp = x_prompt.reshape(bp * tp, d)
    xs = x_sample.reshape(bs * ts, d)
    mem = mem_prompt.reshape(bp * n_mem, d)
    p_new, s_new, p_mk, p_mv = [], [], [], []
    bufs_p = _new_state_buffers(depth, bp, tp)
    bufs_s = _new_state_buffers(depth, bs, ts)
    for l in range(depth):
        lw = _layer_weights(l, wts)
        mk, mv = _memkv(mem, lw["g_mem"], lw["w_xkv"], _tile(bp * n_mem, 256))
        xp, st_p, bufs_p = _layer(xp, lw, l, bp, tp, None, mk.reshape((1, bp) + mem_rows),
                                  mv.reshape((1, bp) + mem_rows), 0, tabs_p, bufs_p)
        xs, st_s, bufs_s = _layer(xs, lw, l, bs, ts, past_s, mem_k_s, mem_v_s, l, tabs_s, bufs_s)
        p_new.append(st_p)
        s_new.append(st_s)
        p_mk.append(mk.reshape(bp, n_mem, XATTN_HEADS, XATTN_HEAD_DIM))
        p_mv.append(mv.reshape(bp, n_mem, XATTN_HEADS, XATTN_HEAD_DIM))
    p_out = _assemble_states(p_new, bufs_p, bp, tp)
    s_out = _assemble_states(s_new, bufs_s, bs, ts)
    return (xp.reshape(bp, tp, d), xs.reshape(bs, ts, d)) + p_out + (jnp.stack(p_mk), jnp.stack(p_mv)) + s_out
```

```python
import functools
import math

import numpy as np
import jax
import jax.numpy as jnp
from jax import lax
from jax.experimental import pallas as pl
from jax.experimental.pallas import tpu as pltpu

F32 = jnp.float32
BF = jnp.bfloat16

CHUNK = 64
HEAD_DIM = 64
FOX_HEADS = 8
DIFF_HEADS = 4
RWKV_HEADS = 8
GROUPS = 8
BRANCH_W = 512
DECAY_LORA = 64
AAA_LORA = 64
GATE_LORA = 128
RWKV_IN = 3 * BRANCH_W + DECAY_LORA + AAA_LORA + GATE_LORA
ROT_DIM = HEAD_DIM // 4
ROPE_THETA = 500000.0
XATTN_HEADS = 4
XATTN_HEAD_DIM = 128
RMS_EPS = 1e-6
RWKV_GN_EPS = 64e-5
ATTN_SCALE = HEAD_DIM ** -0.5
LOG2E = math.log2(math.e)
QK_SCALE = ATTN_SCALE * LOG2E
D_PIECES = 3
NEG_BIG = -1e30

LANES = 128
SUBLANES = 8
VMEM_LIMIT_BYTES = 56 * 1024 * 1024


def _cparams(*sem):
    return pltpu.CompilerParams(dimension_semantics=sem, vmem_limit_bytes=VMEM_LIMIT_BYTES)


def _const_spec(shape):
    nd = len(shape)
    return pl.BlockSpec(shape, lambda *_: (0,) * nd)


def _rms(x, g):
    return x * lax.rsqrt(jnp.mean(x * x, axis=-1, keepdims=True) + RMS_EPS) * g


_NN = ((1,), (0,))
_NT = ((1,), (1,))
_TN = ((0,), (0,))


def _dg(a, b, dims):
    return lax.dot_general(a, b, (dims, ((), ())), preferred_element_type=F32)


def _dot(a, b):
    return _dg(a.astype(BF), b.astype(BF), _NN)


def _dot_nt(a, b):
    return _dg(a.astype(BF), b.astype(BF), _NT)


def _split2(x):
    hi = x.astype(BF)
    lo = (x - hi.astype(F32)).astype(BF)
    return hi, lo


def _split3(x):
    hi = x.astype(BF)
    r1 = x - hi.astype(F32)
    mid = r1.astype(BF)
    lo = (r1 - mid.astype(F32)).astype(BF)
    return hi, mid, lo


def _dot_sel(sel, x):
    hi, mid, lo = _split3(x)
    return _dg(sel, hi, _NN) + (_dg(sel, mid, _NN) + _dg(sel, lo, _NN))


def _dot_sel_right(x, sel):
    hi, lo = _split2(x)
    return _dg(hi, sel, _NN) + _dg(lo, sel, _NN)


def _softplus(x):
    return jnp.maximum(x, 0.0) + jnp.log(1.0 + jnp.exp(-jnp.abs(x)))


def _fox_inproj_kernel(x_ref, g_ref, wq_ref, qc_ref, wkv_t_ref, wf_t_ref, kbuf_ref, vbuf_ref,
                       q_ref, kt_ref, vt_ref, f_ref):
    del kbuf_ref, vbuf_ref
    xn = _rms(x_ref[...], g_ref[...]).astype(BF)
    w = BRANCH_W
    q_ref[...] = (_dg(xn, wq_ref[...], _NN) * QK_SCALE + qc_ref[...]).astype(BF)
    kt_ref[0, 0] = _dg(wkv_t_ref[0:w, :], xn, _NT)
    vt_ref[0, 0] = _dg(wkv_t_ref[w:2 * w, :], xn, _NT)
    f_ref[0] = _dg(wf_t_ref[...], xn, _NT)


def _row_specs(b, t, tm, layer):
    nt = t // tm
    row = lambda width: pl.BlockSpec((tm, width), lambda bi, ti: (bi * nt + ti, 0))
    col = lambda height: pl.BlockSpec((1, height, tm), lambda bi, ti: (bi, 0, ti))
    state = lambda height: pl.BlockSpec((1, 1, height, tm), lambda bi, ti: (layer, bi, 0, ti))
    return nt, row, col, state


_ANY = pl.BlockSpec(memory_space=pl.ANY)


def _fox_q_layout(wq):
    d = wq.shape[0]
    w3 = wq.reshape(d, FOX_HEADS, HEAD_DIM)
    wq_slab = jnp.concatenate([w3, jnp.zeros_like(w3)], axis=2).reshape(d, FOX_HEADS * LANES)
    slab = np.zeros((FOX_HEADS, LANES), np.float32)
    slab[:, HEAD_DIM:HEAD_DIM + D_PIECES] = -1.0
    return wq_slab, jnp.asarray(slab.reshape(1, FOX_HEADS * LANES))


def _fox_inproj(x, g, wq_slab, q_const, wkv_t, wf_t, kbuf, vbuf, layer, b, tm):
    n, d = x.shape
    t = n // b
    w = BRANCH_W
    wq_w = wq_slab.shape[1]
    nt, row, col, state = _row_specs(b, t, tm, layer)
    return pl.pallas_call(
        _fox_inproj_kernel,
        grid=(b, nt),
        in_specs=[row(d), _const_spec((1, d)), _const_spec((d, wq_w)), _const_spec((1, wq_w)),
                  _const_spec((2 * w, d)), _const_spec((FOX_HEADS, d)), _ANY, _ANY],
        out_specs=[row(wq_w), state(w), state(w), col(FOX_HEADS)],
        out_shape=[jax.ShapeDtypeStruct((n, wq_w), BF), jax.ShapeDtypeStruct(kbuf.shape, F32),
                   jax.ShapeDtypeStruct(vbuf.shape, F32), jax.ShapeDtypeStruct((b, FOX_HEADS, t), F32)],
        input_output_aliases={6: 1, 7: 2},
        compiler_params=_cparams("parallel", "parallel"),
    )(x, g, wq_slab, q_const, wkv_t, wf_t, kbuf, vbuf)


def _diff_inproj_kernel(x_ref, g_ref, wq_ref, wk_t_ref, wv_ref, cos_ref, sa_ref, sb_ref, cos_t_ref, sin_t_ref,
                        kbuf_ref, vbuf_ref, q_ref, kt_ref, v_ref):
    del kbuf_ref, vbuf_ref
    kt_ref = kt_ref.at[0]
    tm = x_ref.shape[0]
    xn = _rms(x_ref[...], g_ref[...]).astype(BF)
    w = BRANCH_W
    half = ROT_DIM // 2
    cos, sa, sb = cos_ref[...], sa_ref[...], sb_ref[...]
    yq = _dg(xn, wq_ref[...], _NN)
    for j in range(w // LANES):
        yj = yq[:, j * LANES:(j + 1) * LANES]
        rj = yj * cos + pltpu.roll(yj, LANES - half, 1) * sa + pltpu.roll(yj, half, 1) * sb
        q_ref[:, j * LANES:(j + 1) * LANES] = (rj * QK_SCALE).astype(BF)
    yk = _dg(wk_t_ref[...], xn, _NT)
    cos_t, sin_t = cos_t_ref[...], sin_t_ref[...]
    for gi in range(GROUPS):
        r0 = gi * HEAD_DIM
        y1, y2 = yk[r0:r0 + half, :], yk[r0 + half:r0 + ROT_DIM, :]
        kt_ref[0, r0:r0 + half, :] = y1 * cos_t - y2 * sin_t
        kt_ref[0, r0 + half:r0 + ROT_DIM, :] = y2 * cos_t + y1 * sin_t
        kt_ref[0, r0 + ROT_DIM:r0 + HEAD_DIM, :] = yk[r0 + ROT_DIM:r0 + HEAD_DIM, :]
    yv = _dg(xn, wv_ref[...], _NN)
    dv = 2 * HEAD_DIM
    for h in range(DIFF_HEADS):
        v_ref[0, pl.ds(h, tm, stride=DIFF_HEADS), :] = yv[:, h * dv:(h + 1) * dv]


def _diff_inproj(x, g, wq, wk_t, wv, tabs, kbuf, vbuf, layer, b, tm):
    n, d = x.shape
    t = n // b
    w = BRANCH_W
    nt, row, col, state = _row_specs(b, t, tm, layer)
    row_tab = pl.BlockSpec((tm, LANES), lambda bi, ti: (ti, 0))
    col_tab = pl.BlockSpec((ROT_DIM // 2, tm), lambda bi, ti: (0, ti))
    vspec = pl.BlockSpec((1, DIFF_HEADS * tm, 2 * HEAD_DIM), lambda bi, ti: (layer, bi * nt + ti, 0))
    return pl.pallas_call(
        _diff_inproj_kernel,
        grid=(b, nt),
        in_specs=[row(d), _const_spec((1, d)), _const_spec((d, w)), _const_spec((w, d)), _const_spec((d, w)),
                  row_tab, row_tab, row_tab, col_tab, col_tab, _ANY, _ANY],
        out_specs=[row(w), state(w), vspec],
        out_shape=[jax.ShapeDtypeStruct((n, w), BF), jax.ShapeDtypeStruct(kbuf.shape, F32),
                   jax.ShapeDtypeStruct(vbuf.shape, F32)],
        input_output_aliases={10: 1, 11: 2},
        compiler_params=_cparams("parallel", "parallel"),
    )(x, g, wq, wk_t, wv, *tabs, kbuf, vbuf)


def _rotary_tables(pos):
    half = ROT_DIM // 2
    inv = ROPE_THETA ** (-np.arange(half, dtype=np.float32) / half)
    ang = pos.astype(F32)[:, None] * jnp.asarray(inv, F32)[None, :]
    cos, sin = jnp.cos(ang), jnp.sin(ang)
    t = pos.shape[0]
    one = jnp.ones((t, HEAD_DIM - ROT_DIM), F32)
    zero = jnp.zeros((t, HEAD_DIM - ROT_DIM), F32)
    zh = jnp.zeros((t, half), F32)
    cos_h = jnp.concatenate([cos, cos, one], axis=1)
    sa_h = jnp.concatenate([-sin, zh, zero], axis=1)
    sb_h = jnp.concatenate([zh, sin, zero], axis=1)
    dup = lambda a: jnp.concatenate([a, a], axis=1)
    return dup(cos_h), dup(sa_h), dup(sb_h), cos.T, sin.T


def _rwkv_prep_kernel(x_ref, g_ref, wc_ref, shift0_ref, mu_ref, w0_ref, w2_ref, a0_ref, a2_ref, g2_ref,
                      kks_ref, ka_ref, rk_ref, seg_ref,
                      r_ref, lw_ref, km_ref, v_ref, kk_ref, kka_ref, gg_ref, bonus_ref, shift_ref,
                      carry_ref):
    t = pl.program_id(1)
    tm = x_ref.shape[0]
    w = BRANCH_W
    xn = _rms(x_ref[...], g_ref[...]).astype(BF)
    c = _dg(xn, wc_ref[...], _NN)

    @pl.when(t == 0)
    def _():
        carry_ref[...] = shift0_ref[0]

    row = lax.broadcasted_iota(jnp.int32, (tm, 1), 0)
    prev = jnp.where(row == 0, carry_ref[...], pltpu.roll(c, 1, 0))
    last = c[tm - 1:tm, :]
    carry_ref[...] = last
    shift_ref[0] = last

    xs = c + mu_ref[...] * (prev - c)
    cr, ck, cv = xs[:, 0:w], xs[:, w:2 * w], xs[:, 2 * w:3 * w]
    o = 3 * w
    xw = xs[:, o:o + DECAY_LORA]
    xa = xs[:, o + DECAY_LORA:o + DECAY_LORA + AAA_LORA]
    xg = xs[:, o + DECAY_LORA + AAA_LORA:]
    z = w0_ref[...] + _dot(jnp.tanh(xw), w2_ref[...])
    w_raw = -_softplus(-z) - 0.5
    a = jax.nn.sigmoid(a0_ref[...] + _dot(xa, a2_ref[...]))
    seg = seg_ref[...]
    kk = ck * kks_ref[...]
    kk = kk / jnp.maximum(jnp.sqrt(_dot_sel_right(kk * kk, seg)), 1e-12)
    kmod = ck * (1.0 + (a - 1.0) * ka_ref[...])
    r_ref[...] = cr
    lw_ref[...] = -jnp.exp(w_raw)
    km_ref[...] = kmod
    v_ref[...] = cv
    kk_ref[...] = kk
    kka_ref[...] = kk * a
    gg_ref[...] = _dot(jax.nn.sigmoid(xg), g2_ref[...])
    bonus_ref[...] = _dot_sel_right(cr * kmod * rk_ref[...], seg) * cv


def _rwkv_prep(x, g, wc, shift0, vecs, mats, seg, b, tm):
    n, d = x.shape
    t = n // b
    w = BRANCH_W
    mu, w0, a0, kks, ka, rk = vecs
    w2, a2, g2 = mats
    nt, row, _, _ = _row_specs(b, t, tm, 0)
    per_b = pl.BlockSpec((1, 1, RWKV_IN), lambda bi, ti: (bi, 0, 0))
    outs = [jax.ShapeDtypeStruct((n, w), F32)] * 8 + [jax.ShapeDtypeStruct((b, 1, RWKV_IN), F32)]
    return pl.pallas_call(
        _rwkv_prep_kernel,
        grid=(b, nt),
        in_specs=[row(d), _const_spec((1, d)), _const_spec((d, RWKV_IN)), per_b,
                  _const_spec((1, RWKV_IN)), _const_spec((1, w)), _const_spec((DECAY_LORA, w)),
                  _const_spec((1, w)), _const_spec((AAA_LORA, w)), _const_spec((GATE_LORA, w)),
                  _const_spec((1, w)), _const_spec((1, w)), _const_spec((1, w)), _const_spec((w, w))],
        out_specs=[row(w)] * 8 + [per_b],
        out_shape=outs,
        scratch_shapes=[pltpu.VMEM((1, RWKV_IN), F32)],
        compiler_params=_cparams("parallel", "arbitrary"),
    )(x, g, wc, shift0, mu, w0, w2, a0, a2, g2, kks, ka, rk, seg)


def _logf_cumsum_kernel(z_ref, bias_ref, lf_ref, d_ref, *, past, new):
    z = z_ref[0]
    width = z.shape[1]
    col = lax.broadcasted_iota(jnp.int32, z.shape, 1)
    zz = z + bias_ref[...]
    log_sig = jnp.minimum(zz, 0.0) - jnp.log(1.0 + jnp.exp(-jnp.abs(zz)))
    lf = jnp.where(col < past, z, jnp.where(col < past + new, log_sig, 0.0))
    lf_ref[0] = lf
    x = lf
    s = 1
    while s < width:
        x = x + jnp.where(col >= s, pltpu.roll(x, s, 1), 0.0)
        s *= 2
    d_ref[0] = x


def _logf_cumsum(z, bias, past, new):
    b, h, width = z.shape
    blk = pl.BlockSpec((1, h, width), lambda i: (i, 0, 0))
    return pl.pallas_call(
        functools.partial(_logf_cumsum_kernel, past=past, new=new),
        grid=(b,),
        in_specs=[blk, _const_spec((h, 1))],
        out_specs=[blk, blk],
        out_shape=[jax.ShapeDtypeStruct(z.shape, F32)] * 2,
        compiler_params=_cparams("parallel"),
    )(z, bias)


def _softmax_update(s, m_ref, l_ref, idx):
    m_prev = m_ref[idx]
    m_new = jnp.maximum(m_prev, jnp.max(s, axis=-1, keepdims=True))
    alpha = jnp.exp2(m_prev - m_new)
    p = jnp.exp2(s - m_new)
    l_ref[idx] = alpha * l_ref[idx] + jnp.sum(p, axis=-1, keepdims=True)
    m_ref[idx] = m_new
    return p.astype(BF), alpha


def _init_softmax_state(m_ref, l_ref, acc_ref):
    m_ref[...] = jnp.full(m_ref.shape, NEG_BIG, F32)
    l_ref[...] = jnp.zeros(l_ref.shape, F32)
    acc_ref[...] = jnp.zeros(acc_ref.shape, F32)


def _row_max_update(s, m_ref, idx):
    reps = s.shape[1] // LANES
    m_prev = m_ref[idx]
    m_new = jnp.maximum(m_prev, jnp.max(s, axis=-1, keepdims=True))
    alpha = jnp.exp2(m_prev - m_new)
    p = jnp.exp2(s - jnp.concatenate([m_new] * reps, axis=1))
    m_ref[idx] = m_new
    return p.astype(BF), alpha


def _init_prompt_state(m_ref, acc_ref):
    m_ref[...] = jnp.full(m_ref.shape, NEG_BIG, F32)
    acc_ref[...] = jnp.zeros(acc_ref.shape, F32)


def _causal_tile_pairs(nq):
    pairs = [(i, j) for i in range(nq) for j in range(i + 1)]
    return (jnp.asarray([p[0] for p in pairs], jnp.int32), jnp.asarray([p[1] for p in pairs], jnp.int32))


def _causal_blocks(i, j, block):
    pl.when(j < i)(functools.partial(block, False))
    pl.when(j == i)(functools.partial(block, True))


def _fox_prompt_kernel(it_ref, jt_ref, q_ref, kt_ref, vt_ref, dk_ref, o_ref, m_ref, acc_ref):
    pair = pl.program_id(1)
    i, j = it_ref[pair], jt_ref[pair]
    tq, tk = q_ref.shape[0], kt_ref.shape[3]
    heads = [slice(h * HEAD_DIM, (h + 1) * HEAD_DIM) for h in range(FOX_HEADS)]

    @pl.when(j == 0)
    def _():
        _init_prompt_state(m_ref, acc_ref)

    def block(masked):
        kb = kt_ref[0, 0].astype(BF)
        vb = vt_ref[0, 0].astype(BF)
        ones = jnp.ones((HEAD_DIM, tk), BF)
        d_hi, d_mid, d_lo = _split3(dk_ref[0] * LOG2E)
        rows = 2 * SUBLANES
        row = lax.broadcasted_iota(jnp.int32, (rows, tk), 0)
        pad = jnp.zeros((LANES - HEAD_DIM - rows, tk), BF)
        k_ops = []
        for h, hs in enumerate(heads):
            pieces = [jnp.broadcast_to(a[h:h + 1, :].astype(F32), (rows, tk)) for a in (d_hi, d_mid, d_lo)]
            extra = jnp.where(row == 0, pieces[0], jnp.where(row == 1, pieces[1], jnp.where(row == 2, pieces[2], 0.0)))
            k_ops.append(jnp.concatenate([kb[hs, :], extra.astype(BF), pad], axis=0))
        scores = [_dg(q_ref[:, h * LANES:(h + 1) * LANES], k_ops[h], _NN) for h in range(FOX_HEADS)]
        if masked:
            visible = (lax.broadcasted_iota(jnp.int32, (tq, tk), 1) <= lax.broadcasted_iota(jnp.int32, (tq, tk), 0))
        for h, hs in enumerate(heads):
            s = jnp.where(visible, scores[h], NEG_BIG) if masked else scores[h]
            p, alpha = _row_max_update(s, m_ref, h)
            v_ones = jnp.concatenate([vb[hs, :], ones], axis=0)
            acc_ref[h] = alpha * acc_ref[h] + _dg(p, v_ones, _NT)

    _causal_blocks(i, j, block)

    @pl.when(j == i)
    def _():
        for h, hs in enumerate(heads):
            acc = acc_ref[h]
            o_ref[:, hs] = acc[:, 0:HEAD_DIM] / acc[:, HEAD_DIM:2 * HEAD_DIM]


def _fox_prompt(q, kt, vt, d_row, layer, b, tq):
    n, wq = q.shape
    w = BRANCH_W
    t = n // b
    nq = t // tq
    i_tab, j_tab = _causal_tile_pairs(nq)
    qspec = lambda width: pl.BlockSpec((tq, width), lambda bi, p, it, jt: (bi * nq + it[p], 0))
    kspec = pl.BlockSpec((1, 1, w, tq), lambda bi, p, it, jt: (layer, bi, 0, jt[p]))
    grid_spec = pltpu.PrefetchScalarGridSpec(
        num_scalar_prefetch=2,
        grid=(b, i_tab.shape[0]),
        in_specs=[qspec(wq), kspec, kspec,
                  pl.BlockSpec((1, FOX_HEADS, tq), lambda bi, p, it, jt: (bi, 0, jt[p]))],
        out_specs=qspec(w),
        scratch_shapes=[pltpu.VMEM((FOX_HEADS, tq, LANES), F32), pltpu.VMEM((FOX_HEADS, tq, 2 * HEAD_DIM), F32)],
    )
    return pl.pallas_call(
        _fox_prompt_kernel,
        grid_spec=grid_spec,
        out_shape=jax.ShapeDtypeStruct((n, w), F32),
        compiler_params=_cparams("parallel", "arbitrary"),
    )(i_tab, j_tab, q, kt, vt, d_row)


def _diff_lambda(lam_ref, lam_init):
    p = lam_ref[...]
    s1 = jnp.sum(p[0:1, :] * p[1:2, :], axis=-1, keepdims=True)
    s2 = jnp.sum(p[2:3, :] * p[3:4, :], axis=-1, keepdims=True)
    return jnp.exp(s1) - jnp.exp(s2) + lam_init


def _diff_combine(acc0, l0, acc1, l1, lam, subln, lam_init):
    o = acc0 / l0 - lam * (acc1 / l1)
    return _rms(o, subln) * (1.0 - lam_init)


def _diff_prompt_kernel(it_ref, jt_ref, q_ref, kt_ref, v_ref, lam_ref, subln_ref, o_ref, m_ref, acc_ref, *,
                        lam_init):
    pair = pl.program_id(1)
    i, j = it_ref[pair], jt_ref[pair]
    tq, tk = q_ref.shape[0], kt_ref.shape[3]
    dv = 2 * HEAD_DIM

    @pl.when(j == 0)
    def _():
        _init_prompt_state(m_ref, acc_ref)

    def block(masked):
        kb = kt_ref[0, 0].astype(BF)
        ones = jnp.ones((tk, dv), BF)
        groups = [slice(gi * HEAD_DIM, (gi + 1) * HEAD_DIM) for gi in range(GROUPS)]
        scores = [_dg(q_ref[:, gs], kb[gs, :], _NN) for gs in groups]
        if masked:
            visible = ((lax.broadcasted_iota(jnp.int32, (tq, tk), 1) // CHUNK)
                       <= (lax.broadcasted_iota(jnp.int32, (tq, tk), 0) // CHUNK))
        for h in range(DIFF_HEADS):
            vh = v_ref[0, pl.ds(h, tk, stride=DIFF_HEADS), :].astype(BF)
            v_ones = jnp.concatenate([vh, ones], axis=1)
            for gi in (2 * h, 2 * h + 1):
                s = jnp.where(visible, scores[gi], NEG_BIG) if masked else scores[gi]
                p, alpha = _row_max_update(s, m_ref, gi)
                acc_ref[gi] = jnp.concatenate([alpha, alpha], axis=1) * acc_ref[gi] + _dg(p, v_ones, _NN)

    _causal_blocks(i, j, block)

    @pl.when(j == i)
    def _():
        lam = _diff_lambda(lam_ref, lam_init)
        for h in range(DIFF_HEADS):
            a0, a1 = acc_ref[2 * h], acc_ref[2 * h + 1]
            o_ref[:, h * dv:(h + 1) * dv] = _diff_combine(a0[:, 0:dv], a0[:, dv:], a1[:, 0:dv], a1[:, dv:], lam,
                                                          subln_ref[...], lam_init)


def _diff_prompt(q, kt, v, lam_p, subln, lam_init, layer, b, tq):
    n, w = q.shape
    t = n // b
    nq = t // tq
    i_tab, j_tab = _causal_tile_pairs(nq)
    qspec = pl.BlockSpec((tq, w), lambda bi, p, it, jt: (bi * nq + it[p], 0))
    ktspec = pl.BlockSpec((1, 1, w, tq), lambda bi, p, it, jt: (layer, bi, 0, jt[p]))
    vspec = pl.BlockSpec((1, DIFF_HEADS * tq, 2 * HEAD_DIM), lambda bi, p, it, jt: (layer, bi * nq + jt[p], 0))
    grid_spec = pltpu.PrefetchScalarGridSpec(
        num_scalar_prefetch=2,
        grid=(b, i_tab.shape[0]),
        in_specs=[qspec, ktspec, vspec, _const_spec((4, HEAD_DIM)), _const_spec((1, 2 * HEAD_DIM))],
        out_specs=qspec,
        scratch_shapes=[pltpu.VMEM((GROUPS, tq, LANES), F32), pltpu.VMEM((GROUPS, tq, 4 * HEAD_DIM), F32)],
    )
    return pl.pallas_call(
        functools.partial(_diff_prompt_kernel, lam_init=lam_init),
        grid_spec=grid_spec,
        out_shape=jax.ShapeDtypeStruct((n, w), F32),
        compiler_params=_cparams("parallel", "arbitrary"),
    )(i_tab, j_tab, q, kt, v, lam_p, subln)


def _sample_attn_kernel(*refs, fox, t_new, lam_init):
    if fox:
        q_ref, kc_ref, vc_ref, kn_ref, vn_ref, dq_ref, dkc_ref, dkn_ref, o_ref, m_ref, l_ref, acc_ref = refs
    else:
        q_ref, kc_ref, vc_ref, kn_ref, vn_ref, lam_ref, subln_ref, o_ref, m_ref, l_ref, acc_ref = refs
    j = pl.program_id(1)
    dv = 2 * HEAD_DIM
    pair = 2 * t_new

    @pl.when(j == 0)
    def _():
        _init_softmax_state(m_ref, l_ref, acc_ref)

    def expand(dk):
        return jnp.concatenate([jnp.broadcast_to(dk[gi:gi + 1, :], (t_new, dk.shape[1])) for gi in range(GROUPS)],
                               axis=0)

    def scores(kt, dk, causal):
        s = _dg(q_ref[0], kt.astype(BF), _NN)
        if fox:
            s = s + dq_ref[0] * LOG2E - expand(dk * LOG2E)
        if causal:
            qi = lax.broadcasted_iota(jnp.int32, s.shape, 0) % t_new
            kj = lax.broadcasted_iota(jnp.int32, s.shape, 1)
            s = jnp.where(kj <= qi, s, NEG_BIG)
        return _softmax_update(s, m_ref, l_ref, 0)

    def accumulate_diff(p, alpha, value_of_head):
        for h in range(DIFF_HEADS):
            rs = slice(h * pair, (h + 1) * pair)
            acc_ref[0, rs, :] = alpha[rs] * acc_ref[0, rs, :] + _dg(p[rs], value_of_head(h).astype(BF), _NN)

    tk = kc_ref.shape[3]
    if fox:
        p, alpha = scores(kc_ref[0, 0], dkc_ref[0], False)
        acc_ref[0] = alpha * acc_ref[0] + _dg(p, vc_ref[0, 0].astype(BF), _NT)
    else:
        p, alpha = scores(kc_ref[0, 0], None, False)
        accumulate_diff(p, alpha, lambda h: vc_ref[0, 0, pl.ds(h, tk, stride=DIFF_HEADS), :])

    @pl.when(j == pl.num_programs(1) - 1)
    def _():
        if fox:
            p, alpha = scores(kn_ref[0, 0], dkn_ref[0][:, 0:t_new], True)
            acc = alpha * acc_ref[0] + _dg(p, vn_ref[0, 0].astype(BF), _NT)
            l = l_ref[0]
            for gi in range(GROUPS):
                rs = slice(gi * t_new, (gi + 1) * t_new)
                cs = slice(gi * HEAD_DIM, (gi + 1) * HEAD_DIM)
                o_ref[:, cs] = acc[rs, cs] / l[rs]
        else:
            p, alpha = scores(kn_ref[0, 0], None, False)
            accumulate_diff(p, alpha, lambda h: vn_ref[0, pl.ds(h, t_new, stride=DIFF_HEADS), :])
            acc = acc_ref[0]
            l = l_ref[0]
            lam = _diff_lambda(lam_ref, lam_init)
            for h in range(DIFF_HEADS):
                r0 = slice(h * pair, h * pair + t_new)
                r1 = slice(h * pair + t_new, (h + 1) * pair)
                o_ref[:, h * dv:(h + 1) * dv] = _diff_combine(acc[r0], l[r0], acc[r1], l[r1], lam, subln_ref[...],
                                                              lam_init)


def _sample_attn(qbd, kc, vc, kn, vn, extra, *, layer, fox, b, t_new, past, tk, lam_init=0.0):
    w = BRANCH_W
    rows = GROUPS * t_new
    nk = past // tk
    qspec = pl.BlockSpec((1, rows, w), lambda bi, j: (bi, 0, 0))
    kcspec = pl.BlockSpec((1, 1, w, tk), lambda bi, j: (layer, bi, 0, j))
    knspec = pl.BlockSpec((1, 1, w, t_new), lambda bi, j: (layer, bi, 0, 0))
    ospec = pl.BlockSpec((t_new, w), lambda bi, j: (bi, 0))
    if fox:
        dq, d_row = extra
        vcspec, vnspec, acc_w = kcspec, knspec, w
        especs = [pl.BlockSpec((1, rows, 1), lambda bi, j: (bi, 0, 0)),
                  pl.BlockSpec((1, GROUPS, tk), lambda bi, j: (bi, 0, j)),
                  pl.BlockSpec((1, GROUPS, LANES), lambda bi, j: (bi, 0, past // LANES))]
        eargs = [dq, d_row, d_row]
    else:
        vcspec = pl.BlockSpec((1, 1, DIFF_HEADS * tk, 2 * HEAD_DIM), lambda bi, j: (layer, bi, j, 0))
        vnspec = pl.BlockSpec((1, DIFF_HEADS * t_new, 2 * HEAD_DIM), lambda bi, j: (layer, bi, 0))
        acc_w = 2 * HEAD_DIM
        especs = [_const_spec((4, HEAD_DIM)), _const_spec((1, 2 * HEAD_DIM))]
        eargs = list(extra)
    return pl.pallas_call(
        functools.partial(_sample_attn_kernel, fox=fox, t_new=t_new, lam_init=lam_init),
        grid=(b, nk),
        in_specs=[qspec, kcspec, vcspec, knspec, vnspec] + especs,
        out_specs=ospec,
        out_shape=jax.ShapeDtypeStruct((b * t_new, w), F32),
        scratch_shapes=[pltpu.VMEM((1, rows, 1), F32), pltpu.VMEM((1, rows, 1), F32),
                        pltpu.VMEM((1, rows, acc_w), F32)],
        compiler_params=_cparams("parallel", "arbitrary"),
    )(qbd, kc, vc, kn, vn, *eargs)


def _block_diag_queries(q, b, t_new):
    rows, w = GROUPS * t_new, GROUPS * HEAD_DIM
    tiled = jnp.tile(q.reshape(b, t_new, w), (1, GROUPS, 1))
    row_group = lax.broadcasted_iota(jnp.int32, (rows, w), 0) // t_new
    col_group = lax.broadcasted_iota(jnp.int32, (rows, w), 1) // HEAD_DIM
    return jnp.where((row_group == col_group)[None], tiled, jnp.zeros_like(tiled))


def _rwkv_scan_kernel(r_ref, lw_ref, k_ref, v_ref, kk_ref, kka_ref, gg_ref, bonus_ref, gain_ref, bias_ref,
                      s0_ref, o_ref, sfin_ref, s_ref):
    c = pl.program_id(1)
    nb, ch = r_ref.shape[0], r_ref.shape[1]
    n = HEAD_DIM

    @pl.when(c == 0)
    def _():
        s_ref[...] = s0_ref[...]

    ri = lax.broadcasted_iota(jnp.int32, (ch, ch), 0)
    ci = lax.broadcasted_iota(jnp.int32, (ch, ch), 1)
    tri = (ci <= ri).astype(BF)
    ri2 = lax.broadcasted_iota(jnp.int32, (ch, 2 * ch), 0)
    ci2 = lax.broadcasted_iota(jnp.int32, (ch, 2 * ch), 1)
    ci2m = jnp.where(ci2 >= ch, ci2 - ch, ci2)
    strict_right = (ci2 >= ch) & (ci2m < ri2)
    strict_left = ci < ri
    is_x = lax.broadcasted_iota(jnp.int32, (ch, ch + n), 1) >= ch
    incl_both = ci2m <= ri2
    steps = max(1, int(math.ceil(math.log2(ch))))
    heads = [slice(h * n, (h + 1) * n) for h in range(RWKV_HEADS)]

    prep = []
    for bi in range(nb):
        lw = lw_ref[bi]
        cum = _dot_sel(tri, lw)
        cum_end = cum[ch - 1:ch, :]
        kk, kka, kmod, v_all = kk_ref[bi], kka_ref[bi], k_ref[bi], v_ref[bi]
        e_neg = jnp.exp(-cum)
        e_tail = jnp.exp(cum_end - cum)
        prep.append(dict(
            ar=_split2(jnp.concatenate([-kk * jnp.exp(cum - lw), r_ref[bi] * jnp.exp(cum)], axis=0)),
            bk=_split2(jnp.concatenate([kka * e_neg, kmod * e_neg], axis=0)),
            bk_tail=_split2(jnp.concatenate([kka * e_tail, kmod * e_tail], axis=0)),
            vv=_split2(jnp.concatenate([v_all, v_all], axis=0)),
            v=v_all, decay_end=jnp.exp(cum_end)))
    units = [(bi, h, heads[h]) for bi in range(nb) for h in range(RWKV_HEADS)]
    s_old = [s_ref[bi, h] for bi, h, _ in units]

    def dot3s(a2, b2, dims):
        (ah, al), (bh, bl) = a2, b2
        m = ah.shape[0]
        r = _dg(jnp.concatenate([ah, al], axis=0), bh, dims)
        return r[0:m] + r[m:] + _dg(ah, bl, dims)

    def dot3_tn(a2, b2):
        (ah, al), (bh, bl) = a2, b2
        return _dg(ah, bh, _TN) + (_dg(ah, bl, _TN) + _dg(al, bh, _TN))

    def nil_times(zh):
        a2, b2 = _split2(zh[:, 0:ch]), _split2(zh)
        wz = zh.shape[1]
        if wz % LANES:
            return dot3s(a2, b2, _NN)
        r = _dg(jnp.concatenate(a2, axis=0), jnp.concatenate(b2, axis=1), _NN)
        return (r[0:ch, 0:wz] + r[ch:, 0:wz]) + (r[0:ch, wz:] + r[ch:, wz:])

    cols = lambda a2, hs: (a2[0][:, hs], a2[1][:, hs])
    ga = []
    for u, (bi, h, hs) in enumerate(units):
        sh, sl = _split2(s_old[u])
        bk = prep[bi]["bk"]
        rhs = (jnp.concatenate([bk[0][:, hs], sh], axis=0), jnp.concatenate([bk[1][:, hs], sl], axis=0))
        ga.append(dot3s(cols(prep[bi]["ar"], hs), rhs, _NT))
    z = []
    for u, (bi, h, hs) in enumerate(units):
        g_top = ga[u][0:ch, 0:2 * ch]
        x0 = ga[u][0:ch, 2 * ch:] + dot3s(_split2(jnp.where(strict_right, g_top, 0.0)), cols(prep[bi]["vv"], hs), _NN)
        z.append(jnp.concatenate([jnp.where(strict_left, g_top[:, 0:ch], 0.0), x0], axis=1))
    for _ in range(steps):
        z = [nil_times(zh) + jnp.where(is_x, zh, 0.0) for zh in z]
    for u, (bi, h, hs) in enumerate(units):
        x = z[u][:, ch:]
        xv = _split2(jnp.concatenate([x, prep[bi]["v"][:, hs]], axis=0))
        g_bot = jnp.where(incl_both, ga[u][ch:, 0:2 * ch], 0.0)
        y = ga[u][ch:, 2 * ch:] + _dg(g_bot.astype(BF), xv[0], _NN)
        s_ref[bi, h] = s_old[u] * prep[bi]["decay_end"][:, hs] + dot3_tn(xv, cols(prep[bi]["bk_tail"], hs))

        mu = jnp.mean(y, axis=-1, keepdims=True)
        yc = y - mu
        var = jnp.mean(yc * yc, axis=-1, keepdims=True)
        yn = yc * lax.rsqrt(var + RWKV_GN_EPS) * gain_ref[:, hs] + bias_ref[:, hs]
        o_ref[bi, :, hs] = (yn + bonus_ref[bi, :, hs]) * gg_ref[bi, :, hs]

    @pl.when(c == pl.num_programs(1) - 1)
    def _():
        sfin_ref[...] = s_ref[...]


def _rwkv_scan(r, lw, km, v, kk, kka, gg, bonus, gain, bias, s0, b, ch, nb):
    n, w = r.shape
    t = n // b
    rows = [a.reshape(b, t, w) for a in (r, lw, km, v, kk, kka, gg, bonus)]
    row = pl.BlockSpec((nb, ch, w), lambda bi, ci: (bi, ci, 0))
    st = pl.BlockSpec((nb, RWKV_HEADS, HEAD_DIM, HEAD_DIM), lambda bi, ci: (bi, 0, 0, 0))
    out, s_fin = pl.pallas_call(
        _rwkv_scan_kernel,
        grid=(b // nb, t // ch),
        in_specs=[row] * 8 + [_const_spec((1, w)), _const_spec((1, w)), st],
        out_specs=[row, st],
        out_shape=[jax.ShapeDtypeStruct((b, t, w), F32), jax.ShapeDtypeStruct(s0.shape, F32)],
        scratch_shapes=[pltpu.VMEM((nb, RWKV_HEADS, HEAD_DIM, HEAD_DIM), F32)],
        compiler_params=_cparams("parallel", "arbitrary"),
    )(*rows, gain, bias, s0)
    return out.reshape(n, w), s_fin


def _merge_kernel(x_ref, oa_ref, ob_ref, oc_ref, gpre_ref, wg_ref, wb_ref, wo_ref, gpost_ref, y_ref):
    x = x_ref[...]
    d = x.shape[1]
    xn = _rms(x, gpre_ref[...]).astype(BF)
    acc = jnp.zeros(x.shape, F32)
    for i, o_ref in enumerate((oa_ref, ob_ref, oc_ref)):
        gate = jax.nn.sigmoid(_dg(xn, wg_ref[:, i * d:(i + 1) * d], _NN))
        acc = acc + gate * _dot(o_ref[...], wb_ref[i])
    y = _dot(acc, wo_ref[...])
    y_ref[...] = x + _rms(y, gpost_ref[...])


def _merge(x, oa, ob, oc, gpre, wg, wb, wo, gpost, tm):
    n, d = x.shape
    w = BRANCH_W
    row = lambda width: pl.BlockSpec((tm, width), lambda i: (i, 0))
    return pl.pallas_call(
        _merge_kernel,
        grid=(n // tm,),
        in_specs=[row(d), row(w), row(w), row(w), _const_spec((1, d)), _const_spec((d, 3 * d)),
                  _const_spec((3, w, d)), _const_spec((d, d)), _const_spec((1, d))],
        out_specs=row(d),
        out_shape=jax.ShapeDtypeStruct((n, d), F32),
        compiler_params=_cparams("parallel"),
    )(x, oa, ob, oc, gpre, wg, wb, wo, gpost)


def _memkv_kernel(m_ref, g_ref, w_ref, k_ref, v_ref):
    xn = _rms(m_ref[...], g_ref[...]).astype(BF)
    tm = m_ref.shape[0]
    hd = XATTN_HEAD_DIM
    w = XATTN_HEADS * hd
    yk = _dg(xn, w_ref[:, 0:w], _NN)
    yv = _dg(xn, w_ref[:, w:2 * w], _NN)
    for h in range(XATTN_HEADS):
        k_ref[pl.ds(h, tm, stride=XATTN_HEADS), :] = yk[:, h * hd:(h + 1) * hd]
        v_ref[pl.ds(h, tm, stride=XATTN_HEADS), :] = yv[:, h * hd:(h + 1) * hd]


def _memkv(mem, g, w_xkv, tm):
    n, d = mem.shape
    w = w_xkv.shape[1] // 2
    out = pl.BlockSpec((XATTN_HEADS * tm, XATTN_HEAD_DIM), lambda i: (i, 0))
    return pl.pallas_call(
        _memkv_kernel,
        grid=(n // tm,),
        in_specs=[pl.BlockSpec((tm, d), lambda i: (i, 0)), _const_spec((1, d)), _const_spec((d, 2 * w))],
        out_specs=[out, out],
        out_shape=[jax.ShapeDtypeStruct((XATTN_HEADS * n, XATTN_HEAD_DIM), F32)] * 2,
        compiler_params=_cparams("parallel"),
    )(mem, g, w_xkv)


def _xattn_kernel(x_ref, gpre_ref, wq_ref, mk_ref, mv_ref, wo_ref, gpost_ref, y_ref):
    x = x_ref[...]
    xn = _rms(x, gpre_ref[...])
    q = _dot(xn, wq_ref[...])
    hd = XATTN_HEAD_DIM
    n_mem = mk_ref.shape[2] // XATTN_HEADS
    outs = []
    for h in range(XATTN_HEADS):
        hs = slice(h * hd, (h + 1) * hd)
        mem_rows = pl.ds(h, n_mem, stride=XATTN_HEADS)
        s = _dot_nt(q[:, hs], mk_ref[0, 0, mem_rows, :]) * (hd ** -0.5)
        p = jnp.exp(s - jnp.max(s, axis=-1, keepdims=True))
        p = p / jnp.sum(p, axis=-1, keepdims=True)
        outs.append(_dot(p, mv_ref[0, 0, mem_rows, :]))
    o = jnp.concatenate(outs, axis=-1)
    y_ref[...] = x + _rms(_dot(o, wo_ref[...]), gpost_ref[...])


def _xattn(x, gpre, wq, mk, mv, wo, gpost, layer, b, tm):
    n, d = x.shape
    t = n // b
    nt = t // tm
    w = wq.shape[1]
    row = pl.BlockSpec((tm, d), lambda bi, ti: (bi * nt + ti, 0))
    mem = pl.BlockSpec((1, 1) + mk.shape[2:], lambda bi, ti: (layer, bi, 0, 0))
    return pl.pallas_call(
        _xattn_kernel,
        grid=(b, nt),
        in_specs=[row, _const_spec((1, d)), _const_spec((d, w)), mem, mem, _const_spec((w, d)),
                  _const_spec((1, d))],
        out_specs=row,
        out_shape=jax.ShapeDtypeStruct((n, d), F32),
        compiler_params=_cparams("parallel", "parallel"),
    )(x, gpre, wq, mk, mv, wo, gpost)


def _ffn_kernel(x_ref, gpre_ref, wu_ref, wv_ref, wo_ref, gpost_ref, y_ref, xn_ref, acc_ref):
    j = pl.program_id(1)

    @pl.when(j == 0)
    def _():
        xn_ref[...] = _rms(x_ref[...], gpre_ref[...]).astype(BF)
        acc_ref[...] = jnp.zeros(acc_ref.shape, F32)

    xn = xn_ref[...]
    u = _dg(xn, wu_ref[...], _NN)
    v = _dg(xn, wv_ref[...], _NN)
    acc_ref[...] += _dot(u * jax.nn.sigmoid(u) * v, wo_ref[...])

    @pl.when(j == pl.num_programs(1) - 1)
    def _():
        y_ref[...] = x_ref[...] + _rms(acc_ref[...], gpost_ref[...])


def _ffn(x, gpre, w_in, w_out, gpost, tm, th):
    n, d = x.shape
    hidden = w_out.shape[0]
    nh = hidden // th
    row = pl.BlockSpec((tm, d), lambda i, j: (i, 0))
    return pl.pallas_call(
        _ffn_kernel,
        grid=(n // tm, nh),
        in_specs=[row, _const_spec((1, d)),
                  pl.BlockSpec((d, th), lambda i, j: (0, j)),
                  pl.BlockSpec((d, th), lambda i, j: (0, nh + j)),
                  pl.BlockSpec((th, d), lambda i, j: (j, 0)),
                  _const_spec((1, d))],
        out_specs=row,
        out_shape=jax.ShapeDtypeStruct((n, d), F32),
        scratch_shapes=[pltpu.VMEM((tm, d), BF), pltpu.VMEM((tm, d), F32)],
        compiler_params=_cparams("parallel", "arbitrary"),
    )(x, gpre, w_in, w_in, w_out, gpost)


def _tile(n, pref):
    t = min(n, pref)
    assert n % t == 0, (n, t)
    return t


def _layer_weights(l, wts):
    p = {k: v[l] for k, v in wts.items()}
    w = BRANCH_W
    w_in = p["w_in"].astype(BF)
    w_in_t = p["w_in"].T.astype(BF)
    gate_w = w_in.shape[1] - (6 * w + FOX_HEADS + RWKV_IN)
    o_fox, o_f, o_diff = 0, 3 * w, 3 * w + FOX_HEADS
    o_rwkv = o_diff + 3 * w
    o_gate = o_rwkv + RWKV_IN
    row = lambda a: a.reshape(1, -1)
    seg = np.kron(np.eye(RWKV_HEADS, dtype=np.float32), np.ones((HEAD_DIM, HEAD_DIM), np.float32))
    return dict(
        g_mix_pre=row(p["norm_mix_pre"]), g_mix_post=row(p["norm_mix_post"]),
        w_fox_q=_fox_q_layout(w_in[:, o_fox:o_fox + w]), w_fox_kv_t=w_in_t[o_fox + w:o_fox + 3 * w],
        w_fox_f_t=w_in_t[o_f:o_f + FOX_HEADS],
        w_diff_q=w_in[:, o_diff:o_diff + w], w_diff_k_t=w_in_t[o_diff + w:o_diff + 2 * w],
        w_diff_v=w_in[:, o_diff + 2 * w:o_diff + 3 * w],
        w_rwkv=w_in[:, o_rwkv:o_rwkv + RWKV_IN], w_gate=w_in[:, o_gate:o_gate + gate_w],
        fox_bias=p["fox_forget_bias"].reshape(FOX_HEADS, 1),
        diff_lambda=p["diff_lambda"], diff_subln=row(p["diff_subln"]),
        rwkv_vecs=tuple(row(p[k]) for k in ("rwkv_mu", "rwkv_w0", "rwkv_a0", "rwkv_kk_scale", "rwkv_ka",
                                            "rwkv_rk")),
        rwkv_mats=tuple(p[k].astype(BF) for k in ("rwkv_w2", "rwkv_a2", "rwkv_g2")),
        rwkv_seg=jnp.asarray(seg, BF),
        rwkv_gain=row(p["rwkv_ln_gain"]), rwkv_bias=row(p["rwkv_ln_bias"]),
        w_branch=p["w_branch"].astype(BF), w_out=p["w_out"].astype(BF),
        g_x_pre=row(p["norm_x_pre"]), g_x_post=row(p["norm_x_post"]), g_mem=row(p["norm_mem"]),
        w_xq=p["w_xq"].astype(BF), w_xkv=p["w_xkv"].astype(BF), w_xo=p["w_xo"].astype(BF),
        g_ffn_pre=row(p["norm_ffn_pre"]), g_ffn_post=row(p["norm_ffn_post"]),
        w_ffn_in=p["w_ffn_in"].astype(BF), w_ffn_out=p["w_ffn_out"].astype(BF),
    )


def _new_state_buffers(depth, b, t):
    kt = lambda: jnp.zeros((depth, b, BRANCH_W, t), F32)
    return dict(kf=kt(), vf=kt(), kd=kt(), vd=jnp.zeros((depth, b * t * DIFF_HEADS, 2 * HEAD_DIM), F32))


def _mixer(x, lw, l, b, t, past, tabs, bufs):
    n = x.shape[0]
    lam_init = 0.8 - 0.6 * math.exp(-0.3 * l)
    tm = _tile(t, 512)
    qf, kf_t, vf_t, f_bt = _fox_inproj(x, lw["g_mix_pre"], *lw["w_fox_q"], lw["w_fox_kv_t"], lw["w_fox_f_t"],
                                       bufs["kf"], bufs["vf"], l, b, tm)
    qd, kd_t, vd = _diff_inproj(x, lw["g_mix_pre"], lw["w_diff_q"], lw["w_diff_k_t"], lw["w_diff_v"], tabs,
                                bufs["kd"], bufs["vd"], l, b, tm)
    bufs = dict(kf=kf_t, vf=vf_t, kd=kd_t, vd=vd)

    if past is None:
        p_len = 0
        z = f_bt
        shift0 = jnp.zeros((b, 1, RWKV_IN), F32)
        s0 = jnp.zeros((b, RWKV_HEADS, HEAD_DIM, HEAD_DIM), F32)
    else:
        fox_kt, fox_vt, fox_lf_t, diff_kt, diff_v, s0, shift0 = past
        p_len = fox_kt.shape[3]
        pad = (-(p_len + t)) % LANES
        z = jnp.concatenate([fox_lf_t[l], f_bt, jnp.zeros((b, FOX_HEADS, pad), F32)], axis=2)
        s0 = s0[l]
        shift0 = shift0[l].reshape(b, 1, RWKV_IN)
    lf, d_row = _logf_cumsum(z, lw["fox_bias"], p_len, t)
    log_f = lf[:, :, p_len:p_len + t].transpose(0, 2, 1)

    if past is None:
        tq = _tile(t, 512)
        out_a = _fox_prompt(qf, kf_t, vf_t, d_row, l, b, tq)
        out_b = _diff_prompt(qd, kd_t, vd, lw["diff_lambda"], lw["diff_subln"], lam_init, l, b, tq)
    else:
        assert p_len % CHUNK == 0 and t <= CHUNK and p_len % LANES == 0
        tk = _tile(p_len, 2048)
        dq = d_row[:, :, p_len:p_len + t].reshape(b, GROUPS * t, 1)
        qf_feat = qf.reshape(n, FOX_HEADS, LANES)[:, :, 0:HEAD_DIM].reshape(n, BRANCH_W)
        out_a = _sample_attn(_block_diag_queries(qf_feat, b, t), fox_kt, fox_vt, kf_t, vf_t, (dq, d_row),
                             layer=l, fox=True, b=b, t_new=t, past=p_len, tk=tk)
        out_b = _sample_attn(_block_diag_queries(qd, b, t), diff_kt, diff_v, kd_t, vd,
                             (lw["diff_lambda"], lw["diff_subln"]),
                             layer=l, fox=False, b=b, t_new=t, past=p_len, tk=tk, lam_init=lam_init)

    r, lgw, km, v, kk, kka, gg, bonus, shift = _rwkv_prep(
        x, lw["g_mix_pre"], lw["w_rwkv"], shift0, lw["rwkv_vecs"], lw["rwkv_mats"], lw["rwkv_seg"], b, tm)
    ch = _tile(t, CHUNK)
    out_c, s_new = _rwkv_scan(r, lgw, km, v, kk, kka, gg, bonus, lw["rwkv_gain"], lw["rwkv_bias"], s0, b, ch,
                              _tile(b, 4))

    y = _merge(x, out_a, out_b, out_c, lw["g_mix_pre"], lw["w_gate"], lw["w_branch"], lw["w_out"],
               lw["g_mix_post"], _tile(n, 512))
    state = (log_f, s_new, shift.reshape(b, RWKV_IN))
    return y, state, bufs


def _layer(x, lw, l, b, t, past, mem_k, mem_v, mem_layer, tabs, bufs):
    n = x.shape[0]
    x, state, bufs = _mixer(x, lw, l, b, t, past, tabs, bufs)
    x = _xattn(x, lw["g_x_pre"], lw["w_xq"], mem_k, mem_v, lw["w_xo"], lw["g_x_post"], mem_layer, b,
               _tile(t, 512))
    hidden = lw["w_ffn_out"].shape[0]
    th = hidden // 2 if (hidden // 2) % LANES == 0 else 2 * LANES
    x = _ffn(x, lw["g_ffn_pre"], lw["w_ffn_in"], lw["w_ffn_out"], lw["g_ffn_post"], _tile(n, 512), th)
    return x, state, bufs


def _assemble_states(states, bufs, b, t):
    log_f, s_new, shift = (jnp.stack(e) for e in zip(*states))
    depth = bufs["kf"].shape[0]
    tok_major = lambda a: a.reshape(depth, b, FOX_HEADS, HEAD_DIM, t).transpose(0, 1, 4, 2, 3)
    fox_k, fox_v = tok_major(bufs["kf"]), tok_major(bufs["vf"])
    diff_k = tok_major(bufs["kd"]).reshape(depth, b, t, DIFF_HEADS, 2, HEAD_DIM)
    diff_v = bufs["vd"].reshape(depth, b, t, DIFF_HEADS, 2 * HEAD_DIM)
    return fox_k, fox_v, log_f, diff_k, diff_v, s_new, shift


def kernel(x_prompt, x_sample, mem_prompt, cache_fox_k, cache_fox_v, cache_fox_logf, cache_diff_k, cache_diff_v, state_rwkv, state_rwkv_shift, cache_mem_k, cache_mem_v, norm_mix_pre, norm_mix_post, w_in, fox_forget_bias, diff_lambda, diff_subln, rwkv_mu, rwkv_w0, rwkv_w2, rwkv_a0, rwkv_a2, rwkv_g2, rwkv_kk_scale, rwkv_ka, rwkv_rk, rwkv_ln_gain, rwkv_ln_bias, w_branch, w_out, norm_x_pre, norm_x_post, norm_mem, w_xq, w_xkv, w_xo, norm_ffn_pre, norm_ffn_post, w_ffn_in, w_ffn_out):
    wts = dict(norm_mix_pre=norm_mix_pre, norm_mix_post=norm_mix_post, w_in=w_in, fox_forget_bias=fox_forget_bias,
               diff_lambda=diff_lambda, diff_subln=diff_subln, rwkv_mu=rwkv_mu, rwkv_w0=rwkv_w0, rwkv_w2=rwkv_w2,
               rwkv_a0=rwkv_a0, rwkv_a2=rwkv_a2, rwkv_g2=rwkv_g2, rwkv_kk_scale=rwkv_kk_scale, rwkv_ka=rwkv_ka,
               rwkv_rk=rwkv_rk, rwkv_ln_gain=rwkv_ln_gain, rwkv_ln_bias=rwkv_ln_bias, w_branch=w_branch,
               w_out=w_out, norm_x_pre=norm_x_pre, norm_x_post=norm_x_post, norm_mem=norm_mem, w_xq=w_xq,
               w_xkv=w_xkv, w_xo=w_xo, norm_ffn_pre=norm_ffn_pre, norm_ffn_post=norm_ffn_post,
               w_ffn_in=w_ffn_in, w_ffn_out=w_ffn_out)
    depth = w_in.shape[0]
    bp, tp, d = x_prompt.shape
    bs, ts, _ = x_sample.shape
    p_len = cache_fox_k.shape[2]
    n_mem = mem_prompt.shape[1]
    w = BRANCH_W

    tabs_p = _rotary_tables(jnp.arange(tp, dtype=jnp.int32))
    tabs_s = _rotary_tables(p_len + jnp.arange(ts, dtype=jnp.int32))

    past_s = (cache_fox_k.transpose(0, 1, 3, 4, 2).reshape(depth, bs, w, p_len),
              cache_fox_v.transpose(0, 1, 3, 4, 2).reshape(depth, bs, w, p_len),
              cache_fox_logf.transpose(0, 1, 3, 2),
              cache_diff_k.transpose(0, 1, 3, 4, 5, 2).reshape(depth, bs, w, p_len),
              cache_diff_v.reshape(depth, bs, p_len * DIFF_HEADS, 2 * HEAD_DIM),
              state_rwkv, state_rwkv_shift)
    mem_rows = (n_mem * XATTN_HEADS, XATTN_HEAD_DIM)
    mem_k_s = cache_mem_k.reshape((depth, bs) + mem_rows)
    mem_v_s = cache_mem_v.reshape((depth, bs) + mem_rows)

    xp = x_prompt.reshape(bp * tp, d)
    xs = x_sample.reshape(bs * ts, d)
    mem = mem_prompt.reshape(bp * n_mem, d)
    p_new, s_new, p_mk, p_mv = [], [], [], []
    bufs_p = _new_state_buffers(depth, bp, tp)
    bufs_s = _new_state_buffers(depth, bs, ts)
    for l in range(depth):
        lw = _layer_weights(l, wts)
        mk, mv = _memkv(mem, lw["g_mem"], lw["w_xkv"], _tile(bp * n_mem, 256))
        xp, st_p, bufs_p = _layer(xp, lw, l, bp, tp, None, mk.reshape((1, bp) + mem_rows),
                                  mv.reshape((1, bp) + mem_rows), 0, tabs_p, bufs_p)
        xs, st_s, bufs_s = _layer(xs, lw, l, bs, ts, past_s, mem_k_s, mem_v_s, l, tabs_s, bufs_s)
        p_new.append(st_p)
        s_new.append(st_s)
        p_mk.append(mk.reshape(bp, n_mem, XATTN_HEADS, XATTN_HEAD_DIM))
        p_mv.append(mv.reshape(bp, n_mem, XATTN_HEADS, XATTN_HEAD_DIM))
    p_out = _assemble_states(p_new, bufs_p, bp, tp)
    s_out = _assemble_states(s_new, bufs_s, bs, ts)
    return (xp.reshape(bp, tp, d), xs.reshape(bs, ts, d)) + p_out + (jnp.stack(p_mk), jnp.stack(p_mv)) + s_out
```
